```python
import jax, jax.numpy as jnp
from jax import lax
import numpy as np

D_MODEL = 1024
BATCH = 2
SEQ = 8192
DEPTH = 4

D_MIX = 2 * D_MODEL
SSD_HEADS = 16
SSD_INNER = D_MIX // 2
SSD_HEAD_DIM = SSD_INNER // SSD_HEADS
SSD_GROUPS = 2
SSD_STATE = 128
SSD_XBC = SSD_INNER + 2 * SSD_GROUPS * SSD_STATE
SSD_CHUNK = 128
HG_HEADS = 4
HG_WIDTH = D_MIX // 4
HG_DK = HG_WIDTH // HG_HEADS
HG_CHUNK = 64
ML_HEADS = 4
ML_WIDTH = D_MIX // 4
ML_DH = ML_WIDTH // ML_HEADS
ML_CHUNK = 64
CONV_WIDTH = 5
D_FF_DENSE = 256 * ((8 * D_MODEL // 3 + 255) // 256)
N_EXPERTS = 8
TOP_K = 2
D_FF_EXPERT = 7 * D_MODEL // 2
MOE_BLOCK = 256
LN_EPS = 1e-5
RMS_EPS = 1e-6
MASK_NEG = -1e4
PROJ_SIZES = (SSD_INNER, SSD_XBC, 2 * SSD_HEADS,
              HG_WIDTH, 2 * HG_WIDTH, HG_WIDTH, HG_WIDTH,
              2 * ML_WIDTH, ML_WIDTH, ML_WIDTH, 2 * ML_HEADS, 2 * ML_HEADS)
D_PROJ = sum(PROJ_SIZES)

kernel_name = "bidir_hybrid_ssd_hgrn2_mlstm_moe"


def _split_points():
    pts, acc = [], 0
    for s in PROJ_SIZES[:-1]:
        acc += s
        pts.append(acc)
    return pts


def _flip(t):
    return jnp.flip(t, axis=1)


def layer_norm(x, g, b):
    xf = x.astype(jnp.float32)
    mu = jnp.mean(xf, -1, keepdims=True)
    xc = xf - mu
    var = jnp.mean(jnp.square(xc), -1, keepdims=True)
    return (xc * lax.rsqrt(var + LN_EPS) * g.astype(jnp.float32) + b.astype(jnp.float32)).astype(x.dtype)


def group_rms_norm(x, w, n_groups):
    shp = x.shape
    xg = x.reshape(shp[:-1] + (n_groups, shp[-1] // n_groups))
    xg = xg * lax.rsqrt(jnp.mean(jnp.square(xg), -1, keepdims=True) + RMS_EPS)
    return xg.reshape(shp) * w


def group_layer_norm(x, w, n_groups):
    shp = x.shape
    xg = x.reshape(shp[:-1] + (n_groups, shp[-1] // n_groups))
    xc = xg - jnp.mean(xg, -1, keepdims=True)
    xg = xc * lax.rsqrt(jnp.mean(jnp.square(xc), -1, keepdims=True) + LN_EPS)
    return xg.reshape(shp) * w


def centred_depthwise_conv(x, w, b):
    pad = w.shape[0] // 2
    y = lax.conv_general_dilated(x, w[:, None, :], window_strides=(1,), padding=[(pad, pad)],
                                 dimension_numbers=('NWC', 'WIO', 'NWC'),
                                 feature_group_count=x.shape[-1])
    return y + b


def ssd_scan(xh, dt, A, Bm, Cm):
    b, L, H, P = xh.shape
    G, N = Bm.shape[-2:]
    R = H // G
    nc = L // SSD_CHUNK
    X = (xh * dt[..., None]).reshape(b, nc, SSD_CHUNK, G, R, P)
    a = (dt * A).reshape(b, nc, SSD_CHUNK, G, R)
    Bc = Bm.reshape(b, nc, SSD_CHUNK, G, N)
    Cc = Cm.reshape(b, nc, SSD_CHUNK, G, N)
    a_cs = jnp.cumsum(a, axis=2)
    acs_t = jnp.moveaxis(a_cs, 2, -1)
    seg = acs_t[..., :, None] - acs_t[..., None, :]
    tri = jnp.tril(jnp.ones((SSD_CHUNK, SSD_CHUNK), bool))
    decay = jnp.exp(jnp.where(tri, seg, MASK_NEG))
    cb = jnp.einsum('bclgn,bcsgn->bcgls', Cc, Bc)
    y_diag = jnp.einsum('bcgrls,bcsgrp->bclgrp', cb[:, :, :, None] * decay, X)
    x_to_end = X * jnp.exp(a_cs[:, :, -1:] - a_cs)[..., None]
    chunk_states = jnp.einsum('bclgn,bclgrp->bcgrpn', Bc, x_to_end)
    chunk_decay = jnp.exp(a_cs[:, :, -1])

    def step(state, inp):
        st, dec = inp
        return state * dec[..., None, None] + st, state

    init = jnp.zeros((b, G, R, P, N), X.dtype)
    _, prev = lax.scan(step, init, (jnp.moveaxis(chunk_states, 1, 0), jnp.moveaxis(chunk_decay, 1, 0)))
    prev = jnp.moveaxis(prev, 0, 1)
    y_off = jnp.einsum('bclgn,bcgrpn->bclgrp', Cc, prev) * jnp.exp(a_cs)[..., None]
    return (y_diag + y_off).reshape(b, L, H, P)


def ssd_mixer(z, xbc, dt_raw, conv_w, conv_b, dt_bias, a_log, d_skip, norm_w):
    xbc = jax.nn.silu(centred_depthwise_conv(xbc, conv_w, conv_b))
    xs, Bm, Cm = jnp.split(xbc, [SSD_INNER, SSD_INNER + SSD_GROUPS * SSD_STATE], axis=-1)
    b, L, _ = xs.shape
    xh = xs.reshape(b, L, SSD_HEADS, SSD_HEAD_DIM)
    Bm = Bm.reshape(b, L, SSD_GROUPS, SSD_STATE)
    Cm = Cm.reshape(b, L, SSD_GROUPS, SSD_STATE)
    dt = jax.nn.softplus(dt_raw.reshape(b, L, 2, SSD_HEADS) + dt_bias)
    A = -jnp.exp(a_log)
    y_f = ssd_scan(xh, dt[:, :, 0], A[0], Bm, Cm)
    y_b = _flip(ssd_scan(_flip(xh), _flip(dt[:, :, 1]), A[1], _flip(Bm), _flip(Cm)))
    y = (y_f + y_b + xh * d_skip[:, None]).reshape(b, L, SSD_INNER)
    return group_rms_norm(y * jax.nn.silu(z), norm_w, SSD_GROUPS)


def hgrn2_scan(q, k, v, g):
    b, L, H, K = q.shape
    V = v.shape[-1]
    nc = L // HG_CHUNK

    def to_chunks(t):
        return t.reshape(b, nc, HG_CHUNK, H, t.shape[-1]).transpose(1, 0, 3, 2, 4)

    tri = jnp.tril(jnp.ones((HG_CHUNK, HG_CHUNK), bool))

    def step(S, inp):
        qc, kc, vc, gc = inp
        G = jnp.cumsum(gc, axis=2)
        rel = G[:, :, :, None, :] - G[:, :, None, :, :]
        rel = jnp.exp(jnp.where(tri[:, :, None], rel, MASK_NEG))
        att = jnp.einsum('bhtk,bhsk,bhtsk->bhts', qc, kc, rel)
        o = jnp.einsum('bhts,bhsv->bhtv', att, vc) + jnp.einsum('bhtk,bhkv->bhtv', qc * jnp.exp(G), S)
        G_last = G[:, :, -1:]
        S = S * jnp.exp(G_last[:, :, 0])[..., None] + jnp.einsum('bhsk,bhsv->bhkv', kc * jnp.exp(G_last - G), vc)
        return S, o

    S0 = jnp.zeros((b, H, K, V), q.dtype)
    _, o = lax.scan(step, S0, (to_chunks(q), to_chunks(k), to_chunks(v), to_chunks(g)))
    return o.transpose(1, 0, 3, 2, 4).reshape(b, L, H, V)


def hgrn2_mixer(q, f_raw, i, g_out, lb, norm_w):
    b, L, _ = q.shape
    qh = jax.nn.silu(q).reshape(b, L, HG_HEADS, HG_DK)
    vh = i.reshape(b, L, HG_HEADS, HG_DK)
    lbh = lb.reshape(HG_HEADS, HG_DK)
    fr = f_raw.reshape(b, L, 2, HG_HEADS, HG_DK)

    def gates(f):
        key = (1.0 - lbh) * jax.nn.sigmoid(-f)
        log_f = jnp.log1p(-key)
        return key, log_f

    k_f, g_f = gates(fr[:, :, 0])
    k_b, g_b = gates(fr[:, :, 1])
    o_f = hgrn2_scan(qh, k_f, vh, g_f)
    o_b = _flip(hgrn2_scan(_flip(qh), _flip(k_b), _flip(vh), _flip(g_b)))
    o = (o_f + o_b).reshape(b, L, HG_WIDTH)
    return group_rms_norm(o, norm_w, HG_HEADS) * jax.nn.silu(g_out)


def mlstm_scan(q, k, v, log_i, log_f):
    b, L, H, D = q.shape
    nc = L // ML_CHUNK

    def to_chunks(t):
        return t.reshape(b, nc, ML_CHUNK, H, D).transpose(1, 0, 3, 2, 4)

    def gate_chunks(t):
        return t.reshape(b, nc, ML_CHUNK, H).transpose(1, 0, 3, 2)

    tri = jnp.tril(jnp.ones((ML_CHUNK, ML_CHUNK), bool))

    def step(carry, inp):
        C, n, m = carry
        qc, kc, vc, ic, fc = inp
        bcum = jnp.cumsum(fc, axis=-1)
        dmat = jnp.where(tri, bcum[..., :, None] - bcum[..., None, :] + ic[..., None, :], MASK_NEG)
        a_inter = bcum + m[..., None]
        m_t = jnp.maximum(jnp.max(dmat, -1), a_inter)
        w_intra = jnp.exp(jnp.where(tri, dmat - m_t[..., None], MASK_NEG))
        w_inter = jnp.exp(a_inter - m_t)
        s = jnp.einsum('bhtd,bhsd->bhts', qc, kc) * w_intra
        num = jnp.einsum('bhts,bhsd->bhtd', s, vc) + w_inter[..., None] * jnp.einsum('bhtk,bhkv->bhtv', qc, C)
        den = jnp.sum(s, -1) + w_inter * jnp.einsum('bhtk,bhk->bht', qc, n)
        h = num / jnp.maximum(jnp.abs(den), jnp.exp(-m_t))[..., None]
        b_last = bcum[..., -1]
        log_w = b_last[..., None] - bcum + ic
        m_new = jnp.maximum(b_last + m, jnp.max(log_w, -1))
        w_s = jnp.exp(log_w - m_new[..., None])
        dec = jnp.exp(b_last + m - m_new)
        C = dec[..., None, None] * C + jnp.einsum('bhsk,bhsv->bhkv', kc * w_s[..., None], vc)
        n = dec[..., None] * n + jnp.einsum('bhsk,bhs->bhk', kc, w_s)
        return (C, n, m_new), h

    init = (jnp.zeros((b, H, D, D), q.dtype), jnp.zeros((b, H, D), q.dtype), jnp.zeros((b, H), q.dtype))
    _, h = lax.scan(step, init, (to_chunks(q), to_chunks(k), to_chunks(v), gate_chunks(log_i), gate_chunks(log_f)))
    return h.transpose(1, 0, 3, 2, 4).reshape(b, L, H, D)


def mlstm_mixer(qk, v, o_raw, ig_raw, fg_raw, conv_w, conv_b, ig_bias, fg_bias, norm_w):
    b, L, _ = v.shape
    qk = jax.nn.silu(centred_depthwise_conv(qk, conv_w, conv_b))
    q, k = jnp.split(qk, 2, axis=-1)
    qh = q.reshape(b, L, ML_HEADS, ML_DH) * (ML_DH ** -0.5)
    kh = k.reshape(b, L, ML_HEADS, ML_DH)
    vh = v.reshape(b, L, ML_HEADS, ML_DH)
    log_i = ig_raw.reshape(b, L, 2, ML_HEADS) + ig_bias
    log_f = jax.nn.log_sigmoid(fg_raw.reshape(b, L, 2, ML_HEADS) + fg_bias)
    h_f = mlstm_scan(qh, kh, vh, log_i[:, :, 0], log_f[:, :, 0])
    h_b = _flip(mlstm_scan(_flip(qh), _flip(kh), _flip(vh), _flip(log_i[:, :, 1]), _flip(log_f[:, :, 1])))
    h = (h_f + h_b).reshape(b, L, ML_WIDTH)
    return group_layer_norm(h, norm_w, ML_HEADS) * jax.nn.sigmoid(o_raw)


def swiglu(x, w1, w3, w2):
    return jnp.dot(jax.nn.silu(jnp.dot(x, w1)) * jnp.dot(x, w3), w2)


def moe_swiglu(x, router_w, w1, w3, w2):
    b, L, D = x.shape
    T = b * L
    xf = x.reshape(T, D)
    logits = jnp.dot(xf, router_w).astype(jnp.float32)
    top_vals, top_idx = lax.top_k(logits, TOP_K)
    gates = jax.nn.softmax(top_vals, axis=-1).astype(x.dtype)
    expert_ids = top_idx.reshape(-1).astype(jnp.int32)
    token_ids = jnp.repeat(jnp.arange(T, dtype=jnp.int32), TOP_K)
    flat_gates = gates.reshape(-1)
    order = jnp.argsort(expert_ids)
    sorted_e = expert_ids[order]
    counts = jnp.zeros((N_EXPERTS,), jnp.int32).at[expert_ids].add(1)
    padded = (counts + MOE_BLOCK - 1) // MOE_BLOCK * MOE_BLOCK
    start = jnp.cumsum(counts) - counts
    ends = jnp.cumsum(padded)
    pstart = ends - padded
    rank = jnp.arange(T * TOP_K, dtype=jnp.int32) - start[sorted_e]
    dest = pstart[sorted_e] + rank
    n_blocks = -(-(T * TOP_K) // MOE_BLOCK) + N_EXPERTS
    n_rows = n_blocks * MOE_BLOCK
    row_token = jnp.zeros((n_rows,), jnp.int32).at[dest].set(token_ids[order])
    row_gate = jnp.zeros((n_rows,), x.dtype).at[dest].set(flat_gates[order])
    block_start = jnp.arange(n_blocks, dtype=jnp.int32) * MOE_BLOCK
    block_expert = jnp.minimum(jnp.searchsorted(ends, block_start, side='right'), N_EXPERTS - 1)
    xs = xf[row_token].reshape(n_blocks, MOE_BLOCK, D)

    def expert_block(args):
        xb, e = args
        return jnp.dot(jax.nn.silu(jnp.dot(xb, w1[e])) * jnp.dot(xb, w3[e]), w2[e])

    ys = lax.map(expert_block, (xs, block_expert)).reshape(n_rows, D)
    out = jnp.zeros((T, D), x.dtype).at[row_token].add(ys * row_gate[:, None])
    return out.reshape(b, L, D)


def setup_inputs(seed: int = 0) -> dict:
    key = jax.random.key(seed)
    ks = jax.random.split(key, 40)
    f32 = jnp.float32
    nrm = lambda k, shape, scale: jax.random.normal(k, shape, f32) * scale
    beta = (8.0 * DEPTH) ** -0.25
    n_dense = (DEPTH + 1) // 2
    n_moe = DEPTH // 2
    dt0 = jnp.exp(jax.random.uniform(ks[6], (DEPTH, 2, SSD_HEADS), f32, np.log(1e-3), np.log(1e-1)))
    fg_base = jnp.linspace(3.0, 6.0, ML_HEADS, dtype=f32)
    return {
        'x': nrm(ks[0], (BATCH, SEQ, D_MODEL), 1.0),
        'ln_in_g': 1.0 + nrm(ks[1], (D_MODEL,), 0.02),
        'ln_in_b': nrm(ks[2], (D_MODEL,), 0.02),
        'w_in': nrm(ks[3], (DEPTH, D_MODEL, D_PROJ), D_MODEL ** -0.5),
        'ssd_conv_w': nrm(ks[4], (DEPTH, CONV_WIDTH, SSD_XBC), CONV_WIDTH ** -0.5),
        'ssd_conv_b': nrm(ks[5], (DEPTH, SSD_XBC), 0.02),
        'ssd_dt_bias': dt0 + jnp.log(-jnp.expm1(-dt0)),
        'ssd_a_log': jnp.log(jax.random.uniform(ks[7], (DEPTH, 2, SSD_HEADS), f32, 1.0, 16.0)),
        'ssd_d': 1.0 + nrm(ks[8], (DEPTH, SSD_HEADS), 0.1),
        'ssd_norm_w': 1.0 + nrm(ks[9], (DEPTH, SSD_INNER), 0.02),
        'hg_lb_logits': nrm(ks[10], (DEPTH, HG_WIDTH), 0.1),
        'hg_norm_w': 1.0 + nrm(ks[11], (DEPTH, HG_WIDTH), 0.02),
        'ml_conv_w': nrm(ks[12], (DEPTH, CONV_WIDTH, 2 * ML_WIDTH), CONV_WIDTH ** -0.5),
        'ml_conv_b': nrm(ks[13], (DEPTH, 2 * ML_WIDTH), 0.02),
        'ml_ig_bias': nrm(ks[14], (DEPTH, 2, ML_HEADS), 0.1),
        'ml_fg_bias': fg_base + nrm(ks[15], (DEPTH, 2, ML_HEADS), 0.1),
        'ml_norm_w': 1.0 + nrm(ks[16], (DEPTH, ML_WIDTH), 0.02),
        'w_out': nrm(ks[17], (DEPTH, D_MIX, D_MODEL), D_MIX ** -0.5 * beta),
        'ln1_g': 1.0 + nrm(ks[18], (DEPTH, D_MODEL), 0.02),
        'ln1_b': nrm(ks[19], (DEPTH, D_MODEL), 0.02),
        'ln2_g': 1.0 + nrm(ks[20], (DEPTH, D_MODEL), 0.02),
        'ln2_b': nrm(ks[21], (DEPTH, D_MODEL), 0.02),
        'ffn_w1': nrm(ks[22], (n_dense, D_MODEL, D_FF_DENSE), D_MODEL ** -0.5),
        'ffn_w3': nrm(ks[23], (n_dense, D_MODEL, D_FF_DENSE), D_MODEL ** -0.5),
        'ffn_w2': nrm(ks[24], (n_dense, D_FF_DENSE, D_MODEL), D_FF_DENSE ** -0.5 * beta),
        'moe_router': nrm(ks[25], (n_moe, D_MODEL, N_EXPERTS), D_MODEL ** -0.5),
        'moe_w1': nrm(ks[26], (n_moe, N_EXPERTS, D_MODEL, D_FF_EXPERT), D_MODEL ** -0.5),
        'moe_w3': nrm(ks[27], (n_moe, N_EXPERTS, D_MODEL, D_FF_EXPERT), D_MODEL ** -0.5),
        'moe_w2': nrm(ks[28], (n_moe, N_EXPERTS, D_FF_EXPERT, D_MODEL), D_FF_EXPERT ** -0.5 * beta),
    }


def reference(x, ln_in_g, ln_in_b, w_in, ssd_conv_w, ssd_conv_b, ssd_dt_bias, ssd_a_log, ssd_d,
              ssd_norm_w, hg_lb_logits, hg_norm_w, ml_conv_w, ml_conv_b, ml_ig_bias, ml_fg_bias,
              ml_norm_w, w_out, ln1_g, ln1_b, ln2_g, ln2_b, ffn_w1, ffn_w3, ffn_w2,
              moe_router, moe_w1, moe_w3, moe_w2):
    f32 = lambda t: t.astype(jnp.float32)
    alpha = (2.0 * DEPTH) ** 0.25
    lb_soft = jax.nn.softmax(f32(hg_lb_logits), axis=0)
    lb_all = jnp.cumsum(lb_soft, axis=0) - lb_soft[0]
    pts = _split_points()
    h = layer_norm(x, ln_in_g, ln_in_b)
    for l in range(DEPTH):
        proj = f32(jnp.dot(h, w_in[l]))
        (z, xbc, dt_raw, hq, hf, hi, hg, mqk, mv, mo, mig, mfg) = jnp.split(proj, pts, axis=-1)
        y_ssd = ssd_mixer(z, xbc, dt_raw, f32(ssd_conv_w[l]), f32(ssd_conv_b[l]), f32(ssd_dt_bias[l]),
                          f32(ssd_a_log[l]), f32(ssd_d[l]), f32(ssd_norm_w[l]))
        y_hg = hgrn2_mixer(hq, hf, hi, hg, lb_all[l], f32(hg_norm_w[l]))
        y_ml = mlstm_mixer(mqk, mv, mo, mig, mfg, f32(ml_conv_w[l]), f32(ml_conv_b[l]),
                           f32(ml_ig_bias[l]), f32(ml_fg_bias[l]), f32(ml_norm_w[l]))
        mix = jnp.concatenate([y_ssd, y_hg, y_ml], axis=-1).astype(h.dtype)
        h = layer_norm(alpha * h + jnp.dot(mix, w_out[l]), ln1_g[l], ln1_b[l])
        if l % 2 == 0:
            f = swiglu(h, ffn_w1[l // 2], ffn_w3[l // 2], ffn_w2[l // 2])
        else:
            f = moe_swiglu(h, moe_router[l // 2], moe_w1[l // 2], moe_w3[l // 2], moe_w2[l // 2])
        h = layer_norm(alpha * h + f, ln2_g[l], ln2_b[l])
    return h
```

```python
import functools

import jax
import jax.numpy as jnp
from jax import lax
from jax.experimental import pallas as pl
from jax.experimental.pallas import tpu as pltpu

F32 = jnp.float32
BF16 = jnp.bfloat16
HIGHEST = lax.Precision.HIGHEST

D_MODEL = 1024
DEPTH = 4
D_MIX = 2 * D_MODEL
SSD_HEADS = 16
SSD_INNER = 1024
SSD_HEAD_DIM = 64
SSD_GROUPS = 2
SSD_STATE = 128
SSD_XBC = 1536
SSD_CHUNK = 128
HG_HEADS = 4
HG_WIDTH = 512
HG_DK = 128
ML_HEADS = 4
ML_WIDTH = 512
ML_DH = 128
CONV_WIDTH = 5
D_FF_DENSE = 2816
N_EXPERTS = 8
TOP_K = 2
D_FF_EXPERT = 3584
LN_EPS = 1e-5
RMS_EPS = 1e-6
MASK_NEG = -1e4
ALPHA = (2.0 * DEPTH) ** 0.25

LANES = 128
SUBLANES = 8
VMEM_LIMIT = 56 * 1024 * 1024

D_PROJ_PAD = 7296
COLBLK_Z = 0
COLBLK_MQK = 1
COLBLK_XBC = 2
COLBLK_HF_FWD = 4
COLBLK_HF_BWD = 5
COLBLK_HQ = 9
COLBLK_HI = 10
COLBLK_HG = 11
COLBLK_MV = 12
COLBLK_MO = 13
COLBLK_GATES = 56
GATE_DT = 0
GATE_IG = 32
GATE_FG = 40

TM_LN = 512
TM_INPROJ = 256
TM_CONV = 512
SCAN_STEP = 128
HG_SUB = 64
HG_REFBLK = 4
TM_OUT = 256
TM_FFN = 512
TF_FFN = 256
TM_ROUTER = 512
MOE_BM = 512
MOE_TF = 512
GATHER_ROWS = 2048
TM_COMBINE = 256


def _cparams(sem):
    return pltpu.CompilerParams(dimension_semantics=sem, vmem_limit_bytes=VMEM_LIMIT)


def _iota(shape, dim):
    return lax.broadcasted_iota(jnp.int32, shape, dim)


def _sigmoid(x):
    return 1.0 / (1.0 + jnp.exp(-x))


def _silu(x):
    return x * _sigmoid(x)


def _softplus(x):
    return jnp.maximum(x, 0.0) + jnp.log1p(jnp.exp(-jnp.abs(x)))


def _log_sigmoid(x):
    return -_softplus(-x)


def _layer_norm(x, g, b):
    mu = jnp.mean(x, -1, keepdims=True)
    xc = x - mu
    var = jnp.mean(xc * xc, -1, keepdims=True)
    return xc * lax.rsqrt(var + LN_EPS) * g + b


def _causal(n, reverse):
    r = _iota((n, n), 0)
    c = _iota((n, n), 1)
    return (c >= r) if reverse else (c <= r)


def _dot(a, b):
    return jnp.dot(a, b, preferred_element_type=F32)


def _dot_nt(a, b):
    return lax.dot_general(a, b, (((1,), (1,)), ((), ())), preferred_element_type=F32)


def _dot_tn(a, b):
    return lax.dot_general(a, b, (((0,), (0,)), ((), ())), preferred_element_type=F32)


def _dot_hi(a, b):
    return jnp.dot(a, b, precision=HIGHEST, preferred_element_type=F32)


def _dot_nt_hi(a, b):
    return lax.dot_general(a, b, (((1,), (1,)), ((), ())), precision=HIGHEST,
                           preferred_element_type=F32)


def _ln_kernel(x_ref, g_ref, b_ref, o_ref):
    o_ref[...] = _layer_norm(x_ref[...], g_ref[...], b_ref[...])


def _input_ln(x, g, b):
    T, D = x.shape
    return pl.pallas_call(
        _ln_kernel,
        out_shape=jax.ShapeDtypeStruct((T, D), F32),
        grid=(T // TM_LN,),
        in_specs=[pl.BlockSpec((TM_LN, D), lambda i: (i, 0)),
                  pl.BlockSpec((1, D), lambda i: (0, 0)),
                  pl.BlockSpec((1, D), lambda i: (0, 0))],
        out_specs=pl.BlockSpec((TM_LN, D), lambda i: (i, 0)),
        compiler_params=_cparams(("parallel",)),
        name="input_ln",
    )(x, g.reshape(1, D), b.reshape(1, D))


def _inproj_kernel(h_ref, w_ref, wgt_ref, proj_ref, gt_ref):
    hb = h_ref[...].astype(BF16)
    proj_ref[...] = _dot(hb, w_ref[...])
    gt_ref[...] = _dot_nt(wgt_ref[...], hb)


def _inproj(h, w, wgt):
    T, D = h.shape
    tm = TM_INPROJ
    return pl.pallas_call(
        _inproj_kernel,
        out_shape=(jax.ShapeDtypeStruct((T, D_PROJ_PAD), F32),
                   jax.ShapeDtypeStruct((LANES, T), F32)),
        grid=(T // tm,),
        in_specs=[pl.BlockSpec((tm, D), lambda i: (i, 0)),
                  pl.BlockSpec((D, D_PROJ_PAD), lambda i: (0, 0), pipeline_mode=pl.Buffered(1)),
                  pl.BlockSpec((LANES, D), lambda i: (0, 0), pipeline_mode=pl.Buffered(1))],
        out_specs=(pl.BlockSpec((tm, D_PROJ_PAD), lambda i: (i, 0)),
                   pl.BlockSpec((LANES, tm), lambda i: (0, i))),
        compiler_params=_cparams(("parallel",)),
        name="inproj",
    )(h, w, wgt)


def _conv_kernel(prev_ref, x_ref, next_ref, w_ref, b_ref, o_ref, *, tm, tiles_per_seq):
    i = pl.program_id(0)
    pos = i % tiles_per_seq
    x = x_ref[...]
    prev = jnp.where(pos == 0, 0.0, prev_ref[...])
    nxt = jnp.where(pos == tiles_per_seq - 1, 0.0, next_ref[...])
    xe = jnp.concatenate([prev, x, nxt], axis=0)
    n = tm + 2 * SUBLANES
    acc = b_ref[...] + w_ref[2:3, :] * x
    for k in (0, 1, 3, 4):
        shifted = pltpu.roll(xe, (2 - k) % n, 0)
        acc = acc + w_ref[k:k + 1, :] * shifted[SUBLANES:SUBLANES + tm]
    o_ref[...] = _silu(acc)


def _conv_silu(proj, w, b, colblk, width, seq_len):
    T = proj.shape[0]
    tm = TM_CONV
    rb = tm // SUBLANES
    nrb = T // SUBLANES
    kern = functools.partial(_conv_kernel, tm=tm, tiles_per_seq=seq_len // tm)
    return pl.pallas_call(
        kern,
        out_shape=jax.ShapeDtypeStruct((T, width), F32),
        grid=(T // tm,),
        in_specs=[pl.BlockSpec((SUBLANES, width), lambda i: (jnp.maximum(i * rb - 1, 0), colblk)),
                  pl.BlockSpec((tm, width), lambda i: (i, colblk)),
                  pl.BlockSpec((SUBLANES, width), lambda i: (jnp.minimum((i + 1) * rb, nrb - 1), colblk)),
                  pl.BlockSpec((CONV_WIDTH, width), lambda i: (0, 0)),
                  pl.BlockSpec((1, width), lambda i: (0, 0))],
        out_specs=pl.BlockSpec((tm, width), lambda i: (i, 0)),
        compiler_params=_cparams(("parallel",)),
        name="conv_silu",
    )(proj, proj, proj, w, b.reshape(1, width))


def _ssd_direction(x, gates, gates_t, bias_row, alog_row, bias_cr, alog_cr, st_ref, d, y_ref):
    reverse = d == 1
    n = SSD_CHUNK
    tri = _causal(n, reverse)
    lmat = tri.astype(F32)
    last = 0 if reverse else n - 1
    lane_lo = _iota((n, LANES), 1) < SSD_HEAD_DIM

    dt_c = _softplus(gates + bias_row)
    a_c = dt_c * (-jnp.exp(alog_row))
    acs_c = _dot_hi(lmat, a_c)
    dt_r = _softplus(gates_t + bias_cr)
    a_r = dt_r * (-jnp.exp(alog_cr))
    acs_r = _dot_nt_hi(a_r, lmat)
    a_last = acs_c[last:last + 1, :]
    ea_c = jnp.exp(acs_c)
    w_c = dt_c * jnp.exp(a_last - acs_c)
    dec_row = jnp.exp(a_last)

    for g in range(SSD_GROUPS):
        b0 = SSD_INNER + g * SSD_STATE
        c0 = SSD_INNER + SSD_GROUPS * SSD_STATE + g * SSD_STATE
        bg = x[:, b0:b0 + SSD_STATE]
        cg = x[:, c0:c0 + SSD_STATE]
        cb = _dot_nt(cg.astype(BF16), bg.astype(BF16))
        for pair in range(4):
            p0 = (g * 4 + pair) * LANES
            xp = x[:, p0:p0 + LANES]
            ypair = None
            for hh in range(2):
                h = g * 8 + pair * 2 + hh
                c = GATE_DT + d * SSD_HEADS + h
                xm = jnp.where(lane_lo if hh == 0 else jnp.logical_not(lane_lo), xp, 0.0).astype(BF16)
                col = acs_c[:, c:c + 1]
                row = acs_r[h:h + 1, :]
                decay = jnp.exp(jnp.where(tri, col - row, MASK_NEG))
                m = (cb * decay * dt_r[h:h + 1, :]).astype(BF16)
                ch = (cg * ea_c[:, c:c + 1]).astype(BF16)
                st = st_ref[d, h]
                y = _dot(m, xm) + _dot(ch, st.astype(BF16))
                bh = (bg * w_c[:, c:c + 1]).astype(BF16)
                st_ref[d, h] = st * dec_row[0:1, c:c + 1] + _dot_tn(bh, xm)
                ypair = y if ypair is None else ypair + y
            y_ref[:, p0:p0 + LANES] = ypair


def _ssd_kernel(xf_ref, xb_ref, gf_ref, gb_ref, gtf_ref, gtb_ref, brow_ref, arow_ref,
                bcr_ref, acr_ref, yf_ref, yb_ref, st_ref):
    @pl.when(pl.program_id(1) == 0)
    def _():
        st_ref[...] = jnp.zeros_like(st_ref)

    _ssd_direction(xf_ref[...], gf_ref[...], gtf_ref[...], brow_ref[...], arow_ref[...],
                   bcr_ref[0:SSD_HEADS, :], acr_ref[0:SSD_HEADS, :], st_ref, 0, yf_ref)
    _ssd_direction(xb_ref[...], gb_ref[...], gtb_ref[...], brow_ref[...], arow_ref[...],
                   bcr_ref[SSD_HEADS:2 * SSD_HEADS, :], acr_ref[SSD_HEADS:2 * SSD_HEADS, :],
                   st_ref, 1, yb_ref)


def _ssd_scan(xbc_act, proj, gates_t, bias_row, alog_row, bias_cr, alog_cr, batch):
    T = xbc_act.shape[0]
    n = T // batch // SSD_CHUNK
    fwd = lambda b, i: b * n + i
    bwd = lambda b, i: b * n + (n - 1 - i)
    const = lambda b, i: (0, 0)
    return pl.pallas_call(
        _ssd_kernel,
        out_shape=(jax.ShapeDtypeStruct((T, SSD_INNER), F32),
                   jax.ShapeDtypeStruct((T, SSD_INNER), F32)),
        grid=(batch, n),
        in_specs=[pl.BlockSpec((SSD_CHUNK, SSD_XBC), lambda b, i: (fwd(b, i), 0)),
                  pl.BlockSpec((SSD_CHUNK, SSD_XBC), lambda b, i: (bwd(b, i), 0)),
                  pl.BlockSpec((SSD_CHUNK, LANES), lambda b, i: (fwd(b, i), COLBLK_GATES)),
                  pl.BlockSpec((SSD_CHUNK, LANES), lambda b, i: (bwd(b, i), COLBLK_GATES)),
                  pl.BlockSpec((SSD_HEADS, SSD_CHUNK), lambda b, i: (0, fwd(b, i))),
                  pl.BlockSpec((SSD_HEADS, SSD_CHUNK), lambda b, i: (1, bwd(b, i))),
                  pl.BlockSpec((1, LANES), const),
                  pl.BlockSpec((1, LANES), const),
                  pl.BlockSpec((2 * SSD_HEADS, LANES), const),
                  pl.BlockSpec((2 * SSD_HEADS, LANES), const)],
        out_specs=(pl.BlockSpec((SSD_CHUNK, SSD_INNER), lambda b, i: (fwd(b, i), 0)),
                   pl.BlockSpec((SSD_CHUNK, SSD_INNER), lambda b, i: (bwd(b, i), 0))),
        scratch_shapes=[pltpu.VMEM((2, SSD_HEADS, SSD_STATE, LANES), F32)],
        compiler_params=_cparams(("arbitrary", "arbitrary")),
        name="ssd_scan",
    )(xbc_act, xbc_act, proj, proj, gates_t, gates_t, bias_row, alog_row, bias_cr, alog_cr)


def _hgrn2_subchunk(q, f, v, lb, st_ref, d, h):
    reverse = d == 1
    n = HG_SUB
    nblk = n // HG_REFBLK
    tri = _causal(n, reverse)
    lmat = tri.astype(F32)
    last = 0 if reverse else n - 1
    r_i = _iota((n, n), 0)
    c_i = _iota((n, n), 1)
    ref_of_row = (r_i // HG_REFBLK) * HG_REFBLK + (HG_REFBLK - 1 if reverse else 0)
    lref = ((c_i >= ref_of_row) if reverse else (c_i <= ref_of_row)).astype(F32)
    rowblk = _iota((n, HG_DK), 0) // HG_REFBLK

    qh = _silu(q)
    key = (1.0 - lb) / (1.0 + jnp.exp(f))
    g = jnp.log1p(-key)
    gc = _dot_hi(lmat, g)
    gref = _dot_hi(lref, g)
    qg = qh * jnp.exp(gc - gref)
    q_parts = []
    k_parts = []
    for j in range(nblk):
        ref_row = j * HG_REFBLK + (HG_REFBLK - 1 if reverse else 0)
        rj = gc[ref_row:ref_row + 1, :]
        live = (rowblk >= j) if reverse else (rowblk <= j)
        k_parts.append((key * jnp.exp(jnp.where(live, rj - gc, MASK_NEG))).astype(BF16))
        q_parts.append(jnp.where(rowblk == j, qg, 0.0).astype(BF16))
    att = _dot_nt(jnp.concatenate(q_parts, axis=1), jnp.concatenate(k_parts, axis=1))
    att = jnp.where(tri, att, 0.0)
    st = st_ref[d, h]
    vb = v.astype(BF16)
    o = _dot(att.astype(BF16), vb) + _dot_nt((qh * jnp.exp(gc)).astype(BF16), st.astype(BF16))
    g_last = gc[last:last + 1, :]
    kd = (key * jnp.exp(g_last - gc)).astype(BF16)
    st_ref[d, h] = st * jnp.exp(g_last) + _dot_tn(vb, kd)
    return o


def _hgrn2_kernel(qf_ref, ff_ref, vf_ref, qb_ref, fb_ref, vb_ref, lb_ref, of_ref, ob_ref, st_ref):
    @pl.when(pl.program_id(1) == 0)
    def _():
        st_ref[...] = jnp.zeros_like(st_ref)

    nsub = SCAN_STEP // HG_SUB
    for d, (q_ref, f_ref, v_ref, o_ref) in enumerate(((qf_ref, ff_ref, vf_ref, of_ref),
                                                      (qb_ref, fb_ref, vb_ref, ob_ref))):
        order = range(nsub) if d == 0 else range(nsub - 1, -1, -1)
        for sc in order:
            rows = slice(sc * HG_SUB, (sc + 1) * HG_SUB)
            for h in range(HG_HEADS):
                cols = slice(h * HG_DK, (h + 1) * HG_DK)
                o_ref[rows, cols] = _hgrn2_subchunk(q_ref[rows, cols], f_ref[rows, cols],
                                                    v_ref[rows, cols], lb_ref[:, cols], st_ref, d, h)


def _hgrn2_scan(proj, lb, batch):
    T = proj.shape[0]
    n = T // batch // SCAN_STEP
    fwd = lambda b, i: b * n + i
    bwd = lambda b, i: b * n + (n - 1 - i)
    blk = (SCAN_STEP, HG_WIDTH)
    return pl.pallas_call(
        _hgrn2_kernel,
        out_shape=(jax.ShapeDtypeStruct((T, HG_WIDTH), F32),
                   jax.ShapeDtypeStruct((T, HG_WIDTH), F32)),
        grid=(batch, n),
        in_specs=[pl.BlockSpec(blk, lambda b, i: (fwd(b, i), COLBLK_HQ)),
                  pl.BlockSpec(blk, lambda b, i: (fwd(b, i), COLBLK_HF_FWD)),
                  pl.BlockSpec(blk, lambda b, i: (fwd(b, i), COLBLK_HI)),
                  pl.BlockSpec(blk, lambda b, i: (bwd(b, i), COLBLK_HQ)),
                  pl.BlockSpec(blk, lambda b, i: (bwd(b, i), COLBLK_HF_BWD)),
                  pl.BlockSpec(blk, lambda b, i: (bwd(b, i), COLBLK_HI)),
                  pl.BlockSpec((1, HG_WIDTH), lambda b, i: (0, 0))],
        out_specs=(pl.BlockSpec(blk, lambda b, i: (fwd(b, i), 0)),
                   pl.BlockSpec(blk, lambda b, i: (bwd(b, i), 0))),
        scratch_shapes=[pltpu.VMEM((2, HG_HEADS, HG_DK, HG_DK), F32)],
        compiler_params=_cparams(("arbitrary", "arbitrary")),
        name="hgrn2_scan",
    )(proj, proj, proj, proj, proj, proj, lb.reshape(1, HG_WIDTH))


def _mlstm_direction(qk, v, gates, gates_t, bias_row, bias_cr, c_ref, m_ref, d, o_ref):
    reverse = d == 1
    n = SCAN_STEP
    tri = _causal(n, reverse)
    lmat = tri.astype(F32)
    last = 0 if reverse else n - 1

    g_c = gates + bias_row
    bcum_c = _dot_hi(lmat, _log_sigmoid(g_c))
    g_r = gates_t + bias_cr
    bcum_r = _dot_nt_hi(_log_sigmoid(g_r), lmat)
    ones = jnp.ones((n, ML_DH), BF16)

    for h in range(ML_HEADS):
        ci = GATE_IG + d * ML_HEADS + h
        cf = GATE_FG + d * ML_HEADS + h
        ri = d * ML_HEADS + h
        rf = 2 * ML_HEADS + d * ML_HEADS + h
        ms = d * ML_HEADS + h
        q = (qk[:, h * ML_DH:(h + 1) * ML_DH] * (ML_DH ** -0.5)).astype(BF16)
        k = qk[:, ML_WIDTH + h * ML_DH:ML_WIDTH + (h + 1) * ML_DH]
        ve = jnp.concatenate([v[:, h * ML_DH:(h + 1) * ML_DH].astype(BF16), ones], axis=1)
        b_col = bcum_c[:, cf:cf + 1]
        i_col = g_c[:, ci:ci + 1]
        b_row = bcum_r[rf:rf + 1, :]
        i_row = g_r[ri:ri + 1, :]
        m_prev = m_ref[ms:ms + 1, 0:1]
        dmat = jnp.where(tri, b_col - b_row + i_row, MASK_NEG)
        a_inter = b_col + m_prev
        m_t = jnp.maximum(jnp.max(dmat, axis=1, keepdims=True), a_inter)
        w_intra = jnp.exp(jnp.where(tri, dmat - m_t, MASK_NEG))
        w_inter = jnp.exp(a_inter - m_t)
        s = (_dot_nt(q, k.astype(BF16)) * w_intra).astype(BF16)
        cst = c_ref[d, h]
        numden = _dot(s, ve) + w_inter * _dot(q, cst.astype(BF16))
        den = numden[:, ML_DH:ML_DH + 1]
        o_ref[:, h * ML_DH:(h + 1) * ML_DH] = numden[:, :ML_DH] / jnp.maximum(jnp.abs(den), jnp.exp(-m_t))
        b_last = b_col[last:last + 1, :]
        m_new = jnp.maximum(b_last + m_prev, jnp.max(b_last - b_row + i_row, axis=1, keepdims=True))
        w_s = jnp.exp(b_last - b_col + i_col - m_new)
        dec = jnp.exp(b_last + m_prev - m_new)
        c_ref[d, h] = dec * cst + _dot_tn((k * w_s).astype(BF16), ve)
        m_ref[ms:ms + 1, :] = jnp.broadcast_to(m_new, (1, LANES))


def _mlstm_kernel(qkf_ref, vf_ref, gf_ref, gtf_ref, qkb_ref, vb_ref, gb_ref, gtb_ref,
                  brow_ref, bcr_ref, of_ref, ob_ref, c_ref, m_ref):
    @pl.when(pl.program_id(1) == 0)
    def _():
        c_ref[...] = jnp.zeros_like(c_ref)
        m_ref[...] = jnp.zeros_like(m_ref)

    _mlstm_direction(qkf_ref[...], vf_ref[...], gf_ref[...], gtf_ref[...], brow_ref[...], bcr_ref[...],
                     c_ref, m_ref, 0, of_ref)
    _mlstm_direction(qkb_ref[...], vb_ref[...], gb_ref[...], gtb_ref[...], brow_ref[...], bcr_ref[...],
                     c_ref, m_ref, 1, ob_ref)


def _mlstm_scan(mqk_act, proj, gates_t, bias_row, bias_cr, batch):
    T = proj.shape[0]
    n = T // batch // SCAN_STEP
    fwd = lambda b, i: b * n + i
    bwd = lambda b, i: b * n + (n - 1 - i)
    const = lambda b, i: (0, 0)
    gt_rowblk = GATE_IG // (2 * SUBLANES)
    return pl.pallas_call(
        _mlstm_kernel,
        out_shape=(jax.ShapeDtypeStruct((T, ML_WIDTH), F32),
                   jax.ShapeDtypeStruct((T, ML_WIDTH), F32)),
        grid=(batch, n),
        in_specs=[pl.BlockSpec((SCAN_STEP, 2 * ML_WIDTH), lambda b, i: (fwd(b, i), 0)),
                  pl.BlockSpec((SCAN_STEP, ML_WIDTH), lambda b, i: (fwd(b, i), COLBLK_MV)),
                  pl.BlockSpec((SCAN_STEP, LANES), lambda b, i: (fwd(b, i), COLBLK_GATES)),
                  pl.BlockSpec((2 * SUBLANES, SCAN_STEP), lambda b, i: (gt_rowblk, fwd(b, i))),
                  pl.BlockSpec((SCAN_STEP, 2 * ML_WIDTH), lambda b, i: (bwd(b, i), 0)),
                  pl.BlockSpec((SCAN_STEP, ML_WIDTH), lambda b, i: (bwd(b, i), COLBLK_MV)),
                  pl.BlockSpec((SCAN_STEP, LANES), lambda b, i: (bwd(b, i), COLBLK_GATES)),
                  pl.BlockSpec((2 * SUBLANES, SCAN_STEP), lambda b, i: (gt_rowblk, bwd(b, i))),
                  pl.BlockSpec((1, LANES), const),
                  pl.BlockSpec((2 * SUBLANES, LANES), const)],
        out_specs=(pl.BlockSpec((SCAN_STEP, ML_WIDTH), lambda b, i: (fwd(b, i), 0)),
                   pl.BlockSpec((SCAN_STEP, ML_WIDTH), lambda b, i: (bwd(b, i), 0))),
        scratch_shapes=[pltpu.VMEM((2, ML_HEADS, ML_DH, 2 * ML_DH), F32),
                        pltpu.VMEM((2 * ML_HEADS, LANES), F32)],
        compiler_params=_cparams(("arbitrary", "arbitrary")),
        name="mlstm_scan",
    )(mqk_act, proj, proj, gates_t, mqk_act, proj, proj, gates_t, bias_row, bias_cr)


def _outproj_kernel(yf_ref, yb_ref, xs_ref, z_ref, of_ref, ob_ref, hg_ref, mf_ref, mb_ref, mo_ref,
                    h_ref, dskip_ref, ssdw_ref, hgw_ref, mlw_ref, wout_ref, g_ref, b_ref, o_ref):
    y = (yf_ref[...] + yb_ref[...] + xs_ref[...] * dskip_ref[...]) * _silu(z_ref[...])
    gw = SSD_INNER // SSD_GROUPS
    acc = None
    for g in range(SSD_GROUPS):
        seg = y[:, g * gw:(g + 1) * gw]
        seg = seg * lax.rsqrt(jnp.mean(seg * seg, -1, keepdims=True) + RMS_EPS) * ssdw_ref[:, g * gw:(g + 1) * gw]
        part = _dot(seg.astype(BF16), wout_ref[g * gw:(g + 1) * gw, :])
        acc = part if acc is None else acc + part
    o = of_ref[...] + ob_ref[...]
    hgate = _silu(hg_ref[...])
    m = mf_ref[...] + mb_ref[...]
    mgate = _sigmoid(mo_ref[...])
    for h in range(HG_HEADS):
        cols = slice(h * HG_DK, (h + 1) * HG_DK)
        seg = o[:, cols]
        seg = seg * lax.rsqrt(jnp.mean(seg * seg, -1, keepdims=True) + RMS_EPS) * hgw_ref[:, cols] * hgate[:, cols]
        r0 = SSD_INNER + h * HG_DK
        acc = acc + _dot(seg.astype(BF16), wout_ref[r0:r0 + HG_DK, :])
    for h in range(ML_HEADS):
        cols = slice(h * ML_DH, (h + 1) * ML_DH)
        seg = m[:, cols]
        seg = seg - jnp.mean(seg, -1, keepdims=True)
        seg = seg * lax.rsqrt(jnp.mean(seg * seg, -1, keepdims=True) + LN_EPS) * mlw_ref[:, cols] * mgate[:, cols]
        r0 = SSD_INNER + HG_WIDTH + h * ML_DH
        acc = acc + _dot(seg.astype(BF16), wout_ref[r0:r0 + ML_DH, :])
    o_ref[...] = _layer_norm(ALPHA * h_ref[...] + acc, g_ref[...], b_ref[...])


def _outproj(y_f, y_b, xbc_act, proj, o_f, o_b, m_f, m_b, h, dskip, ssd_w, hg_w, ml_w, w_out, g, b):
    T, D = h.shape
    tm = TM_OUT
    row = lambda width, cb: pl.BlockSpec((tm, width), lambda i: (i, cb))
    par = lambda width: pl.BlockSpec((1, width), lambda i: (0, 0))
    return pl.pallas_call(
        _outproj_kernel,
        out_shape=jax.ShapeDtypeStruct((T, D), F32),
        grid=(T // tm,),
        in_specs=[row(SSD_INNER, 0), row(SSD_INNER, 0), row(SSD_INNER, 0), row(SSD_INNER, COLBLK_Z),
                  row(HG_WIDTH, 0), row(HG_WIDTH, 0), row(HG_WIDTH, COLBLK_HG),
                  row(ML_WIDTH, 0), row(ML_WIDTH, 0), row(ML_WIDTH, COLBLK_MO),
                  row(D, 0), par(SSD_INNER), par(SSD_INNER), par(HG_WIDTH), par(ML_WIDTH),
                  pl.BlockSpec((D_MIX, D), lambda i: (0, 0), pipeline_mode=pl.Buffered(1)),
                  par(D), par(D)],
        out_specs=pl.BlockSpec((tm, D), lambda i: (i, 0)),
        compiler_params=_cparams(("parallel",)),
        name="outproj_ln1",
    )(y_f, y_b, xbc_act, proj, o_f, o_b, proj, m_f, m_b, proj, h, dskip, ssd_w, hg_w, ml_w, w_out,
      g.reshape(1, D), b.reshape(1, D))


def _ffn_kernel(h_ref, w1_ref, w3_ref, w2_ref, g_ref, b_ref, o_ref, acc_ref):
    h = h_ref[...]
    hb = h.astype(BF16)
    for c in range(D_FF_DENSE // TF_FFN):
        cols = slice(c * TF_FFN, (c + 1) * TF_FFN)
        a = _dot(hb, w1_ref[:, cols])
        u = _dot(hb, w3_ref[:, cols])
        part = _dot((_silu(a) * u).astype(BF16), w2_ref[cols, :])
        if c == 0:
            acc_ref[...] = part
        else:
            acc_ref[...] += part
    o_ref[...] = _layer_norm(ALPHA * h + acc_ref[...], g_ref[...], b_ref[...])


def _dense_ffn(h, w1, w3, w2, g, b):
    T, D = h.shape
    tm = TM_FFN
    resident = lambda shape: pl.BlockSpec(shape, lambda i: (0, 0), pipeline_mode=pl.Buffered(1))
    return pl.pallas_call(
        _ffn_kernel,
        out_shape=jax.ShapeDtypeStruct((T, D), F32),
        grid=(T // tm,),
        in_specs=[pl.BlockSpec((tm, D), lambda i: (i, 0)),
                  resident((D, D_FF_DENSE)), resident((D, D_FF_DENSE)), resident((D_FF_DENSE, D)),
                  pl.BlockSpec((1, D), lambda i: (0, 0)), pl.BlockSpec((1, D), lambda i: (0, 0))],
        out_specs=pl.BlockSpec((tm, D), lambda i: (i, 0)),
        scratch_shapes=[pltpu.VMEM((tm, D), F32)],
        compiler_params=_cparams(("parallel",)),
        name="dense_ffn_ln2",
    )(h, w1, w3, w2, g.reshape(1, D), b.reshape(1, D))


def _router_kernel(h_ref, w_ref, idx_ref, gate_ref):
    logits = _dot_hi(h_ref[...], w_ref[...])
    lane = _iota(logits.shape, 1)
    lane_f = lane.astype(F32)
    x = jnp.where(lane < N_EXPERTS, logits, -jnp.inf)
    m1 = jnp.max(x, axis=1, keepdims=True)
    i1 = jnp.min(jnp.where(x == m1, lane_f, float(LANES)), axis=1, keepdims=True)
    x2 = jnp.where(lane_f == i1, -jnp.inf, x)
    m2 = jnp.max(x2, axis=1, keepdims=True)
    i2 = jnp.min(jnp.where(x2 == m2, lane_f, float(LANES)), axis=1, keepdims=True)
    e2 = jnp.exp(m2 - m1)
    g1 = 1.0 / (1.0 + e2)
    g2 = e2 / (1.0 + e2)
    idx_ref[...] = jnp.where(lane == 0, i1, jnp.where(lane == 1, i2, 0.0)).astype(jnp.int32)
    gate_ref[...] = jnp.where(lane == 0, g1, jnp.where(lane == 1, g2, 0.0))


def _router(h, w_pad):
    T, D = h.shape
    tm = TM_ROUTER
    return pl.pallas_call(
        _router_kernel,
        out_shape=(jax.ShapeDtypeStruct((T, LANES), jnp.int32),
                   jax.ShapeDtypeStruct((T, LANES), F32)),
        grid=(T // tm,),
        in_specs=[pl.BlockSpec((tm, D), lambda i: (i, 0)),
                  pl.BlockSpec((D, LANES), lambda i: (0, 0))],
        out_specs=(pl.BlockSpec((tm, LANES), lambda i: (i, 0)),
                   pl.BlockSpec((tm, LANES), lambda i: (i, 0))),
        compiler_params=_cparams(("parallel",)),
        name="moe_router",
    )(h, w_pad)


def _row_copy(src_ref, dst_ref, sem, src_row, dst_row):
    return pltpu.make_async_copy(src_ref.at[pl.ds(src_row, 1)], dst_ref.at[pl.ds(dst_row, 1)], sem)


def _gather_kernel(idx_ref, src_ref, dst_ref, sem):
    base = pl.program_id(0) * GATHER_ROWS

    def start(j, carry):
        _row_copy(src_ref, dst_ref, sem, idx_ref[0, 0, j], base + j).start()
        return carry

    def wait(j, carry):
        _row_copy(src_ref, dst_ref, sem, 0, base).wait()
        return carry

    lax.fori_loop(0, GATHER_ROWS, start, 0)
    lax.fori_loop(0, GATHER_ROWS, wait, 0)


def _gather_rows(src, idx):
    n = idx.shape[0]
    steps = n // GATHER_ROWS
    return pl.pallas_call(
        _gather_kernel,
        out_shape=jax.ShapeDtypeStruct((n, src.shape[1]), src.dtype),
        grid=(steps,),
        in_specs=[pl.BlockSpec((1, 1, GATHER_ROWS), lambda i: (i, 0, 0), memory_space=pltpu.SMEM),
                  pl.BlockSpec(memory_space=pl.ANY)],
        out_specs=pl.BlockSpec(memory_space=pl.ANY),
        scratch_shapes=[pltpu.SemaphoreType.DMA(())],
        compiler_params=_cparams(("arbitrary",)),
        name="moe_row_gather",
    )(idx.reshape(steps, 1, GATHER_ROWS), src)


def _expert_kernel(be_ref, nact_ref, x_ref, w1_ref, w3_ref, w2_ref, o_ref):
    i = pl.program_id(0)
    f = pl.program_id(1)

    @pl.when(f == 0)
    def _():
        o_ref[...] = jnp.zeros_like(o_ref)

    @pl.when(i < nact_ref[0])
    def _():
        xb = x_ref[...].astype(BF16)
        a = _dot(xb, w1_ref[...])
        u = _dot(xb, w3_ref[...])
        o_ref[...] += _dot((_silu(a) * u).astype(BF16), w2_ref[...])


def _expert_ffn(xs, block_expert, n_active, w1, w3, w2):
    n_rows, D = xs.shape
    n_blocks = n_rows // MOE_BM
    n_ff = D_FF_EXPERT // MOE_TF

    def widx(i, f, be, nact):
        live = i < nact[0]
        return be[jnp.minimum(i, nact[0] - 1)], jnp.where(live, f, n_ff - 1)

    def w13_map(i, f, be, nact):
        e, ff = widx(i, f, be, nact)
        return e, 0, ff

    def w2_map(i, f, be, nact):
        e, ff = widx(i, f, be, nact)
        return e, ff, 0

    grid_spec = pltpu.PrefetchScalarGridSpec(
        num_scalar_prefetch=2,
        grid=(n_blocks, n_ff),
        in_specs=[pl.BlockSpec((MOE_BM, D), lambda i, f, be, nact: (i, 0)),
                  pl.BlockSpec((None, D, MOE_TF), w13_map),
                  pl.BlockSpec((None, D, MOE_TF), w13_map),
                  pl.BlockSpec((None, MOE_TF, D), w2_map)],
        out_specs=pl.BlockSpec((MOE_BM, D), lambda i, f, be, nact: (i, 0)),
    )
    return pl.pallas_call(
        _expert_kernel,
        out_shape=jax.ShapeDtypeStruct((n_rows, D), F32),
        grid_spec=grid_spec,
        compiler_params=_cparams(("arbitrary", "arbitrary")),
        name="moe_experts",
    )(block_expert, n_active, xs, w1, w3, w2)


def _combine_kernel(y_ref, gate_ref, h_ref, g_ref, b_ref, o_ref):
    D = h_ref.shape[1]
    gate = gate_ref[...]
    f = y_ref[:, :D] * gate[:, 0:1] + y_ref[:, D:] * gate[:, 1:2]
    o_ref[...] = _layer_norm(ALPHA * h_ref[...] + f, g_ref[...], b_ref[...])


def _combine(y_pairs, gates, h, g, b):
    T, D = h.shape
    tm = TM_COMBINE
    return pl.pallas_call(
        _combine_kernel,
        out_shape=jax.ShapeDtypeStruct((T, D), F32),
        grid=(T // tm,),
        in_specs=[pl.BlockSpec((tm, TOP_K * D), lambda i: (i, 0)),
                  pl.BlockSpec((tm, LANES), lambda i: (i, 0)),
                  pl.BlockSpec((tm, D), lambda i: (i, 0)),
                  pl.BlockSpec((1, D), lambda i: (0, 0)),
                  pl.BlockSpec((1, D), lambda i: (0, 0))],
        out_specs=pl.BlockSpec((tm, D), lambda i: (i, 0)),
        compiler_params=_cparams(("parallel",)),
        name="moe_combine_ln2",
    )(y_pairs, gates, h, g.reshape(1, D), b.reshape(1, D))


def _moe_ffn(h, router_w, w1, w3, w2, g, b):
    T, D = h.shape
    n_pairs = T * TOP_K
    n_blocks = n_pairs // MOE_BM + N_EXPERTS
    n_rows = n_blocks * MOE_BM
    w_pad = jnp.pad(router_w.astype(F32), ((0, 0), (0, LANES - N_EXPERTS)))
    idx_pad, gates_pad = _router(h, w_pad)
    expert_ids = idx_pad[:, :TOP_K].reshape(-1)
    onehot = (expert_ids[:, None] == jnp.arange(N_EXPERTS, dtype=jnp.int32)[None, :]).astype(jnp.int32)
    running = jnp.cumsum(onehot, axis=0)
    counts = running[-1]
    rank = jnp.sum(onehot * running, axis=1) - 1
    padded = (counts + MOE_BM - 1) // MOE_BM * MOE_BM
    ends = jnp.cumsum(padded)
    pos = (ends - padded)[expert_ids] + rank
    token_ids = jnp.arange(n_pairs, dtype=jnp.int32) // TOP_K
    row_token = jnp.zeros((n_rows,), jnp.int32).at[pos].set(token_ids)
    block_start = jnp.arange(n_blocks, dtype=jnp.int32) * MOE_BM
    block_expert = jnp.minimum(jnp.searchsorted(ends, block_start, side='right'),
                               N_EXPERTS - 1).astype(jnp.int32)
    n_active = (ends[-1:] // MOE_BM).astype(jnp.int32)

    xs = _gather_rows(h, row_token)
    ys = _expert_ffn(xs, block_expert, n_active, w1, w3, w2)
    y_pairs = _gather_rows(ys, pos.astype(jnp.int32)).reshape(T, TOP_K * D)
    return _combine(y_pairs, gates_pad, h, g, b)


def _relayout_w_in(w):
    sizes = (SSD_INNER, SSD_XBC, 2 * SSD_HEADS, HG_WIDTH, 2 * HG_WIDTH, HG_WIDTH, HG_WIDTH,
             2 * ML_WIDTH, ML_WIDTH, ML_WIDTH, 2 * ML_HEADS, 2 * ML_HEADS)
    offs = [0]
    for s in sizes:
        offs.append(offs[-1] + s)
    (z, xbc, dt, hq, hf, hi, hg, mqk, mv, mo, mig, mfg) = [w[:, offs[k]:offs[k + 1]] for k in range(len(sizes))]
    pad = jnp.zeros((w.shape[0], LANES - 2 * SSD_HEADS - 4 * ML_HEADS), w.dtype)
    wn = jnp.concatenate([z, mqk, hf, xbc, hq, hi, hg, mv, mo, dt, mig, mfg, pad], axis=1).astype(BF16)
    return wn, wn[:, D_PROJ_PAD - LANES:].T


def _gate_rows(vals_by_offset):
    row = jnp.zeros((LANES,), F32)
    for off, v in vals_by_offset:
        row = lax.dynamic_update_slice(row, v.reshape(-1).astype(F32), (off,))
    return row


def kernel(x, ln_in_g, ln_in_b, w_in, ssd_conv_w, ssd_conv_b, ssd_dt_bias, ssd_a_log, ssd_d, ssd_norm_w, hg_lb_logits, hg_norm_w, ml_conv_w, ml_conv_b, ml_ig_bias, ml_fg_bias, ml_norm_w, w_out, ln1_g, ln1_b, ln2_g, ln2_b, ffn_w1, ffn_w3, ffn_w2, moe_router, moe_w1, moe_w3, moe_w2):
    batch, seq_len, D = x.shape
    T = batch * seq_len
    f32 = lambda t: t.astype(F32)
    lb_soft = jax.nn.softmax(f32(hg_lb_logits), axis=0)
    lb_all = jnp.cumsum(lb_soft, axis=0) - lb_soft[0]

    h = _input_ln(f32(x).reshape(T, D), f32(ln_in_g), f32(ln_in_b))
    for l in range(DEPTH):
        w, wgt = _relayout_w_in(w_in[l])
        bias_row = _gate_rows(((GATE_DT, ssd_dt_bias[l]), (GATE_IG, ml_ig_bias[l]), (GATE_FG, ml_fg_bias[l])))
        alog_row = _gate_rows(((GATE_DT, ssd_a_log[l]),))
        bias_cr = jnp.broadcast_to(bias_row[:, None], (LANES, LANES))
        alog_cr = jnp.broadcast_to(alog_row[:, None], (LANES, LANES))
        bias_row = bias_row.reshape(1, LANES)
        alog_row = alog_row.reshape(1, LANES)

        proj, gates_t = _inproj(h, w, wgt)
        xbc_act = _conv_silu(proj, f32(ssd_conv_w[l]), f32(ssd_conv_b[l]), COLBLK_XBC, SSD_XBC, seq_len)
        mqk_act = _conv_silu(proj, f32(ml_conv_w[l]), f32(ml_conv_b[l]), COLBLK_MQK, 2 * ML_WIDTH, seq_len)
        y_f, y_b = _ssd_scan(xbc_act, proj, gates_t, bias_row, alog_row,
                             bias_cr[GATE_DT:GATE_DT + 2 * SSD_HEADS], alog_cr[GATE_DT:GATE_DT + 2 * SSD_HEADS],
                             batch)
        o_f, o_b = _hgrn2_scan(proj, lb_all[l], batch)
        m_f, m_b = _mlstm_scan(mqk_act, proj, gates_t, bias_row, bias_cr[GATE_IG:GATE_IG + 4 * ML_HEADS], batch)
        dskip = jnp.repeat(f32(ssd_d[l]), SSD_HEAD_DIM).reshape(1, SSD_INNER)
        h = _outproj(y_f, y_b, xbc_act, proj, o_f, o_b, m_f, m_b, h, dskip,
                     f32(ssd_norm_w[l]).reshape(1, -1), f32(hg_norm_w[l]).reshape(1, -1),
                     f32(ml_norm_w[l]).reshape(1, -1), w_out[l].astype(BF16), f32(ln1_g[l]), f32(ln1_b[l]))
        if l % 2 == 0:
            j = l // 2
            h = _dense_ffn(h, ffn_w1[j].astype(BF16), ffn_w3[j].astype(BF16), ffn_w2[j].astype(BF16),
                           f32(ln2_g[l]), f32(ln2_b[l]))
        else:
            j = l // 2
            h = _moe_ffn(h, moe_router[j], moe_w1[j].astype(BF16), moe_w3[j].astype(BF16),
                         moe_w2[j].astype(BF16), f32(ln2_g[l]), f32(ln2_b[l]))
    return h.reshape(batch, seq_len, D).astype(x.dtype)
```

```python
import functools

import jax
import jax.numpy as jnp
from jax import lax
from jax.experimental import pallas as pl
from jax.experimental.pallas import tpu as pltpu
from jax.experimental.pallas import tpu_sc as plsc

F32 = jnp.float32
BF16 = jnp.bfloat16
HIGHEST = lax.Precision.HIGHEST

D_MODEL = 1024
DEPTH = 4
D_MIX = 2 * D_MODEL
SSD_HEADS = 16
SSD_INNER = 1024
SSD_HEAD_DIM = 64
SSD_GROUPS = 2
SSD_STATE = 128
SSD_XBC = 1536
SSD_CHUNK = 128
HG_HEADS = 4
HG_WIDTH = 512
HG_DK = 128
ML_HEADS = 4
ML_WIDTH = 512
ML_DH = 128
CONV_WIDTH = 5
D_FF_DENSE = 2816
N_EXPERTS = 8
TOP_K = 2
D_FF_EXPERT = 3584
LN_EPS = 1e-5
RMS_EPS = 1e-6
MASK_NEG = -1e4
ALPHA = (2.0 * DEPTH) ** 0.25

LANES = 128
SUBLANES = 8
VMEM_LIMIT = 56 * 1024 * 1024

D_PROJ_PAD = 7296
COLBLK_Z = 0
COLBLK_MQK = 1
COLBLK_XBC = 2
COLBLK_HF_FWD = 4
COLBLK_HF_BWD = 5
COLBLK_HQ = 9
COLBLK_HI = 10
COLBLK_HG = 11
COLBLK_MV = 12
COLBLK_MO = 13
COLBLK_GATES = 56
GATE_DT = 0
GATE_IG = 32
GATE_FG = 40

TM_LN = 512
TM_INPROJ = 256
TM_CONV = 512
SCAN_STEP = 128
HG_SUB = 64
HG_REFBLK = 4
TM_OUT = 256
TM_FFN = 512
TF_FFN = 256
TM_ROUTER = 512
MOE_BM = 512
MOE_TF = 512
SC_CORES = 2
SC_WORKERS = 32
SC_GATHER_CHUNK = 64
TM_COMBINE = 256


def _cparams(sem):
    return pltpu.CompilerParams(dimension_semantics=sem, vmem_limit_bytes=VMEM_LIMIT)


def _iota(shape, dim):
    return lax.broadcasted_iota(jnp.int32, shape, dim)


def _sigmoid(x):
    return 1.0 / (1.0 + jnp.exp(-x))


def _silu(x):
    return x * _sigmoid(x)


def _softplus(x):
    return jnp.maximum(x, 0.0) + jnp.log1p(jnp.exp(-jnp.abs(x)))


def _log_sigmoid(x):
    return -_softplus(-x)


def _layer_norm(x, g, b):
    mu = jnp.mean(x, -1, keepdims=True)
    xc = x - mu
    var = jnp.mean(xc * xc, -1, keepdims=True)
    return xc * lax.rsqrt(var + LN_EPS) * g + b


def _causal(n, reverse):
    r = _iota((n, n), 0)
    c = _iota((n, n), 1)
    return (c >= r) if reverse else (c <= r)


def _dot(a, b):
    return jnp.dot(a, b, preferred_element_type=F32)


def _dot_nt(a, b):
    return lax.dot_general(a, b, (((1,), (1,)), ((), ())), preferred_element_type=F32)


def _dot_tn(a, b):
    return lax.dot_general(a, b, (((0,), (0,)), ((), ())), preferred_element_type=F32)


def _dot_hi(a, b):
    return jnp.dot(a, b, precision=HIGHEST, preferred_element_type=F32)


def _dot_nt_hi(a, b):
    return lax.dot_general(a, b, (((1,), (1,)), ((), ())), precision=HIGHEST,
                           preferred_element_type=F32)


def _ln_kernel(x_ref, g_ref, b_ref, o_ref):
    o_ref[...] = _layer_norm(x_ref[...], g_ref[...], b_ref[...])


def _input_ln(x, g, b):
    T, D = x.shape
    return pl.pallas_call(
        _ln_kernel,
        out_shape=jax.ShapeDtypeStruct((T, D), F32),
        grid=(T // TM_LN,),
        in_specs=[pl.BlockSpec((TM_LN, D), lambda i: (i, 0)),
                  pl.BlockSpec((1, D), lambda i: (0, 0)),
                  pl.BlockSpec((1, D), lambda i: (0, 0))],
        out_specs=pl.BlockSpec((TM_LN, D), lambda i: (i, 0)),
        compiler_params=_cparams(("parallel",)),
        name="input_ln",
    )(x, g.reshape(1, D), b.reshape(1, D))


def _inproj_kernel(h_ref, w_ref, wgt_ref, proj_ref, gt_ref):
    hb = h_ref[...].astype(BF16)
    proj_ref[...] = _dot(hb, w_ref[...])
    gt_ref[...] = _dot_nt(wgt_ref[...], hb)


def _inproj(h, w, wgt):
    T, D = h.shape
    tm = TM_INPROJ
    return pl.pallas_call(
        _inproj_kernel,
        out_shape=(jax.ShapeDtypeStruct((T, D_PROJ_PAD), F32),
                   jax.ShapeDtypeStruct((LANES, T), F32)),
        grid=(T // tm,),
        in_specs=[pl.BlockSpec((tm, D), lambda i: (i, 0)),
                  pl.BlockSpec((D, D_PROJ_PAD), lambda i: (0, 0), pipeline_mode=pl.Buffered(1)),
                  pl.BlockSpec((LANES, D), lambda i: (0, 0), pipeline_mode=pl.Buffered(1))],
        out_specs=(pl.BlockSpec((tm, D_PROJ_PAD), lambda i: (i, 0)),
                   pl.BlockSpec((LANES, tm), lambda i: (0, i))),
        compiler_params=_cparams(("parallel",)),
        name="inproj",
    )(h, w, wgt)


def _conv_kernel(prev_ref, x_ref, next_ref, w_ref, b_ref, o_ref, *, tm, tiles_per_seq):
    i = pl.program_id(0)
    pos = i % tiles_per_seq
    x = x_ref[...]
    prev = jnp.where(pos == 0, 0.0, prev_ref[...])
    nxt = jnp.where(pos == tiles_per_seq - 1, 0.0, next_ref[...])
    xe = jnp.concatenate([prev, x, nxt], axis=0)
    n = tm + 2 * SUBLANES
    acc = b_ref[...] + w_ref[2:3, :] * x
    for k in (0, 1, 3, 4):
        shifted = pltpu.roll(xe, (2 - k) % n, 0)
        acc = acc + w_ref[k:k + 1, :] * shifted[SUBLANES:SUBLANES + tm]
    o_ref[...] = _silu(acc)


def _conv_silu(proj, w, b, colblk, width, seq_len):
    T = proj.shape[0]
    tm = TM_CONV
    rb = tm // SUBLANES
    nrb = T // SUBLANES
    kern = functools.partial(_conv_kernel, tm=tm, tiles_per_seq=seq_len // tm)
    return pl.pallas_call(
        kern,
        out_shape=jax.ShapeDtypeStruct((T, width), F32),
        grid=(T // tm,),
        in_specs=[pl.BlockSpec((SUBLANES, width), lambda i: (jnp.maximum(i * rb - 1, 0), colblk)),
                  pl.BlockSpec((tm, width), lambda i: (i, colblk)),
                  pl.BlockSpec((SUBLANES, width), lambda i: (jnp.minimum((i + 1) * rb, nrb - 1), colblk)),
                  pl.BlockSpec((CONV_WIDTH, width), lambda i: (0, 0)),
                  pl.BlockSpec((1, width), lambda i: (0, 0))],
        out_specs=pl.BlockSpec((tm, width), lambda i: (i, 0)),
        compiler_params=_cparams(("parallel",)),
        name="conv_silu",
    )(proj, proj, proj, w, b.reshape(1, width))


def _ssd_direction(x, gates, gates_t, bias_row, alog_row, bias_cr, alog_cr, st_ref, d, y_ref):
    reverse = d == 1
    n = SSD_CHUNK
    tri = _causal(n, reverse)
    lmat = tri.astype(F32)
    last = 0 if reverse else n - 1
    lane_lo = _iota((n, LANES), 1) < SSD_HEAD_DIM

    dt_c = _softplus(gates + bias_row)
    a_c = dt_c * (-jnp.exp(alog_row))
    acs_c = _dot_hi(lmat, a_c)
    dt_r = _softplus(gates_t + bias_cr)
    a_r = dt_r * (-jnp.exp(alog_cr))
    acs_r = _dot_nt_hi(a_r, lmat)
    a_last = acs_c[last:last + 1, :]
    ea_c = jnp.exp(acs_c)
    w_c = dt_c * jnp.exp(a_last - acs_c)
    dec_row = jnp.exp(a_last)

    for g in range(SSD_GROUPS):
        b0 = SSD_INNER + g * SSD_STATE
        c0 = SSD_INNER + SSD_GROUPS * SSD_STATE + g * SSD_STATE
        bg = x[:, b0:b0 + SSD_STATE]
        cg = x[:, c0:c0 + SSD_STATE]
        cb = _dot_nt(cg.astype(BF16), bg.astype(BF16))
        for pair in range(4):
            p0 = (g * 4 + pair) * LANES
            xp = x[:, p0:p0 + LANES]
            ypair = None
            for hh in range(2):
                h = g * 8 + pair * 2 + hh
                c = GATE_DT + d * SSD_HEADS + h
                xm = jnp.where(lane_lo if hh == 0 else jnp.logical_not(lane_lo), xp, 0.0).astype(BF16)
                col = acs_c[:, c:c + 1]
                row = acs_r[h:h + 1, :]
                decay = jnp.exp(jnp.where(tri, col - row, MASK_NEG))
                m = (cb * decay * dt_r[h:h + 1, :]).astype(BF16)
                ch = (cg * ea_c[:, c:c + 1]).astype(BF16)
                st = st_ref[d, h]
                y = _dot(m, xm) + _dot(ch, st.astype(BF16))
                bh = (bg * w_c[:, c:c + 1]).astype(BF16)
                st_ref[d, h] = st * dec_row[0:1, c:c + 1] + _dot_tn(bh, xm)
                ypair = y if ypair is None else ypair + y
            y_ref[:, p0:p0 + LANES] = ypair


def _ssd_kernel(xf_ref, xb_ref, gf_ref, gb_ref, gtf_ref, gtb_ref, brow_ref, arow_ref,
                bcr_ref, acr_ref, yf_ref, yb_ref, st_ref):
    @pl.when(pl.program_id(1) == 0)
    def _():
        st_ref[...] = jnp.zeros_like(st_ref)

    _ssd_direction(xf_ref[...], gf_ref[...], gtf_ref[...], brow_ref[...], arow_ref[...],
                   bcr_ref[0:SSD_HEADS, :], acr_ref[0:SSD_HEADS, :], st_ref, 0, yf_ref)
    _ssd_direction(xb_ref[...], gb_ref[...], gtb_ref[...], brow_ref[...], arow_ref[...],
                   bcr_ref[SSD_HEADS:2 * SSD_HEADS, :], acr_ref[SSD_HEADS:2 * SSD_HEADS, :],
                   st_ref, 1, yb_ref)


def _ssd_scan(xbc_act, proj, gates_t, bias_row, alog_row, bias_cr, alog_cr, batch):
    T = xbc_act.shape[0]
    n = T // batch // SSD_CHUNK
    fwd = lambda b, i: b * n + i
    bwd = lambda b, i: b * n + (n - 1 - i)
    const = lambda b, i: (0, 0)
    return pl.pallas_call(
        _ssd_kernel,
        out_shape=(jax.ShapeDtypeStruct((T, SSD_INNER), F32),
                   jax.ShapeDtypeStruct((T, SSD_INNER), F32)),
        grid=(batch, n),
        in_specs=[pl.BlockSpec((SSD_CHUNK, SSD_XBC), lambda b, i: (fwd(b, i), 0)),
                  pl.BlockSpec((SSD_CHUNK, SSD_XBC), lambda b, i: (bwd(b, i), 0)),
                  pl.BlockSpec((SSD_CHUNK, LANES), lambda b, i: (fwd(b, i), COLBLK_GATES)),
                  pl.BlockSpec((SSD_CHUNK, LANES), lambda b, i: (bwd(b, i), COLBLK_GATES)),
                  pl.BlockSpec((SSD_HEADS, SSD_CHUNK), lambda b, i: (0, fwd(b, i))),
                  pl.BlockSpec((SSD_HEADS, SSD_CHUNK), lambda b, i: (1, bwd(b, i))),
                  pl.BlockSpec((1, LANES), const),
                  pl.BlockSpec((1, LANES), const),
                  pl.BlockSpec((2 * SSD_HEADS, LANES), const),
                  pl.BlockSpec((2 * SSD_HEADS, LANES), const)],
        out_specs=(pl.BlockSpec((SSD_CHUNK, SSD_INNER), lambda b, i: (fwd(b, i), 0)),
                   pl.BlockSpec((SSD_CHUNK, SSD_INNER), lambda b, i: (bwd(b, i), 0))),
        scratch_shapes=[pltpu.VMEM((2, SSD_HEADS, SSD_STATE, LANES), F32)],
        compiler_params=_cparams(("arbitrary", "arbitrary")),
        name="ssd_scan",
    )(xbc_act, xbc_act, proj, proj, gates_t, gates_t, bias_row, alog_row, bias_cr, alog_cr)


def _hgrn2_subchunk(q, f, v, lb, st_ref, d, h):
    reverse = d == 1
    n = HG_SUB
    nblk = n // HG_REFBLK
    tri = _causal(n, reverse)
    lmat = tri.astype(F32)
    last = 0 if reverse else n - 1
    r_i = _iota((n, n), 0)
    c_i = _iota((n, n), 1)
    ref_of_row = (r_i // HG_REFBLK) * HG_REFBLK + (HG_REFBLK - 1 if reverse else 0)
    lref = ((c_i >= ref_of_row) if reverse else (c_i <= ref_of_row)).astype(F32)
    rowblk = _iota((n, HG_DK), 0) // HG_REFBLK

    qh = _silu(q)
    key = (1.0 - lb) / (1.0 + jnp.exp(f))
    g = jnp.log1p(-key)
    gc = _dot_hi(lmat, g)
    gref = _dot_hi(lref, g)
    qg = qh * jnp.exp(gc - gref)
    q_parts = []
    k_parts = []
    for j in range(nblk):
        ref_row = j * HG_REFBLK + (HG_REFBLK - 1 if reverse else 0)
        rj = gc[ref_row:ref_row + 1, :]
        live = (rowblk >= j) if reverse else (rowblk <= j)
        k_parts.append((key * jnp.exp(jnp.where(live, rj - gc, MASK_NEG))).astype(BF16))
        q_parts.append(jnp.where(rowblk == j, qg, 0.0).astype(BF16))
    att = _dot_nt(jnp.concatenate(q_parts, axis=1), jnp.concatenate(k_parts, axis=1))
    att = jnp.where(tri, att, 0.0)
    st = st_ref[d, h]
    vb = v.astype(BF16)
    o = _dot(att.astype(BF16), vb) + _dot_nt((qh * jnp.exp(gc)).astype(BF16), st.astype(BF16))
    g_last = gc[last:last + 1, :]
    kd = (key * jnp.exp(g_last - gc)).astype(BF16)
    st_ref[d, h] = st * jnp.exp(g_last) + _dot_tn(vb, kd)
    return o


def _hgrn2_kernel(qf_ref, ff_ref, vf_ref, qb_ref, fb_ref, vb_ref, lb_ref, of_ref, ob_ref, st_ref):
    @pl.when(pl.program_id(1) == 0)
    def _():
        st_ref[...] = jnp.zeros_like(st_ref)

    nsub = SCAN_STEP // HG_SUB
    for d, (q_ref, f_ref, v_ref, o_ref) in enumerate(((qf_ref, ff_ref, vf_ref, of_ref),
                                                      (qb_ref, fb_ref, vb_ref, ob_ref))):
        order = range(nsub) if d == 0 else range(nsub - 1, -1, -1)
        for sc in order:
            rows = slice(sc * HG_SUB, (sc + 1) * HG_SUB)
            for h in range(HG_HEADS):
                cols = slice(h * HG_DK, (h + 1) * HG_DK)
                o_ref[rows, cols] = _hgrn2_subchunk(q_ref[rows, cols], f_ref[rows, cols],
                                                    v_ref[rows, cols], lb_ref[:, cols], st_ref, d, h)


def _hgrn2_scan(proj, lb, batch):
    T = proj.shape[0]
    n = T // batch // SCAN_STEP
    fwd = lambda b, i: b * n + i
    bwd = lambda b, i: b * n + (n - 1 - i)
    blk = (SCAN_STEP, HG_WIDTH)
    return pl.pallas_call(
        _hgrn2_kernel,
        out_shape=(jax.ShapeDtypeStruct((T, HG_WIDTH), F32),
                   jax.ShapeDtypeStruct((T, HG_WIDTH), F32)),
        grid=(batch, n),
        in_specs=[pl.BlockSpec(blk, lambda b, i: (fwd(b, i), COLBLK_HQ)),
                  pl.BlockSpec(blk, lambda b, i: (fwd(b, i), COLBLK_HF_FWD)),
                  pl.BlockSpec(blk, lambda b, i: (fwd(b, i), COLBLK_HI)),
                  pl.BlockSpec(blk, lambda b, i: (bwd(b, i), COLBLK_HQ)),
                  pl.BlockSpec(blk, lambda b, i: (bwd(b, i), COLBLK_HF_BWD)),
                  pl.BlockSpec(blk, lambda b, i: (bwd(b, i), COLBLK_HI)),
                  pl.BlockSpec((1, HG_WIDTH), lambda b, i: (0, 0))],
        out_specs=(pl.BlockSpec(blk, lambda b, i: (fwd(b, i), 0)),
                   pl.BlockSpec(blk, lambda b, i: (bwd(b, i), 0))),
        scratch_shapes=[pltpu.VMEM((2, HG_HEADS, HG_DK, HG_DK), F32)],
        compiler_params=_cparams(("arbitrary", "arbitrary")),
        name="hgrn2_scan",
    )(proj, proj, proj, proj, proj, proj, lb.reshape(1, HG_WIDTH))


def _mlstm_direction(qk, v, gates, gates_t, bias_row, bias_cr, c_ref, m_ref, d, o_ref):
    reverse = d == 1
    n = SCAN_STEP
    tri = _causal(n, reverse)
    lmat = tri.astype(F32)
    last = 0 if reverse else n - 1

    g_c = gates + bias_row
    bcum_c = _dot_hi(lmat, _log_sigmoid(g_c))
    g_r = gates_t + bias_cr
    bcum_r = _dot_nt_hi(_log_sigmoid(g_r), lmat)
    ones = jnp.ones((n, ML_DH), BF16)

    for h in range(ML_HEADS):
        ci = GATE_IG + d * ML_HEADS + h
        cf = GATE_FG + d * ML_HEADS + h
        ri = d * ML_HEADS + h
        rf = 2 * ML_HEADS + d * ML_HEADS + h
        ms = d * ML_HEADS + h
        q = (qk[:, h * ML_DH:(h + 1) * ML_DH] * (ML_DH ** -0.5)).astype(BF16)
        k = qk[:, ML_WIDTH + h * ML_DH:ML_WIDTH + (h + 1) * ML_DH]
        ve = jnp.concatenate([v[:, h * ML_DH:(h + 1) * ML_DH].astype(BF16), ones], axis=1)
        b_col = bcum_c[:, cf:cf + 1]
        i_col = g_c[:, ci:ci + 1]
        b_row = bcum_r[rf:rf + 1, :]
        i_row = g_r[ri:ri + 1, :]
        m_prev = m_ref[ms:ms + 1, 0:1]
        dmat = jnp.where(tri, b_col - b_row + i_row, MASK_NEG)
        a_inter = b_col + m_prev
        m_t = jnp.maximum(jnp.max(dmat, axis=1, keepdims=True), a_inter)
        w_intra = jnp.exp(jnp.where(tri, dmat - m_t, MASK_NEG))
        w_inter = jnp.exp(a_inter - m_t)
        s = (_dot_nt(q, k.astype(BF16)) * w_intra).astype(BF16)
        cst = c_ref[d, h]
        numden = _dot(s, ve) + w_inter * _dot(q, cst.astype(BF16))
        den = numden[:, ML_DH:ML_DH + 1]
        o_ref[:, h * ML_DH:(h + 1) * ML_DH] = numden[:, :ML_DH] / jnp.maximum(jnp.abs(den), jnp.exp(-m_t))
        b_last = b_col[last:last + 1, :]
        m_new = jnp.maximum(b_last + m_prev, jnp.max(b_last - b_row + i_row, axis=1, keepdims=True))
        w_s = jnp.exp(b_last - b_col + i_col - m_new)
        dec = jnp.exp(b_last + m_prev - m_new)
        c_ref[d, h] = dec * cst + _dot_tn((k * w_s).astype(BF16), ve)
        m_ref[ms:ms + 1, :] = jnp.broadcast_to(m_new, (1, LANES))


def _mlstm_kernel(qkf_ref, vf_ref, gf_ref, gtf_ref, qkb_ref, vb_ref, gb_ref, gtb_ref,
                  brow_ref, bcr_ref, of_ref, ob_ref, c_ref, m_ref):
    @pl.when(pl.program_id(1) == 0)
    def _():
        c_ref[...] = jnp.zeros_like(c_ref)
        m_ref[...] = jnp.zeros_like(m_ref)

    _mlstm_direction(qkf_ref[...], vf_ref[...], gf_ref[...], gtf_ref[...], brow_ref[...], bcr_ref[...],
                     c_ref, m_ref, 0, of_ref)
    _mlstm_direction(qkb_ref[...], vb_ref[...], gb_ref[...], gtb_ref[...], brow_ref[...], bcr_ref[...],
                     c_ref, m_ref, 1, ob_ref)


def _mlstm_scan(mqk_act, proj, gates_t, bias_row, bias_cr, batch):
    T = proj.shape[0]
    n = T // batch // SCAN_STEP
    fwd = lambda b, i: b * n + i
    bwd = lambda b, i: b * n + (n - 1 - i)
    const = lambda b, i: (0, 0)
    gt_rowblk = GATE_IG // (2 * SUBLANES)
    return pl.pallas_call(
        _mlstm_kernel,
        out_shape=(jax.ShapeDtypeStruct((T, ML_WIDTH), F32),
                   jax.ShapeDtypeStruct((T, ML_WIDTH), F32)),
        grid=(batch, n),
        in_specs=[pl.BlockSpec((SCAN_STEP, 2 * ML_WIDTH), lambda b, i: (fwd(b, i), 0)),
                  pl.BlockSpec((SCAN_STEP, ML_WIDTH), lambda b, i: (fwd(b, i), COLBLK_MV)),
                  pl.BlockSpec((SCAN_STEP, LANES), lambda b, i: (fwd(b, i), COLBLK_GATES)),
                  pl.BlockSpec((2 * SUBLANES, SCAN_STEP), lambda b, i: (gt_rowblk, fwd(b, i))),
                  pl.BlockSpec((SCAN_STEP, 2 * ML_WIDTH), lambda b, i: (bwd(b, i), 0)),
                  pl.BlockSpec((SCAN_STEP, ML_WIDTH), lambda b, i: (bwd(b, i), COLBLK_MV)),
                  pl.BlockSpec((SCAN_STEP, LANES), lambda b, i: (bwd(b, i), COLBLK_GATES)),
                  pl.BlockSpec((2 * SUBLANES, SCAN_STEP), lambda b, i: (gt_rowblk, bwd(b, i))),
                  pl.BlockSpec((1, LANES), const),
                  pl.BlockSpec((2 * SUBLANES, LANES), const)],
        out_specs=(pl.BlockSpec((SCAN_STEP, ML_WIDTH), lambda b, i: (fwd(b, i), 0)),
                   pl.BlockSpec((SCAN_STEP, ML_WIDTH), lambda b, i: (bwd(b, i), 0))),
        scratch_shapes=[pltpu.VMEM((2, ML_HEADS, ML_DH, 2 * ML_DH), F32),
                        pltpu.VMEM((2 * ML_HEADS, LANES), F32)],
        compiler_params=_cparams(("arbitrary", "arbitrary")),
        name="mlstm_scan",
    )(mqk_act, proj, proj, gates_t, mqk_act, proj, proj, gates_t, bias_row, bias_cr)


def _outproj_kernel(yf_ref, yb_ref, xs_ref, z_ref, of_ref, ob_ref, hg_ref, mf_ref, mb_ref, mo_ref,
                    h_ref, dskip_ref, ssdw_ref, hgw_ref, mlw_ref, wout_ref, g_ref, b_ref, o_ref):
    y = (yf_ref[...] + yb_ref[...] + xs_ref[...] * dskip_ref[...]) * _silu(z_ref[...])
    gw = SSD_INNER // SSD_GROUPS
    acc = None
    for g in range(SSD_GROUPS):
        seg = y[:, g * gw:(g + 1) * gw]
        seg = seg * lax.rsqrt(jnp.mean(seg * seg, -1, keepdims=True) + RMS_EPS) * ssdw_ref[:, g * gw:(g + 1) * gw]
        part = _dot(seg.astype(BF16), wout_ref[g * gw:(g + 1) * gw, :])
        acc = part if acc is None else acc + part
    o = of_ref[...] + ob_ref[...]
    hgate = _silu(hg_ref[...])
    m = mf_ref[...] + mb_ref[...]
    mgate = _sigmoid(mo_ref[...])
    for h in range(HG_HEADS):
        cols = slice(h * HG_DK, (h + 1) * HG_DK)
        seg = o[:, cols]
        seg = seg * lax.rsqrt(jnp.mean(seg * seg, -1, keepdims=True) + RMS_EPS) * hgw_ref[:, cols] * hgate[:, cols]
        r0 = SSD_INNER + h * HG_DK
        acc = acc + _dot(seg.astype(BF16), wout_ref[r0:r0 + HG_DK, :])
    for h in range(ML_HEADS):
        cols = slice(h * ML_DH, (h + 1) * ML_DH)
        seg = m[:, cols]
        seg = seg - jnp.mean(seg, -1, keepdims=True)
        seg = seg * lax.rsqrt(jnp.mean(seg * seg, -1, keepdims=True) + LN_EPS) * mlw_ref[:, cols] * mgate[:, cols]
        r0 = SSD_INNER + HG_WIDTH + h * ML_DH
        acc = acc + _dot(seg.astype(BF16), wout_ref[r0:r0 + ML_DH, :])
    o_ref[...] = _layer_norm(ALPHA * h_ref[...] + acc, g_ref[...], b_ref[...])


def _outproj(y_f, y_b, xbc_act, proj, o_f, o_b, m_f, m_b, h, dskip, ssd_w, hg_w, ml_w, w_out, g, b):
    T, D = h.shape
    tm = TM_OUT
    row = lambda width, cb: pl.BlockSpec((tm, width), lambda i: (i, cb))
    par = lambda width: pl.BlockSpec((1, width), lambda i: (0, 0))
    return pl.pallas_call(
        _outproj_kernel,
        out_shape=jax.ShapeDtypeStruct((T, D), F32),
        grid=(T // tm,),
        in_specs=[row(SSD_INNER, 0), row(SSD_INNER, 0), row(SSD_INNER, 0), row(SSD_INNER, COLBLK_Z),
                  row(HG_WIDTH, 0), row(HG_WIDTH, 0), row(HG_WIDTH, COLBLK_HG),
                  row(ML_WIDTH, 0), row(ML_WIDTH, 0), row(ML_WIDTH, COLBLK_MO),
                  row(D, 0), par(SSD_INNER), par(SSD_INNER), par(HG_WIDTH), par(ML_WIDTH),
                  pl.BlockSpec((D_MIX, D), lambda i: (0, 0), pipeline_mode=pl.Buffered(1)),
                  par(D), par(D)],
        out_specs=pl.BlockSpec((tm, D), lambda i: (i, 0)),
        compiler_params=_cparams(("parallel",)),
        name="outproj_ln1",
    )(y_f, y_b, xbc_act, proj, o_f, o_b, proj, m_f, m_b, proj, h, dskip, ssd_w, hg_w, ml_w, w_out,
      g.reshape(1, D), b.reshape(1, D))


def _ffn_kernel(h_ref, w1_ref, w3_ref, w2_ref, g_ref, b_ref, o_ref, acc_ref):
    h = h_ref[...]
    hb = h.astype(BF16)
    for c in range(D_FF_DENSE // TF_FFN):
        cols = slice(c * TF_FFN, (c + 1) * TF_FFN)
        a = _dot(hb, w1_ref[:, cols])
        u = _dot(hb, w3_ref[:, cols])
        part = _dot((_silu(a) * u).astype(BF16), w2_ref[cols, :])
        if c == 0:
            acc_ref[...] = part
        else:
            acc_ref[...] += part
    o_ref[...] = _layer_norm(ALPHA * h + acc_ref[...], g_ref[...], b_ref[...])


def _dense_ffn(h, w1, w3, w2, g, b):
    T, D = h.shape
    tm = TM_FFN
    resident = lambda shape: pl.BlockSpec(shape, lambda i: (0, 0), pipeline_mode=pl.Buffered(1))
    return pl.pallas_call(
        _ffn_kernel,
        out_shape=jax.ShapeDtypeStruct((T, D), F32),
        grid=(T // tm,),
        in_specs=[pl.BlockSpec((tm, D), lambda i: (i, 0)),
                  resident((D, D_FF_DENSE)), resident((D, D_FF_DENSE)), resident((D_FF_DENSE, D)),
                  pl.BlockSpec((1, D), lambda i: (0, 0)), pl.BlockSpec((1, D), lambda i: (0, 0))],
        out_specs=pl.BlockSpec((tm, D), lambda i: (i, 0)),
        scratch_shapes=[pltpu.VMEM((tm, D), F32)],
        compiler_params=_cparams(("parallel",)),
        name="dense_ffn_ln2",
    )(h, w1, w3, w2, g.reshape(1, D), b.reshape(1, D))


def _router_kernel(h_ref, w_ref, idx_ref, gate_ref):
    logits = _dot_hi(h_ref[...], w_ref[...])
    lane = _iota(logits.shape, 1)
    lane_f = lane.astype(F32)
    x = jnp.where(lane < N_EXPERTS, logits, -jnp.inf)
    m1 = jnp.max(x, axis=1, keepdims=True)
    i1 = jnp.min(jnp.where(x == m1, lane_f, float(LANES)), axis=1, keepdims=True)
    x2 = jnp.where(lane_f == i1, -jnp.inf, x)
    m2 = jnp.max(x2, axis=1, keepdims=True)
    i2 = jnp.min(jnp.where(x2 == m2, lane_f, float(LANES)), axis=1, keepdims=True)
    e2 = jnp.exp(m2 - m1)
    g1 = 1.0 / (1.0 + e2)
    g2 = e2 / (1.0 + e2)
    idx_ref[...] = jnp.where(lane == 0, i1, jnp.where(lane == 1, i2, 0.0)).astype(jnp.int32)
    gate_ref[...] = jnp.where(lane == 0, g1, jnp.where(lane == 1, g2, 0.0))


def _router(h, w_pad):
    T, D = h.shape
    tm = TM_ROUTER
    return pl.pallas_call(
        _router_kernel,
        out_shape=(jax.ShapeDtypeStruct((T, LANES), jnp.int32),
                   jax.ShapeDtypeStruct((T, LANES), F32)),
        grid=(T // tm,),
        in_specs=[pl.BlockSpec((tm, D), lambda i: (i, 0)),
                  pl.BlockSpec((D, LANES), lambda i: (0, 0))],
        out_specs=(pl.BlockSpec((tm, LANES), lambda i: (i, 0)),
                   pl.BlockSpec((tm, LANES), lambda i: (i, 0))),
        compiler_params=_cparams(("parallel",)),
        name="moe_router",
    )(h, w_pad)


def _gather_rows(src, idx):
    n = idx.shape[0]
    D = src.shape[1]
    per_worker = n // SC_WORKERS
    n_chunks = per_worker // SC_GATHER_CHUNK
    assert n_chunks * SC_GATHER_CHUNK * SC_WORKERS == n
    mesh = plsc.VectorSubcoreMesh(core_axis_name="c", subcore_axis_name="s")

    @functools.partial(
        pl.kernel, mesh=mesh,
        out_type=jax.ShapeDtypeStruct((n, D), src.dtype),
        scratch_types=[pltpu.VMEM((SC_GATHER_CHUNK,), jnp.int32),
                       pltpu.VMEM((SC_GATHER_CHUNK, D), src.dtype),
                       pltpu.SemaphoreType.DMA],
        name="moe_row_gather",
    )
    def gather(src_hbm, idx_hbm, out_hbm, idx_v, rows_v, sem):
        worker = lax.axis_index("s") * SC_CORES + lax.axis_index("c")
        base = worker * per_worker

        @pl.loop(0, n_chunks)
        def _(j):
            off = base + j * SC_GATHER_CHUNK
            pltpu.sync_copy(idx_hbm.at[pl.ds(off, SC_GATHER_CHUNK)], idx_v)
            pltpu.async_copy(src_hbm.at[idx_v], rows_v, sem).wait()
            pltpu.sync_copy(rows_v, out_hbm.at[pl.ds(off, SC_GATHER_CHUNK)])

    return gather(src, idx)


def _expert_kernel(be_ref, nact_ref, x_ref, w1_ref, w3_ref, w2_ref, o_ref):
    i = pl.program_id(0)
    f = pl.program_id(1)

    @pl.when(f == 0)
    def _():
        o_ref[...] = jnp.zeros_like(o_ref)

    @pl.when(i < nact_ref[0])
    def _():
        xb = x_ref[...].astype(BF16)
        a = _dot(xb, w1_ref[...])
        u = _dot(xb, w3_ref[...])
        o_ref[...] += _dot((_silu(a) * u).astype(BF16), w2_ref[...])


def _expert_ffn(xs, block_expert, n_active, w1, w3, w2):
    n_rows, D = xs.shape
    n_blocks = n_rows // MOE_BM
    n_ff = D_FF_EXPERT // MOE_TF

    def widx(i, f, be, nact):
        live = i < nact[0]
        return be[jnp.minimum(i, nact[0] - 1)], jnp.where(live, f, n_ff - 1)

    def w13_map(i, f, be, nact):
        e, ff = widx(i, f, be, nact)
        return e, 0, ff

    def w2_map(i, f, be, nact):
        e, ff = widx(i, f, be, nact)
        return e, ff, 0

    grid_spec = pltpu.PrefetchScalarGridSpec(
        num_scalar_prefetch=2,
        grid=(n_blocks, n_ff),
        in_specs=[pl.BlockSpec((MOE_BM, D), lambda i, f, be, nact: (i, 0)),
                  pl.BlockSpec((None, D, MOE_TF), w13_map),
                  pl.BlockSpec((None, D, MOE_TF), w13_map),
                  pl.BlockSpec((None, MOE_TF, D), w2_map)],
        out_specs=pl.BlockSpec((MOE_BM, D), lambda i, f, be, nact: (i, 0)),
    )
    return pl.pallas_call(
        _expert_kernel,
        out_shape=jax.ShapeDtypeStruct((n_rows, D), F32),
        grid_spec=grid_spec,
        compiler_params=_cparams(("arbitrary", "arbitrary")),
        name="moe_experts",
    )(block_expert, n_active, xs, w1, w3, w2)


def _combine_kernel(y_ref, gate_ref, h_ref, g_ref, b_ref, o_ref):
    D = h_ref.shape[1]
    gate = gate_ref[...]
    f = y_ref[:, :D] * gate[:, 0:1] + y_ref[:, D:] * gate[:, 1:2]
    o_ref[...] = _layer_norm(ALPHA * h_ref[...] + f, g_ref[...], b_ref[...])


def _combine(y_pairs, gates, h, g, b):
    T, D = h.shape
    tm = TM_COMBINE
    return pl.pallas_call(
        _combine_kernel,
        out_shape=jax.ShapeDtypeStruct((T, D), F32),
        grid=(T // tm,),
        in_specs=[pl.BlockSpec((tm, TOP_K * D), lambda i: (i, 0)),
                  pl.BlockSpec((tm, LANES), lambda i: (i, 0)),
                  pl.BlockSpec((tm, D), lambda i: (i, 0)),
                  pl.BlockSpec((1, D), lambda i: (0, 0)),
                  pl.BlockSpec((1, D), lambda i: (0, 0))],
        out_specs=pl.BlockSpec((tm, D), lambda i: (i, 0)),
        compiler_params=_cparams(("parallel",)),
        name="moe_combine_ln2",
    )(y_pairs, gates, h, g.reshape(1, D), b.reshape(1, D))


def _moe_ffn(h, router_w, w1, w3, w2, g, b):
    T, D = h.shape
    n_pairs = T * TOP_K
    n_blocks = n_pairs // MOE_BM + N_EXPERTS
    n_rows = n_blocks * MOE_BM
    w_pad = jnp.pad(router_w.astype(F32), ((0, 0), (0, LANES - N_EXPERTS)))
    idx_pad, gates_pad = _router(h, w_pad)
    expert_ids = idx_pad[:, :TOP_K].reshape(-1)
    onehot = (expert_ids[:, None] == jnp.arange(N_EXPERTS, dtype=jnp.int32)[None, :]).astype(jnp.int32)
    running = jnp.cumsum(onehot, axis=0)
    counts = running[-1]
    rank = jnp.sum(onehot * running, axis=1) - 1
    padded = (counts + MOE_BM - 1) // MOE_BM * MOE_BM
    ends = jnp.cumsum(padded)
    pos = (ends - padded)[expert_ids] + rank
    token_ids = jnp.arange(n_pairs, dtype=jnp.int32) // TOP_K
    row_token = jnp.zeros((n_rows,), jnp.int32).at[pos].set(token_ids)
    block_start = jnp.arange(n_blocks, dtype=jnp.int32) * MOE_BM
    block_expert = jnp.minimum(jnp.searchsorted(ends, block_start, side='right'),
                               N_EXPERTS - 1).astype(jnp.int32)
    n_active = (ends[-1:] // MOE_BM).astype(jnp.int32)

    xs = _gather_rows(h, row_token)
    ys = _expert_ffn(xs, block_expert, n_active, w1, w3, w2)
    y_pairs = _gather_rows(ys, pos.astype(jnp.int32)).reshape(T, TOP_K * D)
    return _combine(y_pairs, gates_pad, h, g, b)


def _relayout_w_in(w):
    sizes = (SSD_INNER, SSD_XBC, 2 * SSD_HEADS, HG_WIDTH, 2 * HG_WIDTH, HG_WIDTH, HG_WIDTH,
             2 * ML_WIDTH, ML_WIDTH, ML_WIDTH, 2 * ML_HEADS, 2 * ML_HEADS)
    offs = [0]
    for s in sizes:
        offs.append(offs[-1] + s)
    (z, xbc, dt, hq, hf, hi, hg, mqk, mv, mo, mig, mfg) = [w[:, offs[k]:offs[k + 1]] for k in range(len(sizes))]
    pad = jnp.zeros((w.shape[0], LANES - 2 * SSD_HEADS - 4 * ML_HEADS), w.dtype)
    wn = jnp.concatenate([z, mqk, hf, xbc, hq, hi, hg, mv, mo, dt, mig, mfg, pad], axis=1).astype(BF16)
    return wn, wn[:, D_PROJ_PAD - LANES:].T


def _gate_rows(vals_by_offset):
    row = jnp.zeros((LANES,), F32)
    for off, v in vals_by_offset:
        row = lax.dynamic_update_slice(row, v.reshape(-1).astype(F32), (off,))
    return row


def kernel(x, ln_in_g, ln_in_b, w_in, ssd_conv_w, ssd_conv_b, ssd_dt_bias, ssd_a_log, ssd_d, ssd_norm_w, hg_lb_logits, hg_norm_w, ml_conv_w, ml_conv_b, ml_ig_bias, ml_fg_bias, ml_norm_w, w_out, ln1_g, ln1_b, ln2_g, ln2_b, ffn_w1, ffn_w3, ffn_w2, moe_router, moe_w1, moe_w3, moe_w2):
    batch, seq_len, D = x.shape
    T = batch * seq_len
    f32 = lambda t: t.astype(F32)
    lb_soft = jax.nn.softmax(f32(hg_lb_logits), axis=0)
    lb_all = jnp.cumsum(lb_soft, axis=0) - lb_soft[0]

    h = _input_ln(f32(x).reshape(T, D), f32(ln_in_g), f32(ln_in_b))
    for l in range(DEPTH):
        w, wgt = _relayout_w_in(w_in[l])
        bias_row = _gate_rows(((GATE_DT, ssd_dt_bias[l]), (GATE_IG, ml_ig_bias[l]), (GATE_FG, ml_fg_bias[l])))
        alog_row = _gate_rows(((GATE_DT, ssd_a_log[l]),))
        bias_cr = jnp.broadcast_to(bias_row[:, None], (LANES, LANES))
        alog_cr = jnp.broadcast_to(alog_row[:, None], (LANES, LANES))
        bias_row = bias_row.reshape(1, LANES)
        alog_row = alog_row.reshape(1, LANES)

        proj, gates_t = _inproj(h, w, wgt)
        xbc_act = _conv_silu(proj, f32(ssd_conv_w[l]), f32(ssd_conv_b[l]), COLBLK_XBC, SSD_XBC, seq_len)
        mqk_act = _conv_silu(proj, f32(ml_conv_w[l]), f32(ml_conv_b[l]), COLBLK_MQK, 2 * ML_WIDTH, seq_len)
        y_f, y_b = _ssd_scan(xbc_act, proj, gates_t, bias_row, alog_row,
                             bias_cr[GATE_DT:GATE_DT + 2 * SSD_HEADS], alog_cr[GATE_DT:GATE_DT + 2 * SSD_HEADS],
                             batch)
        o_f, o_b = _hgrn2_scan(proj, lb_all[l], batch)
        m_f, m_b = _mlstm_scan(mqk_act, proj, gates_t, bias_row, bias_cr[GATE_IG:GATE_IG + 4 * ML_HEADS], batch)
        dskip = jnp.repeat(f32(ssd_d[l]), SSD_HEAD_DIM).reshape(1, SSD_INNER)
        h = _outproj(y_f, y_b, xbc_act, proj, o_f, o_b, m_f, m_b, h, dskip,
                     f32(ssd_norm_w[l]).reshape(1, -1), f32(hg_norm_w[l]).reshape(1, -1),
                     f32(ml_norm_w[l]).reshape(1, -1), w_out[l].astype(BF16), f32(ln1_g[l]), f32(ln1_b[l]))
        if l % 2 == 0:
            j = l // 2
            h = _dense_ffn(h, ffn_w1[j].astype(BF16), ffn_w3[j].astype(BF16), ffn_w2[j].astype(BF16),
                           f32(ln2_g[l]), f32(ln2_b[l]))
        else:
            j = l // 2
            h = _moe_ffn(h, moe_router[j], moe_w1[j].astype(BF16), moe_w3[j].astype(BF16),
                         moe_w2[j].astype(BF16), f32(ln2_g[l]), f32(ln2_b[l]))
    return h.reshape(batch, seq_len, D).astype(x.dtype)
```

```python
import functools

import jax
import jax.numpy as jnp
from jax import lax
from jax.experimental import pallas as pl
from jax.experimental.pallas import tpu as pltpu
from jax.experimental.pallas import tpu_sc as plsc

F32 = jnp.float32
BF16 = jnp.bfloat16
HIGHEST = lax.Precision.HIGHEST

D_MODEL = 1024
DEPTH = 4
D_MIX = 2 * D_MODEL
SSD_HEADS = 16
SSD_INNER = 1024
SSD_HEAD_DIM = 64
SSD_GROUPS = 2
SSD_STATE = 128
SSD_XBC = 1536
SSD_CHUNK = 128
HG_HEADS = 4
HG_WIDTH = 512
HG_DK = 128
ML_HEADS = 4
ML_WIDTH = 512
ML_DH = 128
CONV_WIDTH = 5
D_FF_DENSE = 2816
N_EXPERTS = 8
TOP_K = 2
D_FF_EXPERT = 3584
LN_EPS = 1e-5
RMS_EPS = 1e-6
MASK_NEG = -1e4
ALPHA = (2.0 * DEPTH) ** 0.25

LANES = 128
SUBLANES = 8
VMEM_LIMIT = 56 * 1024 * 1024

D_PROJ_PAD = 7168
COLBLK_Z = 0
COLBLK_MQK = 1
COLBLK_XBC = 2
COLBLK_HF_FWD = 4
COLBLK_HF_BWD = 5
COLBLK_HQ = 9
COLBLK_HI = 10
COLBLK_HG = 11
COLBLK_MV = 12
COLBLK_MO = 13
GATE_DT = 0
GATE_IG = 32
GATE_FG = 40

TM_LN = 512
TM_INPROJ = 256
TM_CONV = 512
SCAN_STEP = 128
HG_SUB = 64
HG_REFBLK = 4
TM_OUT = 256
TM_FFN = 512
TF_FFN = 256
TM_ROUTER = 512
MOE_BM = 1024
MOE_TF = 512
SC_CORES = 2
SC_WORKERS = 32
SC_GATHER_CHUNK = 64
TM_COMBINE = 256


def _cparams(sem):
    return pltpu.CompilerParams(dimension_semantics=sem, vmem_limit_bytes=VMEM_LIMIT)


def _iota(shape, dim):
    return lax.broadcasted_iota(jnp.int32, shape, dim)


def _sigmoid(x):
    return 1.0 / (1.0 + jnp.exp(-x))


def _silu(x):
    return x * _sigmoid(x)


def _softplus(x):
    return jnp.maximum(x, 0.0) + jnp.log1p(jnp.exp(-jnp.abs(x)))


def _log_sigmoid(x):
    return -_softplus(-x)


def _layer_norm(x, g, b):
    mu = jnp.mean(x, -1, keepdims=True)
    xc = x - mu
    var = jnp.mean(xc * xc, -1, keepdims=True)
    return xc * lax.rsqrt(var + LN_EPS) * g + b


def _causal(n, reverse):
    r = _iota((n, n), 0)
    c = _iota((n, n), 1)
    return (c >= r) if reverse else (c <= r)


def _dot(a, b):
    return jnp.dot(a, b, preferred_element_type=F32)


def _dot_nt(a, b):
    return lax.dot_general(a, b, (((1,), (1,)), ((), ())), preferred_element_type=F32)


def _dot_tn(a, b):
    return lax.dot_general(a, b, (((0,), (0,)), ((), ())), preferred_element_type=F32)


def _dot_hi(a, b):
    return jnp.dot(a, b, precision=HIGHEST, preferred_element_type=F32)


def _ln_kernel(x_ref, g_ref, b_ref, o_ref):
    o_ref[...] = _layer_norm(x_ref[...], g_ref[...], b_ref[...])


def _input_ln(x, g, b):
    T, D = x.shape
    return pl.pallas_call(
        _ln_kernel,
        out_shape=jax.ShapeDtypeStruct((T, D), F32),
        grid=(T // TM_LN,),
        in_specs=[pl.BlockSpec((TM_LN, D), lambda i: (i, 0)),
                  pl.BlockSpec((1, D), lambda i: (0, 0)),
                  pl.BlockSpec((1, D), lambda i: (0, 0))],
        out_specs=pl.BlockSpec((TM_LN, D), lambda i: (i, 0)),
        compiler_params=_cparams(("parallel",)),
        name="input_ln",
    )(x, g.reshape(1, D), b.reshape(1, D))


def _inproj_kernel(h_ref, w_ref, wgt_ref, proj_ref, gt_ref):
    hb = h_ref[...].astype(BF16)
    proj_ref[...] = _dot(hb, w_ref[...])
    gt_ref[...] = _dot_nt(wgt_ref[...], hb)


def _inproj(h, w, wgt):
    T, D = h.shape
    tm = TM_INPROJ
    return pl.pallas_call(
        _inproj_kernel,
        out_shape=(jax.ShapeDtypeStruct((T, D_PROJ_PAD), F32),
                   jax.ShapeDtypeStruct((LANES, T), F32)),
        grid=(T // tm,),
        in_specs=[pl.BlockSpec((tm, D), lambda i: (i, 0)),
                  pl.BlockSpec((D, D_PROJ_PAD), lambda i: (0, 0), pipeline_mode=pl.Buffered(1)),
                  pl.BlockSpec((LANES, D), lambda i: (0, 0), pipeline_mode=pl.Buffered(1))],
        out_specs=(pl.BlockSpec((tm, D_PROJ_PAD), lambda i: (i, 0)),
                   pl.BlockSpec((LANES, tm), lambda i: (0, i))),
        compiler_params=_cparams(("parallel",)),
        name="inproj",
    )(h, w, wgt)


def _conv_kernel(prev_ref, x_ref, next_ref, w_ref, b_ref, o_ref, *, tm, tiles_per_seq):
    i = pl.program_id(0)
    pos = i % tiles_per_seq
    x = x_ref[...]
    prev = jnp.where(pos == 0, 0.0, prev_ref[...])
    nxt = jnp.where(pos == tiles_per_seq - 1, 0.0, next_ref[...])
    xe = jnp.concatenate([prev, x, nxt], axis=0)
    n = tm + 2 * SUBLANES
    acc = b_ref[...] + w_ref[2:3, :] * x
    for k in (0, 1, 3, 4):
        shifted = pltpu.roll(xe, (2 - k) % n, 0)
        acc = acc + w_ref[k:k + 1, :] * shifted[SUBLANES:SUBLANES + tm]
    o_ref[...] = _silu(acc)


def _conv_silu(proj, w, b, colblk, width, seq_len):
    T = proj.shape[0]
    tm = TM_CONV
    rb = tm // SUBLANES
    nrb = T // SUBLANES
    kern = functools.partial(_conv_kernel, tm=tm, tiles_per_seq=seq_len // tm)
    return pl.pallas_call(
        kern,
        out_shape=jax.ShapeDtypeStruct((T, width), F32),
        grid=(T // tm,),
        in_specs=[pl.BlockSpec((SUBLANES, width), lambda i: (jnp.maximum(i * rb - 1, 0), colblk)),
                  pl.BlockSpec((tm, width), lambda i: (i, colblk)),
                  pl.BlockSpec((SUBLANES, width), lambda i: (jnp.minimum((i + 1) * rb, nrb - 1), colblk)),
                  pl.BlockSpec((CONV_WIDTH, width), lambda i: (0, 0)),
                  pl.BlockSpec((1, width), lambda i: (0, 0))],
        out_specs=pl.BlockSpec((tm, width), lambda i: (i, 0)),
        compiler_params=_cparams(("parallel",)),
        name="conv_silu",
    )(proj, proj, proj, w, b.reshape(1, width))


def _cumsum_lanes(x, reverse):
    n = x.shape[1]
    r = _iota((3 * n, n), 0) % n
    c = _iota((3 * n, n), 1)
    tri3 = ((r >= c) if reverse else (r <= c)).astype(BF16)
    return _dot(_split3(x, axis=1), tri3)


def _cols_of(rows_list):
    used = sum(r.shape[0] for r in rows_list)
    pad = jnp.zeros((LANES - used, LANES), F32)
    return jnp.concatenate(list(rows_list) + [pad], axis=0).T


def _ssd_gates(gates_t, bias_cr, alog_cr, reverse):
    last = 0 if reverse else SSD_CHUNK - 1
    dt_r = _softplus(gates_t + bias_cr)
    acs_r = _cumsum_lanes(dt_r * (-jnp.exp(alog_cr)), reverse)
    a_last = acs_r[:, last:last + 1]
    w_r = dt_r * jnp.exp(a_last - acs_r)
    return dt_r, acs_r, w_r, jnp.exp(a_last)


def _ssd_direction(x, gate_rows, acs_cols, st_ref, d, y_ref):
    reverse = d == 1
    n = SSD_CHUNK
    tri = _causal(n, reverse)
    lane_lo = _iota((n, LANES), 1) < SSD_HEAD_DIM
    dt_r, acs_r, w_r, dec = gate_rows

    for g in range(SSD_GROUPS):
        b0 = SSD_INNER + g * SSD_STATE
        c0 = SSD_INNER + SSD_GROUPS * SSD_STATE + g * SSD_STATE
        bg = x[:, b0:b0 + SSD_STATE]
        cgb = x[:, c0:c0 + SSD_STATE].astype(BF16)
        cb = _dot_nt(cgb, bg.astype(BF16))
        bg_t = bg.T
        for pair in range(4):
            p0 = (g * 4 + pair) * LANES
            xp = x[:, p0:p0 + LANES]
            ypair = None
            for hh in range(2):
                h = g * 8 + pair * 2 + hh
                c = d * SSD_HEADS + h
                xm = jnp.where(lane_lo if hh == 0 else jnp.logical_not(lane_lo), xp, 0.0).astype(BF16)
                col = jnp.broadcast_to(acs_cols[:, c:c + 1], (n, n))
                decay = jnp.exp(jnp.where(tri, col - acs_r[h:h + 1, :], MASK_NEG))
                m = (cb * decay * dt_r[h:h + 1, :]).astype(BF16)
                st = st_ref[d, h]
                y = _dot(m, xm) + jnp.exp(col) * _dot(cgb, st.astype(BF16))
                bh = (bg_t * w_r[h:h + 1, :]).astype(BF16)
                st_ref[d, h] = st * dec[h:h + 1, :] + _dot(bh, xm)
                ypair = y if ypair is None else ypair + y
            y_ref[:, p0:p0 + LANES] = ypair


def _ssd_kernel(xf_ref, xb_ref, gtf_ref, gtb_ref, bcr_ref, acr_ref, yf_ref, yb_ref, st_ref):
    @pl.when(pl.program_id(1) == 0)
    def _():
        st_ref[...] = jnp.zeros_like(st_ref)

    gf = _ssd_gates(gtf_ref[...], bcr_ref[0:SSD_HEADS, :], acr_ref[0:SSD_HEADS, :], False)
    gb = _ssd_gates(gtb_ref[...], bcr_ref[SSD_HEADS:2 * SSD_HEADS, :], acr_ref[SSD_HEADS:2 * SSD_HEADS, :], True)
    acs_cols = _cols_of([gf[1], gb[1]])
    _ssd_direction(xf_ref[...], gf, acs_cols, st_ref, 0, yf_ref)
    _ssd_direction(xb_ref[...], gb, acs_cols, st_ref, 1, yb_ref)


def _ssd_scan(xbc_act, gates_t, bias_cr, alog_cr, batch):
    T = xbc_act.shape[0]
    n = T // batch // SSD_CHUNK
    fwd = lambda b, i: b * n + i
    bwd = lambda b, i: b * n + (n - 1 - i)
    const = lambda b, i: (0, 0)
    return pl.pallas_call(
        _ssd_kernel,
        out_shape=(jax.ShapeDtypeStruct((T, SSD_INNER), F32),
                   jax.ShapeDtypeStruct((T, SSD_INNER), F32)),
        grid=(batch, n),
        in_specs=[pl.BlockSpec((SSD_CHUNK, SSD_XBC), lambda b, i: (fwd(b, i), 0)),
                  pl.BlockSpec((SSD_CHUNK, SSD_XBC), lambda b, i: (bwd(b, i), 0)),
                  pl.BlockSpec((SSD_HEADS, SSD_CHUNK), lambda b, i: (0, fwd(b, i))),
                  pl.BlockSpec((SSD_HEADS, SSD_CHUNK), lambda b, i: (1, bwd(b, i))),
                  pl.BlockSpec((2 * SSD_HEADS, LANES), const),
                  pl.BlockSpec((2 * SSD_HEADS, LANES), const)],
        out_specs=(pl.BlockSpec((SSD_CHUNK, SSD_INNER), lambda b, i: (fwd(b, i), 0)),
                   pl.BlockSpec((SSD_CHUNK, SSD_INNER), lambda b, i: (bwd(b, i), 0))),
        scratch_shapes=[pltpu.VMEM((2, SSD_HEADS, SSD_STATE, LANES), F32)],
        compiler_params=_cparams(("arbitrary", "arbitrary")),
        name="ssd_scan",
    )(xbc_act, xbc_act, gates_t, gates_t, bias_cr, alog_cr)


def _split3(x, axis=0):
    hi = x.astype(BF16)
    r1 = x - hi.astype(F32)
    mid = r1.astype(BF16)
    lo = (r1 - mid.astype(F32)).astype(BF16)
    return jnp.concatenate([hi, mid, lo], axis=axis)


def _hgrn2_cumsum_lhs(reverse):
    n = HG_SUB
    r_i = _iota((2 * n, n), 0)
    c_i = _iota((2 * n, n), 1)
    t = jnp.where(r_i < n, r_i, r_i - n)
    ref = (t // HG_REFBLK) * HG_REFBLK + (HG_REFBLK - 1 if reverse else 0)
    upto = jnp.where(r_i < n, t, ref)
    m = ((c_i >= upto) if reverse else (c_i <= upto)).astype(BF16)
    return jnp.concatenate([m, m, m], axis=1)


def _hgrn2_intra(q, f, v, lb, qmask, reverse):
    n = HG_SUB
    nblk = n // HG_REFBLK
    slab = 2 * SUBLANES
    tri = _causal(n, reverse)
    last = 0 if reverse else n - 1
    rowblk = _iota((slab, HG_DK), 0) // HG_REFBLK

    qh = _silu(q)
    key = (1.0 - lb) / (1.0 + jnp.exp(f))
    g = jnp.log1p(-key)
    gg = _dot(_hgrn2_cumsum_lhs(reverse), _split3(g))
    gc = gg[:n]
    gref = gg[n:]
    qg = (qh * jnp.exp(gc - gref)).astype(BF16)
    qs = (qh * jnp.exp(gc)).astype(BF16)
    g_last = gc[last:last + 1, :]
    kd = (key * jnp.exp(g_last - gc)).astype(BF16)
    dec = jnp.exp(g_last)
    vb = v.astype(BF16)

    o_intra = []
    for h in range(HG_HEADS):
        cols = slice(h * HG_DK, (h + 1) * HG_DK)
        key_h = key[:, cols]
        gc_h = gc[:, cols]
        q_ext = jnp.tile(qg[:, cols], (1, nblk)) * qmask
        k_parts = []
        for j in range(nblk):
            ref_row = j * HG_REFBLK + (HG_REFBLK - 1 if reverse else 0)
            rj = gc_h[ref_row:ref_row + 1, :]
            edge = (j * HG_REFBLK) // slab
            slabs = range(edge, n // slab) if reverse else range(0, edge + 1)
            pieces = []
            for sidx in range(n // slab):
                rows = slice(sidx * slab, (sidx + 1) * slab)
                if sidx not in slabs:
                    pieces.append(jnp.zeros((slab, HG_DK), BF16))
                    continue
                arg = rj - gc_h[rows]
                if sidx == edge:
                    jb = j - edge * (slab // HG_REFBLK)
                    live = (rowblk >= jb) if reverse else (rowblk <= jb)
                    arg = jnp.where(live, arg, MASK_NEG)
                pieces.append((key_h[rows] * jnp.exp(arg)).astype(BF16))
            k_parts.append(jnp.concatenate(pieces, axis=0))
        att = _dot_nt(q_ext, jnp.concatenate(k_parts, axis=1))
        att = jnp.where(tri, att, 0.0).astype(BF16)
        o_intra.append(_dot(att, vb[:, cols]))
    return o_intra, qs, kd, dec, vb


def _hgrn2_state(intra, st_ref, d, o_ref, rows):
    o_intra, qs, kd, dec, vb = intra
    for h in range(HG_HEADS):
        cols = slice(h * HG_DK, (h + 1) * HG_DK)
        st = st_ref[d, h]
        o_ref[rows, cols] = o_intra[h] + _dot_nt(qs[:, cols], st.astype(BF16))
        st_ref[d, h] = st * dec[:, cols] + _dot_tn(vb[:, cols], kd[:, cols])


def _hgrn2_kernel(qf_ref, ff_ref, vf_ref, qb_ref, fb_ref, vb_ref, lb_ref, qmask_ref, of_ref, ob_ref, st_ref):
    @pl.when(pl.program_id(1) == 0)
    def _():
        st_ref[...] = jnp.zeros_like(st_ref)

    lb = lb_ref[...]
    qmask = qmask_ref[...]
    refs = ((qf_ref, ff_ref, vf_ref, of_ref), (qb_ref, fb_ref, vb_ref, ob_ref))
    nsub = SCAN_STEP // HG_SUB
    units = []
    for k in range(nsub):
        units.append((0, k))
        units.append((1, nsub - 1 - k))
    pending = None
    for d, sc in units:
        rows = slice(sc * HG_SUB, (sc + 1) * HG_SUB)
        q_ref, f_ref, v_ref, o_ref = refs[d]
        intra = _hgrn2_intra(q_ref[rows, :], f_ref[rows, :], v_ref[rows, :], lb, qmask, d == 1)
        if pending is not None:
            _hgrn2_state(*pending)
        pending = (intra, st_ref, d, o_ref, rows)
    _hgrn2_state(*pending)


def _hgrn2_scan(proj, lb, batch):
    T = proj.shape[0]
    n = T // batch // SCAN_STEP
    fwd = lambda b, i: b * n + i
    bwd = lambda b, i: b * n + (n - 1 - i)
    blk = (SCAN_STEP, HG_WIDTH)
    nblk = HG_SUB // HG_REFBLK
    qmask = (jnp.arange(HG_SUB)[:, None] // HG_REFBLK == jnp.arange(nblk * HG_DK)[None, :] // HG_DK).astype(BF16)
    return pl.pallas_call(
        _hgrn2_kernel,
        out_shape=(jax.ShapeDtypeStruct((T, HG_WIDTH), F32),
                   jax.ShapeDtypeStruct((T, HG_WIDTH), F32)),
        grid=(batch, n),
        in_specs=[pl.BlockSpec(blk, lambda b, i: (fwd(b, i), COLBLK_HQ)),
                  pl.BlockSpec(blk, lambda b, i: (fwd(b, i), COLBLK_HF_FWD)),
                  pl.BlockSpec(blk, lambda b, i: (fwd(b, i), COLBLK_HI)),
                  pl.BlockSpec(blk, lambda b, i: (bwd(b, i), COLBLK_HQ)),
                  pl.BlockSpec(blk, lambda b, i: (bwd(b, i), COLBLK_HF_BWD)),
                  pl.BlockSpec(blk, lambda b, i: (bwd(b, i), COLBLK_HI)),
                  pl.BlockSpec((1, HG_WIDTH), lambda b, i: (0, 0)),
                  pl.BlockSpec((HG_SUB, nblk * HG_DK), lambda b, i: (0, 0))],
        out_specs=(pl.BlockSpec(blk, lambda b, i: (fwd(b, i), 0)),
                   pl.BlockSpec(blk, lambda b, i: (bwd(b, i), 0))),
        scratch_shapes=[pltpu.VMEM((2, HG_HEADS, HG_DK, HG_DK), F32)],
        compiler_params=_cparams(("arbitrary", "arbitrary")),
        name="hgrn2_scan",
    )(proj, proj, proj, proj, proj, proj, lb.reshape(1, HG_WIDTH), qmask)


def _mlstm_kernel(batch, qkf_ref, vf_ref, qkb_ref, vb_ref, *rest):
    gt_refs = (rest[:batch], rest[batch:2 * batch])
    bcr_ref, of_ref, ob_ref, c_ref, m_ref = rest[2 * batch:]

    @pl.when(pl.program_id(0) == 0)
    def _():
        c_ref[...] = jnp.zeros_like(c_ref)
        m_ref[...] = jnp.zeros_like(m_ref)

    n = SCAN_STEP
    ones = jnp.ones((n, ML_DH), BF16)
    io_refs = ((qkf_ref, vf_ref, of_ref), (qkb_ref, vb_ref, ob_ref))
    tris = (_causal(n, False), _causal(n, True))
    eye = _iota((n, n), 0) == _iota((n, n), 1)
    lane = _iota((SUBLANES, n), 1)

    gates = {}
    for b in range(batch):
        for d in range(2):
            reverse = d == 1
            last = 0 if reverse else n - 1
            g_r = gt_refs[d][b][...] + bcr_ref[...]
            bc = _cumsum_lanes(_log_sigmoid(g_r), reverse)[SUBLANES:]
            ig = g_r[:SUBLANES]
            u = ig - bc
            pm = u
            for sh in (1, 2, 4, 8, 16, 32, 64):
                shifted = pltpu.roll(pm, (n - sh) if reverse else sh, 1)
                valid = (lane < n - sh) if reverse else (lane >= sh)
                pm = jnp.maximum(pm, jnp.where(valid, shifted, -jnp.inf))
            m_prev = m_ref[b * 2 + d]
            floor = jnp.where((lane > 0) if reverse else (lane < n - 1), MASK_NEG, -jnp.inf)
            m_t = jnp.maximum(jnp.maximum(bc + pm, bc + m_prev), floor)
            b_last = bc[:, last:last + 1]
            m_new = jnp.maximum(b_last + m_prev[:, 0:1], b_last + pm[:, last:last + 1])
            w_s = jnp.exp(b_last - bc + ig - m_new)
            dec = jnp.exp(b_last + m_prev[:, 0:1] - m_new)
            m_ref[b * 2 + d] = jnp.broadcast_to(m_new, (SUBLANES, LANES))
            gates[b, d] = (u, _cols_of([bc, m_t]), m_prev, w_s, dec)

    units = [(b, d, h) for h in range(ML_HEADS) for b in range(batch) for d in range(2)]
    for b, d, h in units:
        u, cols, m_prev, w_s, dec = gates[b, d]
        qk_ref, v_ref, o_ref = io_refs[d]
        r = d * ML_HEADS + h
        q = (qk_ref[b, :, h * ML_DH:(h + 1) * ML_DH] * (ML_DH ** -0.5)).astype(BF16)
        k = qk_ref[b, :, ML_WIDTH + h * ML_DH:ML_WIDTH + (h + 1) * ML_DH].astype(BF16)
        ve = jnp.concatenate([v_ref[b, :, h * ML_DH:(h + 1) * ML_DH].astype(BF16), ones], axis=1)
        b_minus_m = jnp.broadcast_to(cols[:, r:r + 1] - cols[:, SUBLANES + r:SUBLANES + r + 1], (n, n))
        m_t = jnp.broadcast_to(cols[:, SUBLANES + r:SUBLANES + r + 1], (n, n))
        w_intra = jnp.exp(jnp.where(tris[d], b_minus_m + u[r:r + 1, :], MASK_NEG))
        w_inter = jnp.exp(b_minus_m + m_prev[r:r + 1, 0:1])
        s = (_dot_nt(q, k) * w_intra).astype(BF16)
        cst = c_ref[b, d, h]
        numden = _dot(s, ve) + jnp.concatenate([w_inter, w_inter], axis=1) * _dot(q, cst.astype(BF16))
        den = numden[:, ML_DH:]
        o_ref[b, :, h * ML_DH:(h + 1) * ML_DH] = numden[:, :ML_DH] / jnp.maximum(jnp.abs(den), jnp.exp(-m_t))
        diag = jnp.where(eye, jnp.broadcast_to(w_s[r:r + 1, :], (n, n)), 0.0).astype(BF16)
        c_ref[b, d, h] = dec[r:r + 1, :] * cst + _dot_tn(k, _dot(diag, ve).astype(BF16))


def _mlstm_scan(mqk_act, proj, gates_t, bias_cr, batch):
    T = proj.shape[0]
    seq = T // batch
    n = seq // SCAN_STEP
    gt_rowblk = GATE_IG // (2 * SUBLANES)
    fwd3 = lambda cb: (lambda i: (0, i, cb))
    bwd3 = lambda cb: (lambda i: (0, n - 1 - i, cb))
    gt_fwd = [pl.BlockSpec((2 * SUBLANES, SCAN_STEP), functools.partial(lambda b, i: (gt_rowblk, b * n + i), b))
              for b in range(batch)]
    gt_bwd = [pl.BlockSpec((2 * SUBLANES, SCAN_STEP),
                           functools.partial(lambda b, i: (gt_rowblk, b * n + n - 1 - i), b))
              for b in range(batch)]
    qk3 = mqk_act.reshape(batch, seq, 2 * ML_WIDTH)
    proj3 = proj.reshape(batch, seq, D_PROJ_PAD)
    o_f, o_b = pl.pallas_call(
        functools.partial(_mlstm_kernel, batch),
        out_shape=(jax.ShapeDtypeStruct((batch, seq, ML_WIDTH), F32),
                   jax.ShapeDtypeStruct((batch, seq, ML_WIDTH), F32)),
        grid=(n,),
        in_specs=[pl.BlockSpec((batch, SCAN_STEP, 2 * ML_WIDTH), fwd3(0)),
                  pl.BlockSpec((batch, SCAN_STEP, ML_WIDTH), fwd3(COLBLK_MV)),
                  pl.BlockSpec((batch, SCAN_STEP, 2 * ML_WIDTH), bwd3(0)),
                  pl.BlockSpec((batch, SCAN_STEP, ML_WIDTH), bwd3(COLBLK_MV))]
                 + gt_fwd + gt_bwd
                 + [pl.BlockSpec((2 * SUBLANES, LANES), lambda i: (0, 0))],
        out_specs=(pl.BlockSpec((batch, SCAN_STEP, ML_WIDTH), fwd3(0)),
                   pl.BlockSpec((batch, SCAN_STEP, ML_WIDTH), bwd3(0))),
        scratch_shapes=[pltpu.VMEM((batch, 2, ML_HEADS, ML_DH, 2 * ML_DH), F32),
                        pltpu.VMEM((batch * 2, SUBLANES, LANES), F32)],
        compiler_params=_cparams(("arbitrary",)),
        name="mlstm_scan",
    )(qk3, proj3, qk3, proj3, *([gates_t] * (2 * batch)), bias_cr)
    return o_f.reshape(T, ML_WIDTH), o_b.reshape(T, ML_WIDTH)


def _outproj_kernel(yf_ref, yb_ref, xs_ref, z_ref, of_ref, ob_ref, hg_ref, mf_ref, mb_ref, mo_ref,
                    h_ref, dskip_ref, ssdw_ref, hgw_ref, mlw_ref, wout_ref, g_ref, b_ref, o_ref):
    y = (yf_ref[...] + yb_ref[...] + xs_ref[...] * dskip_ref[...]) * _silu(z_ref[...])
    gw = SSD_INNER // SSD_GROUPS
    acc = None
    for g in range(SSD_GROUPS):
        seg = y[:, g * gw:(g + 1) * gw]
        seg = seg * lax.rsqrt(jnp.mean(seg * seg, -1, keepdims=True) + RMS_EPS) * ssdw_ref[:, g * gw:(g + 1) * gw]
        part = _dot(seg.astype(BF16), wout_ref[g * gw:(g + 1) * gw, :])
        acc = part if acc is None else acc + part
    o = of_ref[...] + ob_ref[...]
    hgate = _silu(hg_ref[...])
    m = mf_ref[...] + mb_ref[...]
    mgate = _sigmoid(mo_ref[...])
    for h in range(HG_HEADS):
        cols = slice(h * HG_DK, (h + 1) * HG_DK)
        seg = o[:, cols]
        seg = seg * lax.rsqrt(jnp.mean(seg * seg, -1, keepdims=True) + RMS_EPS) * hgw_ref[:, cols] * hgate[:, cols]
        r0 = SSD_INNER + h * HG_DK
        acc = acc + _dot(seg.astype(BF16), wout_ref[r0:r0 + HG_DK, :])
    for h in range(ML_HEADS):
        cols = slice(h * ML_DH, (h + 1) * ML_DH)
        seg = m[:, cols]
        seg = seg - jnp.mean(seg, -1, keepdims=True)
        seg = seg * lax.rsqrt(jnp.mean(seg * seg, -1, keepdims=True) + LN_EPS) * mlw_ref[:, cols] * mgate[:, cols]
        r0 = SSD_INNER + HG_WIDTH + h * ML_DH
        acc = acc + _dot(seg.astype(BF16), wout_ref[r0:r0 + ML_DH, :])
    o_ref[...] = _layer_norm(ALPHA * h_ref[...] + acc, g_ref[...], b_ref[...])


def _outproj(y_f, y_b, xbc_act, proj, o_f, o_b, m_f, m_b, h, dskip, ssd_w, hg_w, ml_w, w_out, g, b):
    T, D = h.shape
    tm = TM_OUT
    row = lambda width, cb: pl.BlockSpec((tm, width), lambda i: (i, cb))
    par = lambda width: pl.BlockSpec((1, width), lambda i: (0, 0))
    return pl.pallas_call(
        _outproj_kernel,
        out_shape=jax.ShapeDtypeStruct((T, D), F32),
        grid=(T // tm,),
        in_specs=[row(SSD_INNER, 0), row(SSD_INNER, 0), row(SSD_INNER, 0), row(SSD_INNER, COLBLK_Z),
                  row(HG_WIDTH, 0), row(HG_WIDTH, 0), row(HG_WIDTH, COLBLK_HG),
                  row(ML_WIDTH, 0), row(ML_WIDTH, 0), row(ML_WIDTH, COLBLK_MO),
                  row(D, 0), par(SSD_INNER), par(SSD_INNER), par(HG_WIDTH), par(ML_WIDTH),
                  pl.BlockSpec((D_MIX, D), lambda i: (0, 0), pipeline_mode=pl.Buffered(1)),
                  par(D), par(D)],
        out_specs=pl.BlockSpec((tm, D), lambda i: (i, 0)),
        compiler_params=_cparams(("parallel",)),
        name="outproj_ln1",
    )(y_f, y_b, xbc_act, proj, o_f, o_b, proj, m_f, m_b, proj, h, dskip, ssd_w, hg_w, ml_w, w_out,
      g.reshape(1, D), b.reshape(1, D))


def _ffn_kernel(h_ref, w1_ref, w3_ref, w2_ref, g_ref, b_ref, o_ref, acc_ref):
    h = h_ref[...]
    hb = h.astype(BF16)
    for c in range(D_FF_DENSE // TF_FFN):
        cols = slice(c * TF_FFN, (c + 1) * TF_FFN)
        a = _dot(hb, w1_ref[:, cols])
        u = _dot(hb, w3_ref[:, cols])
        part = _dot((_silu(a) * u).astype(BF16), w2_ref[cols, :])
        if c == 0:
            acc_ref[...] = part
        else:
            acc_ref[...] += part
    o_ref[...] = _layer_norm(ALPHA * h + acc_ref[...], g_ref[...], b_ref[...])


def _dense_ffn(h, w1, w3, w2, g, b):
    T, D = h.shape
    tm = TM_FFN
    resident = lambda shape: pl.BlockSpec(shape, lambda i: (0, 0), pipeline_mode=pl.Buffered(1))
    return pl.pallas_call(
        _ffn_kernel,
        out_shape=jax.ShapeDtypeStruct((T, D), F32),
        grid=(T // tm,),
        in_specs=[pl.BlockSpec((tm, D), lambda i: (i, 0)),
                  resident((D, D_FF_DENSE)), resident((D, D_FF_DENSE)), resident((D_FF_DENSE, D)),
                  pl.BlockSpec((1, D), lambda i: (0, 0)), pl.BlockSpec((1, D), lambda i: (0, 0))],
        out_specs=pl.BlockSpec((tm, D), lambda i: (i, 0)),
        scratch_shapes=[pltpu.VMEM((tm, D), F32)],
        compiler_params=_cparams(("parallel",)),
        name="dense_ffn_ln2",
    )(h, w1, w3, w2, g.reshape(1, D), b.reshape(1, D))


def _router_kernel(h_ref, w_ref, idx_ref, gate_ref):
    logits = _dot_hi(h_ref[...], w_ref[...])
    lane = _iota(logits.shape, 1)
    lane_f = lane.astype(F32)
    x = jnp.where(lane < N_EXPERTS, logits, -jnp.inf)
    m1 = jnp.max(x, axis=1, keepdims=True)
    i1 = jnp.min(jnp.where(x == m1, lane_f, float(LANES)), axis=1, keepdims=True)
    x2 = jnp.where(lane_f == i1, -jnp.inf, x)
    m2 = jnp.max(x2, axis=1, keepdims=True)
    i2 = jnp.min(jnp.where(x2 == m2, lane_f, float(LANES)), axis=1, keepdims=True)
    e2 = jnp.exp(m2 - m1)
    g1 = 1.0 / (1.0 + e2)
    g2 = e2 / (1.0 + e2)
    idx_ref[...] = jnp.where(lane == 0, i1, jnp.where(lane == 1, i2, 0.0)).astype(jnp.int32)
    gate_ref[...] = jnp.where(lane == 0, g1, jnp.where(lane == 1, g2, 0.0))


def _router(h, w_pad):
    T, D = h.shape
    tm = TM_ROUTER
    return pl.pallas_call(
        _router_kernel,
        out_shape=(jax.ShapeDtypeStruct((T, LANES), jnp.int32),
                   jax.ShapeDtypeStruct((T, LANES), F32)),
        grid=(T // tm,),
        in_specs=[pl.BlockSpec((tm, D), lambda i: (i, 0)),
                  pl.BlockSpec((D, LANES), lambda i: (0, 0))],
        out_specs=(pl.BlockSpec((tm, LANES), lambda i: (i, 0)),
                   pl.BlockSpec((tm, LANES), lambda i: (i, 0))),
        compiler_params=_cparams(("parallel",)),
        name="moe_router",
    )(h, w_pad)


def _gather_rows(src, idx):
    n = idx.shape[0]
    D = src.shape[1]
    per_worker = n // SC_WORKERS
    n_chunks = per_worker // SC_GATHER_CHUNK
    assert n_chunks * SC_GATHER_CHUNK * SC_WORKERS == n
    mesh = plsc.VectorSubcoreMesh(core_axis_name="c", subcore_axis_name="s")

    @functools.partial(
        pl.kernel, mesh=mesh,
        out_type=jax.ShapeDtypeStruct((n, D), src.dtype),
        scratch_types=[pltpu.VMEM((SC_GATHER_CHUNK,), jnp.int32),
                       pltpu.VMEM((SC_GATHER_CHUNK, D), src.dtype),
                       pltpu.SemaphoreType.DMA],
        name="moe_row_gather",
    )
    def gather(src_hbm, idx_hbm, out_hbm, idx_v, rows_v, sem):
        worker = lax.axis_index("s") * SC_CORES + lax.axis_index("c")
        base = worker * per_worker

        @pl.loop(0, n_chunks)
        def _(j):
            off = base + j * SC_GATHER_CHUNK
            pltpu.sync_copy(idx_hbm.at[pl.ds(off, SC_GATHER_CHUNK)], idx_v)
            pltpu.async_copy(src_hbm.at[idx_v], rows_v, sem).wait()
            pltpu.sync_copy(rows_v, out_hbm.at[pl.ds(off, SC_GATHER_CHUNK)])

    return gather(src, idx)


def _expert_kernel(be_ref, nact_ref, x_ref, w1_ref, w3_ref, w2_ref, o_ref):
    i = pl.program_id(0)
    f = pl.program_id(1)

    @pl.when(f == 0)
    def _():
        o_ref[...] = jnp.zeros_like(o_ref)

    @pl.when(i < nact_ref[0])
    def _():
        xb = x_ref[...].astype(BF16)
        a = _dot(xb, w1_ref[...].astype(BF16))
        u = _dot(xb, w3_ref[...].astype(BF16))
        o_ref[...] += _dot((_silu(a) * u).astype(BF16), w2_ref[...].astype(BF16))


def _expert_ffn(xs, block_expert, n_active, w1, w3, w2):
    n_rows, D = xs.shape
    n_blocks = n_rows // MOE_BM
    n_ff = D_FF_EXPERT // MOE_TF

    def widx(i, f, be, nact):
        live = i < nact[0]
        return be[jnp.minimum(i, nact[0] - 1)], jnp.where(live, f, n_ff - 1)

    def w13_map(i, f, be, nact):
        e, ff = widx(i, f, be, nact)
        return e, 0, ff

    def w2_map(i, f, be, nact):
        e, ff = widx(i, f, be, nact)
        return e, ff, 0

    grid_spec = pltpu.PrefetchScalarGridSpec(
        num_scalar_prefetch=2,
        grid=(n_blocks, n_ff),
        in_specs=[pl.BlockSpec((MOE_BM, D), lambda i, f, be, nact: (i, 0)),
                  pl.BlockSpec((None, D, MOE_TF), w13_map),
                  pl.BlockSpec((None, D, MOE_TF), w13_map),
                  pl.BlockSpec((None, MOE_TF, D), w2_map)],
        out_specs=pl.BlockSpec((MOE_BM, D), lambda i, f, be, nact: (i, 0)),
    )
    return pl.pallas_call(
        _expert_kernel,
        out_shape=jax.ShapeDtypeStruct((n_rows, D), F32),
        grid_spec=grid_spec,
        compiler_params=_cparams(("arbitrary", "arbitrary")),
        name="moe_experts",
    )(block_expert, n_active, xs, w1, w3, w2)


def _combine_kernel(y0_ref, y1_ref, gate_ref, h_ref, g_ref, b_ref, o_ref):
    gate = gate_ref[...]
    f = y0_ref[...] * gate[:, 0:1] + y1_ref[...] * gate[:, 1:2]
    o_ref[...] = _layer_norm(ALPHA * h_ref[...] + f, g_ref[...], b_ref[...])


def _combine(y_slots, gates, h, g, b):
    T, D = h.shape
    tm = TM_COMBINE
    return pl.pallas_call(
        _combine_kernel,
        out_shape=jax.ShapeDtypeStruct((T, D), F32),
        grid=(T // tm,),
        in_specs=[pl.BlockSpec((tm, D), lambda i: (i, 0)),
                  pl.BlockSpec((tm, D), lambda i: (i + T // tm, 0)),
                  pl.BlockSpec((tm, LANES), lambda i: (i, 0)),
                  pl.BlockSpec((tm, D), lambda i: (i, 0)),
                  pl.BlockSpec((1, D), lambda i: (0, 0)),
                  pl.BlockSpec((1, D), lambda i: (0, 0))],
        out_specs=pl.BlockSpec((tm, D), lambda i: (i, 0)),
        compiler_params=_cparams(("parallel",)),
        name="moe_combine_ln2",
    )(y_slots, y_slots, gates, h, g.reshape(1, D), b.reshape(1, D))


def _moe_ffn(h, router_w, w1, w3, w2, g, b):
    T, D = h.shape
    n_pairs = T * TOP_K
    n_blocks = n_pairs // MOE_BM + N_EXPERTS
    n_rows = n_blocks * MOE_BM
    w_pad = jnp.pad(router_w.astype(F32), ((0, 0), (0, LANES - N_EXPERTS)))
    idx_pad, gates_pad = _router(h, w_pad)
    expert_ids = idx_pad[:, :TOP_K].reshape(-1)
    onehot = (expert_ids[:, None] == jnp.arange(N_EXPERTS, dtype=jnp.int32)[None, :]).astype(jnp.int32)
    running = jnp.cumsum(onehot, axis=0)
    counts = running[-1]
    rank = jnp.sum(onehot * running, axis=1) - 1
    padded = (counts + MOE_BM - 1) // MOE_BM * MOE_BM
    ends = jnp.cumsum(padded)
    pos = (ends - padded)[expert_ids] + rank
    token_ids = jnp.arange(n_pairs, dtype=jnp.int32) // TOP_K
    row_token = jnp.zeros((n_rows,), jnp.int32).at[pos].set(token_ids)
    block_start = jnp.arange(n_blocks, dtype=jnp.int32) * MOE_BM
    block_expert = jnp.minimum(jnp.searchsorted(ends, block_start, side='right'),
                               N_EXPERTS - 1).astype(jnp.int32)
    n_active = (ends[-1:] // MOE_BM).astype(jnp.int32)

    xs = _gather_rows(h, row_token)
    ys = _expert_ffn(xs, block_expert, n_active, w1, w3, w2)
    slot_major = pos.astype(jnp.int32).reshape(T, TOP_K).T.reshape(-1)
    y_slots = _gather_rows(ys, slot_major)
    return _combine(y_slots, gates_pad, h, g, b)


def _relayout_w_in(w):
    sizes = (SSD_INNER, SSD_XBC, 2 * SSD_HEADS, HG_WIDTH, 2 * HG_WIDTH, HG_WIDTH, HG_WIDTH,
             2 * ML_WIDTH, ML_WIDTH, ML_WIDTH, 2 * ML_HEADS, 2 * ML_HEADS)
    offs = [0]
    for s in sizes:
        offs.append(offs[-1] + s)
    (z, xbc, dt, hq, hf, hi, hg, mqk, mv, mo, mig, mfg) = [w[:, offs[k]:offs[k + 1]] for k in range(len(sizes))]
    pad = jnp.zeros((w.shape[0], LANES - 2 * SSD_HEADS - 4 * ML_HEADS), w.dtype)
    wn = jnp.concatenate([z, mqk, hf, xbc, hq, hi, hg, mv, mo], axis=1).astype(BF16)
    wgt = jnp.concatenate([dt, mig, mfg, pad], axis=1).astype(BF16).T
    return wn, wgt


def _gate_rows(vals_by_offset):
    row = jnp.zeros((LANES,), F32)
    for off, v in vals_by_offset:
        row = lax.dynamic_update_slice(row, v.reshape(-1).astype(F32), (off,))
    return jnp.broadcast_to(row[:, None], (LANES, LANES))


def kernel(x, ln_in_g, ln_in_b, w_in, ssd_conv_w, ssd_conv_b, ssd_dt_bias, ssd_a_log, ssd_d, ssd_norm_w, hg_lb_logits, hg_norm_w, ml_conv_w, ml_conv_b, ml_ig_bias, ml_fg_bias, ml_norm_w, w_out, ln1_g, ln1_b, ln2_g, ln2_b, ffn_w1, ffn_w3, ffn_w2, moe_router, moe_w1, moe_w3, moe_w2):
    batch, seq_len, D = x.shape
    T = batch * seq_len
    f32 = lambda t: t.astype(F32)
    lb_soft = jax.nn.softmax(f32(hg_lb_logits), axis=0)
    lb_all = jnp.cumsum(lb_soft, axis=0) - lb_soft[0]

    h = _input_ln(f32(x).reshape(T, D), f32(ln_in_g), f32(ln_in_b))
    for l in range(DEPTH):
        w, wgt = _relayout_w_in(w_in[l])
        bias_cr = _gate_rows(((GATE_DT, ssd_dt_bias[l]), (GATE_IG, ml_ig_bias[l]), (GATE_FG, ml_fg_bias[l])))
        alog_cr = _gate_rows(((GATE_DT, ssd_a_log[l]),))

        proj, gates_t = _inproj(h, w, wgt)
        xbc_act = _conv_silu(proj, f32(ssd_conv_w[l]), f32(ssd_conv_b[l]), COLBLK_XBC, SSD_XBC, seq_len)
        mqk_act = _conv_silu(proj, f32(ml_conv_w[l]), f32(ml_conv_b[l]), COLBLK_MQK, 2 * ML_WIDTH, seq_len)
        y_f, y_b = _ssd_scan(xbc_act, gates_t, bias_cr[GATE_DT:GATE_DT + 2 * SSD_HEADS],
                             alog_cr[GATE_DT:GATE_DT + 2 * SSD_HEADS], batch)
        o_f, o_b = _hgrn2_scan(proj, lb_all[l], batch)
        m_f, m_b = _mlstm_scan(mqk_act, proj, gates_t, bias_cr[GATE_IG:GATE_IG + 4 * ML_HEADS], batch)
        dskip = jnp.repeat(f32(ssd_d[l]), SSD_HEAD_DIM).reshape(1, SSD_INNER)
        h = _outproj(y_f, y_b, xbc_act, proj, o_f, o_b, m_f, m_b, h, dskip,
                     f32(ssd_norm_w[l]).reshape(1, -1), f32(hg_norm_w[l]).reshape(1, -1),
                     f32(ml_norm_w[l]).reshape(1, -1), w_out[l].astype(BF16), f32(ln1_g[l]), f32(ln1_b[l]))
        if l % 2 == 0:
            j = l // 2
            h = _dense_ffn(h, ffn_w1[j].astype(BF16), ffn_w3[j].astype(BF16), ffn_w2[j].astype(BF16),
                           f32(ln2_g[l]), f32(ln2_b[l]))
        else:
            j = l // 2
            h = _moe_ffn(h, moe_router[j], f32(moe_w1[j]), f32(moe_w3[j]), f32(moe_w2[j]),
                         f32(ln2_g[l]), f32(ln2_b[l]))
    return h.reshape(batch, seq_len, D).astype(x.dtype)
```

```python
import functools

import jax
import jax.numpy as jnp
from jax import lax
from jax.experimental import pallas as pl
from jax.experimental.pallas import tpu as pltpu
from jax.experimental.pallas import tpu_sc as plsc

F32 = jnp.float32
BF16 = jnp.bfloat16
HIGHEST = lax.Precision.HIGHEST

D_MODEL = 1024
DEPTH = 4
D_MIX = 2 * D_MODEL
SSD_HEADS = 16
SSD_INNER = 1024
SSD_HEAD_DIM = 64
SSD_GROUPS = 2
SSD_STATE = 128
SSD_XBC = 1536
SSD_CHUNK = 128
HG_HEADS = 4
HG_WIDTH = 512
HG_DK = 128
ML_HEADS = 4
ML_WIDTH = 512
ML_DH = 128
CONV_WIDTH = 5
D_FF_DENSE = 2816
N_EXPERTS = 8
TOP_K = 2
D_FF_EXPERT = 3584
LN_EPS = 1e-5
RMS_EPS = 1e-6
MASK_NEG = -1e4
ALPHA = (2.0 * DEPTH) ** 0.25

LANES = 128
SUBLANES = 8
VMEM_LIMIT = 56 * 1024 * 1024

D_PROJ_PAD = 7168
COLBLK_Z = 0
COLBLK_MQK = 1
COLBLK_XBC = 2
COLBLK_HF_FWD = 4
COLBLK_HF_BWD = 5
COLBLK_HQ = 9
COLBLK_HI = 10
COLBLK_HG = 11
COLBLK_MV = 12
COLBLK_MO = 13
GATE_DT = 0
GATE_IG = 32
GATE_FG = 40

TM_LN = 512
TM_INPROJ = 256
TM_CONV = 512
SCAN_STEP = 128
HG_SUB = 64
HG_REFBLK = 4
TM_OUT = 256
TM_FFN = 512
TF_FFN = 256
TM_ROUTER = 512
MOE_BM = 1024
MOE_TF = 512
SC_CORES = 2
SC_WORKERS = 32
SC_GATHER_CHUNK = 32
TM_COMBINE = 256


def _cparams(sem):
    return pltpu.CompilerParams(dimension_semantics=sem, vmem_limit_bytes=VMEM_LIMIT)


def _iota(shape, dim):
    return lax.broadcasted_iota(jnp.int32, shape, dim)


def _sigmoid(x):
    return 1.0 / (1.0 + jnp.exp(-x))


def _silu(x):
    return x * _sigmoid(x)


def _softplus(x):
    return jnp.maximum(x, 0.0) + jnp.log1p(jnp.exp(-jnp.abs(x)))


def _log_sigmoid(x):
    return -_softplus(-x)


def _layer_norm(x, g, b):
    mu = jnp.mean(x, -1, keepdims=True)
    xc = x - mu
    var = jnp.mean(xc * xc, -1, keepdims=True)
    return xc * lax.rsqrt(var + LN_EPS) * g + b


def _causal(n, reverse):
    r = _iota((n, n), 0)
    c = _iota((n, n), 1)
    return (c >= r) if reverse else (c <= r)


def _dot(a, b):
    return jnp.dot(a, b, preferred_element_type=F32)


def _dot_nt(a, b):
    return lax.dot_general(a, b, (((1,), (1,)), ((), ())), preferred_element_type=F32)


def _dot_tn(a, b):
    return lax.dot_general(a, b, (((0,), (0,)), ((), ())), preferred_element_type=F32)


def _dot_hi(a, b):
    return jnp.dot(a, b, precision=HIGHEST, preferred_element_type=F32)


def _ln_kernel(x_ref, g_ref, b_ref, o_ref):
    o_ref[...] = _layer_norm(x_ref[...], g_ref[...], b_ref[...])


def _input_ln(x, g, b):
    T, D = x.shape
    return pl.pallas_call(
        _ln_kernel,
        out_shape=jax.ShapeDtypeStruct((T, D), F32),
        grid=(T // TM_LN,),
        in_specs=[pl.BlockSpec((TM_LN, D), lambda i: (i, 0)),
                  pl.BlockSpec((1, D), lambda i: (0, 0)),
                  pl.BlockSpec((1, D), lambda i: (0, 0))],
        out_specs=pl.BlockSpec((TM_LN, D), lambda i: (i, 0)),
        compiler_params=_cparams(("parallel",)),
        name="input_ln",
    )(x, g.reshape(1, D), b.reshape(1, D))


def _inproj_kernel(h_ref, w_ref, wgt_ref, proj_ref, gt_ref):
    hb = h_ref[...].astype(BF16)
    proj_ref[...] = _dot(hb, w_ref[...])
    gt_ref[...] = _dot_nt(wgt_ref[...], hb)


def _inproj(h, w, wgt):
    T, D = h.shape
    tm = TM_INPROJ
    return pl.pallas_call(
        _inproj_kernel,
        out_shape=(jax.ShapeDtypeStruct((T, D_PROJ_PAD), F32),
                   jax.ShapeDtypeStruct((LANES, T), F32)),
        grid=(T // tm,),
        in_specs=[pl.BlockSpec((tm, D), lambda i: (i, 0)),
                  pl.BlockSpec((D, D_PROJ_PAD), lambda i: (0, 0), pipeline_mode=pl.Buffered(1)),
                  pl.BlockSpec((LANES, D), lambda i: (0, 0), pipeline_mode=pl.Buffered(1))],
        out_specs=(pl.BlockSpec((tm, D_PROJ_PAD), lambda i: (i, 0)),
                   pl.BlockSpec((LANES, tm), lambda i: (0, i))),
        compiler_params=_cparams(("parallel",)),
        name="inproj",
    )(h, w, wgt)


def _conv_kernel(prev_ref, x_ref, next_ref, w_ref, b_ref, o_ref, *, tm, tiles_per_seq):
    i = pl.program_id(0)
    pos = i % tiles_per_seq
    x = x_ref[...]
    prev = jnp.where(pos == 0, 0.0, prev_ref[...])
    nxt = jnp.where(pos == tiles_per_seq - 1, 0.0, next_ref[...])
    xe = jnp.concatenate([prev, x, nxt], axis=0)
    n = tm + 2 * SUBLANES
    acc = b_ref[...] + w_ref[2:3, :] * x
    for k in (0, 1, 3, 4):
        shifted = pltpu.roll(xe, (2 - k) % n, 0)
        acc = acc + w_ref[k:k + 1, :] * shifted[SUBLANES:SUBLANES + tm]
    o_ref[...] = _silu(acc)


def _conv_silu(proj, w, b, colblk, width, seq_len):
    T = proj.shape[0]
    tm = TM_CONV
    rb = tm // SUBLANES
    nrb = T // SUBLANES
    kern = functools.partial(_conv_kernel, tm=tm, tiles_per_seq=seq_len // tm)
    return pl.pallas_call(
        kern,
        out_shape=jax.ShapeDtypeStruct((T, width), F32),
        grid=(T // tm,),
        in_specs=[pl.BlockSpec((SUBLANES, width), lambda i: (jnp.maximum(i * rb - 1, 0), colblk)),
                  pl.BlockSpec((tm, width), lambda i: (i, colblk)),
                  pl.BlockSpec((SUBLANES, width), lambda i: (jnp.minimum((i + 1) * rb, nrb - 1), colblk)),
                  pl.BlockSpec((CONV_WIDTH, width), lambda i: (0, 0)),
                  pl.BlockSpec((1, width), lambda i: (0, 0))],
        out_specs=pl.BlockSpec((tm, width), lambda i: (i, 0)),
        compiler_params=_cparams(("parallel",)),
        name="conv_silu",
    )(proj, proj, proj, w, b.reshape(1, width))


def _cumsum_lanes(x, reverse):
    n = x.shape[1]
    r = _iota((3 * n, n), 0) % n
    c = _iota((3 * n, n), 1)
    tri3 = ((r >= c) if reverse else (r <= c)).astype(BF16)
    return _dot(_split3(x, axis=1), tri3)


def _cols_of(rows_list):
    used = sum(r.shape[0] for r in rows_list)
    pad = jnp.zeros((LANES - used, LANES), F32)
    return jnp.concatenate(list(rows_list) + [pad], axis=0).T


def _ssd_gates(gates_t, bias_cr, alog_cr, reverse):
    last = 0 if reverse else SSD_CHUNK - 1
    dt_r = _softplus(gates_t + bias_cr)
    acs_r = _cumsum_lanes(dt_r * (-jnp.exp(alog_cr)), reverse)
    a_last = acs_r[:, last:last + 1]
    w_r = dt_r * jnp.exp(a_last - acs_r)
    return dt_r, acs_r, w_r, jnp.exp(a_last)


def _ssd_direction(x, gate_rows, acs_cols, st_ref, d, y_ref):
    reverse = d == 1
    n = SSD_CHUNK
    tri = _causal(n, reverse)
    lane_lo = _iota((n, LANES), 1) < SSD_HEAD_DIM
    dt_r, acs_r, w_r, dec = gate_rows

    for g in range(SSD_GROUPS):
        b0 = SSD_INNER + g * SSD_STATE
        c0 = SSD_INNER + SSD_GROUPS * SSD_STATE + g * SSD_STATE
        bg = x[:, b0:b0 + SSD_STATE]
        cgb = x[:, c0:c0 + SSD_STATE].astype(BF16)
        cb = _dot_nt(cgb, bg.astype(BF16))
        bg_t = bg.T
        for pair in range(4):
            p0 = (g * 4 + pair) * LANES
            xp = x[:, p0:p0 + LANES]
            ypair = None
            for hh in range(2):
                h = g * 8 + pair * 2 + hh
                c = d * SSD_HEADS + h
                xm = jnp.where(lane_lo if hh == 0 else jnp.logical_not(lane_lo), xp, 0.0).astype(BF16)
                col = jnp.broadcast_to(acs_cols[:, c:c + 1], (n, n))
                decay = jnp.exp(jnp.where(tri, col - acs_r[h:h + 1, :], MASK_NEG))
                m = (cb * decay * dt_r[h:h + 1, :]).astype(BF16)
                st = st_ref[d, h]
                y = _dot(m, xm) + jnp.exp(col) * _dot(cgb, st.astype(BF16))
                bh = (bg_t * w_r[h:h + 1, :]).astype(BF16)
                st_ref[d, h] = st * dec[h:h + 1, :] + _dot(bh, xm)
                ypair = y if ypair is None else ypair + y
            y_ref[:, p0:p0 + LANES] = ypair


def _ssd_kernel(xf_ref, xb_ref, gtf_ref, gtb_ref, bcr_ref, acr_ref, yf_ref, yb_ref, st_ref):
    @pl.when(pl.program_id(1) == 0)
    def _():
        st_ref[...] = jnp.zeros_like(st_ref)

    gf = _ssd_gates(gtf_ref[...], bcr_ref[0:SSD_HEADS, :], acr_ref[0:SSD_HEADS, :], False)
    gb = _ssd_gates(gtb_ref[...], bcr_ref[SSD_HEADS:2 * SSD_HEADS, :], acr_ref[SSD_HEADS:2 * SSD_HEADS, :], True)
    acs_cols = _cols_of([gf[1], gb[1]])
    _ssd_direction(xf_ref[...], gf, acs_cols, st_ref, 0, yf_ref)
    _ssd_direction(xb_ref[...], gb, acs_cols, st_ref, 1, yb_ref)


def _ssd_scan(xbc_act, gates_t, bias_cr, alog_cr, batch):
    T = xbc_act.shape[0]
    n = T // batch // SSD_CHUNK
    fwd = lambda b, i: b * n + i
    bwd = lambda b, i: b * n + (n - 1 - i)
    const = lambda b, i: (0, 0)
    return pl.pallas_call(
        _ssd_kernel,
        out_shape=(jax.ShapeDtypeStruct((T, SSD_INNER), F32),
                   jax.ShapeDtypeStruct((T, SSD_INNER), F32)),
        grid=(batch, n),
        in_specs=[pl.BlockSpec((SSD_CHUNK, SSD_XBC), lambda b, i: (fwd(b, i), 0)),
                  pl.BlockSpec((SSD_CHUNK, SSD_XBC), lambda b, i: (bwd(b, i), 0)),
                  pl.BlockSpec((SSD_HEADS, SSD_CHUNK), lambda b, i: (0, fwd(b, i))),
                  pl.BlockSpec((SSD_HEADS, SSD_CHUNK), lambda b, i: (1, bwd(b, i))),
                  pl.BlockSpec((2 * SSD_HEADS, LANES), const),
                  pl.BlockSpec((2 * SSD_HEADS, LANES), const)],
        out_specs=(pl.BlockSpec((SSD_CHUNK, SSD_INNER), lambda b, i: (fwd(b, i), 0)),
                   pl.BlockSpec((SSD_CHUNK, SSD_INNER), lambda b, i: (bwd(b, i), 0))),
        scratch_shapes=[pltpu.VMEM((2, SSD_HEADS, SSD_STATE, LANES), F32)],
        compiler_params=_cparams(("arbitrary", "arbitrary")),
        name="ssd_scan",
    )(xbc_act, xbc_act, gates_t, gates_t, bias_cr, alog_cr)


def _split3(x, axis=0):
    hi = x.astype(BF16)
    r1 = x - hi.astype(F32)
    mid = r1.astype(BF16)
    lo = (r1 - mid.astype(F32)).astype(BF16)
    return jnp.concatenate([hi, mid, lo], axis=axis)


def _hgrn2_cumsum_lhs(reverse):
    n = HG_SUB
    r_i = _iota((2 * n, n), 0)
    c_i = _iota((2 * n, n), 1)
    t = jnp.where(r_i < n, r_i, r_i - n)
    ref = (t // HG_REFBLK) * HG_REFBLK + (HG_REFBLK - 1 if reverse else 0)
    upto = jnp.where(r_i < n, t, ref)
    m = ((c_i >= upto) if reverse else (c_i <= upto)).astype(BF16)
    return jnp.concatenate([m, m, m], axis=1)


def _hgrn2_intra(q, f, v, lb, qmask, reverse):
    n = HG_SUB
    nblk = n // HG_REFBLK
    slab = 2 * SUBLANES
    tri = _causal(n, reverse)
    last = 0 if reverse else n - 1
    rowblk = _iota((slab, HG_DK), 0) // HG_REFBLK

    qh = _silu(q)
    key = (1.0 - lb) / (1.0 + jnp.exp(f))
    g = jnp.log2(1.0 - key)
    gg = _dot(_hgrn2_cumsum_lhs(reverse), _split3(g))
    gc = gg[:n]
    gref = gg[n:]
    qg = (qh * jnp.exp2(gc - gref)).astype(BF16)
    qs = (qh * jnp.exp2(gc)).astype(BF16)
    g_last = gc[last:last + 1, :]
    hk = gc - jnp.log2(key)
    kd = jnp.exp2(g_last - hk).astype(BF16)
    dec = jnp.exp2(g_last)
    vb = v.astype(BF16)

    o_intra = []
    for h in range(HG_HEADS):
        cols = slice(h * HG_DK, (h + 1) * HG_DK)
        hk_h = hk[:, cols]
        gc_h = gc[:, cols]
        q_ext = jnp.tile(qg[:, cols], (1, nblk)) * qmask
        k_parts = []
        for j in range(nblk):
            ref_row = j * HG_REFBLK + (HG_REFBLK - 1 if reverse else 0)
            rj = gc_h[ref_row:ref_row + 1, :]
            edge = (j * HG_REFBLK) // slab
            slabs = range(edge, n // slab) if reverse else range(0, edge + 1)
            pieces = []
            for sidx in range(n // slab):
                rows = slice(sidx * slab, (sidx + 1) * slab)
                if sidx not in slabs:
                    pieces.append(jnp.zeros((slab, HG_DK), BF16))
                    continue
                arg = rj - hk_h[rows]
                if sidx == edge:
                    jb = j - edge * (slab // HG_REFBLK)
                    live = (rowblk >= jb) if reverse else (rowblk <= jb)
                    arg = jnp.where(live, arg, MASK_NEG)
                pieces.append(jnp.exp2(arg).astype(BF16))
            k_parts.append(jnp.concatenate(pieces, axis=0))
        att = _dot_nt(q_ext, jnp.concatenate(k_parts, axis=1))
        att = jnp.where(tri, att, 0.0).astype(BF16)
        o_intra.append(_dot(att, vb[:, cols]))
    return o_intra, qs, kd, dec, vb


def _hgrn2_state(intra, st_ref, d, o_ref, rows):
    o_intra, qs, kd, dec, vb = intra
    for h in range(HG_HEADS):
        cols = slice(h * HG_DK, (h + 1) * HG_DK)
        st = st_ref[d, h]
        o_ref[rows, cols] = o_intra[h] + _dot_nt(qs[:, cols], st.astype(BF16))
        st_ref[d, h] = st * dec[:, cols] + _dot_tn(vb[:, cols], kd[:, cols])


def _hgrn2_kernel(qf_ref, ff_ref, vf_ref, qb_ref, fb_ref, vb_ref, lb_ref, qmask_ref, of_ref, ob_ref, st_ref):
    @pl.when(pl.program_id(1) == 0)
    def _():
        st_ref[...] = jnp.zeros_like(st_ref)

    lb = lb_ref[...]
    qmask = qmask_ref[...]
    refs = ((qf_ref, ff_ref, vf_ref, of_ref), (qb_ref, fb_ref, vb_ref, ob_ref))
    nsub = SCAN_STEP // HG_SUB
    units = []
    for k in range(nsub):
        units.append((0, k))
        units.append((1, nsub - 1 - k))
    pending = None
    for d, sc in units:
        rows = slice(sc * HG_SUB, (sc + 1) * HG_SUB)
        q_ref, f_ref, v_ref, o_ref = refs[d]
        intra = _hgrn2_intra(q_ref[rows, :], f_ref[rows, :], v_ref[rows, :], lb, qmask, d == 1)
        if pending is not None:
            _hgrn2_state(*pending)
        pending = (intra, st_ref, d, o_ref, rows)
    _hgrn2_state(*pending)


def _hgrn2_scan(proj, lb, batch):
    T = proj.shape[0]
    n = T // batch // SCAN_STEP
    fwd = lambda b, i: b * n + i
    bwd = lambda b, i: b * n + (n - 1 - i)
    blk = (SCAN_STEP, HG_WIDTH)
    nblk = HG_SUB // HG_REFBLK
    qmask = (jnp.arange(HG_SUB)[:, None] // HG_REFBLK == jnp.arange(nblk * HG_DK)[None, :] // HG_DK).astype(BF16)
    return pl.pallas_call(
        _hgrn2_kernel,
        out_shape=(jax.ShapeDtypeStruct((T, HG_WIDTH), F32),
                   jax.ShapeDtypeStruct((T, HG_WIDTH), F32)),
        grid=(batch, n),
        in_specs=[pl.BlockSpec(blk, lambda b, i: (fwd(b, i), COLBLK_HQ)),
                  pl.BlockSpec(blk, lambda b, i: (fwd(b, i), COLBLK_HF_FWD)),
                  pl.BlockSpec(blk, lambda b, i: (fwd(b, i), COLBLK_HI)),
                  pl.BlockSpec(blk, lambda b, i: (bwd(b, i), COLBLK_HQ)),
                  pl.BlockSpec(blk, lambda b, i: (bwd(b, i), COLBLK_HF_BWD)),
                  pl.BlockSpec(blk, lambda b, i: (bwd(b, i), COLBLK_HI)),
                  pl.BlockSpec((1, HG_WIDTH), lambda b, i: (0, 0)),
                  pl.BlockSpec((HG_SUB, nblk * HG_DK), lambda b, i: (0, 0))],
        out_specs=(pl.BlockSpec(blk, lambda b, i: (fwd(b, i), 0)),
                   pl.BlockSpec(blk, lambda b, i: (bwd(b, i), 0))),
        scratch_shapes=[pltpu.VMEM((2, HG_HEADS, HG_DK, HG_DK), F32)],
        compiler_params=_cparams(("arbitrary", "arbitrary")),
        name="hgrn2_scan",
    )(proj, proj, proj, proj, proj, proj, lb.reshape(1, HG_WIDTH), qmask)


def _mlstm_kernel(batch, qkf_ref, vf_ref, qkb_ref, vb_ref, *rest):
    gt_refs = (rest[:batch], rest[batch:2 * batch])
    bcr_ref, of_ref, ob_ref, c_ref, m_ref = rest[2 * batch:]

    @pl.when(pl.program_id(0) == 0)
    def _():
        c_ref[...] = jnp.zeros_like(c_ref)
        m_ref[...] = jnp.zeros_like(m_ref)

    n = SCAN_STEP
    ones = jnp.ones((n, ML_DH), BF16)
    io_refs = ((qkf_ref, vf_ref, of_ref), (qkb_ref, vb_ref, ob_ref))
    tris = (_causal(n, False), _causal(n, True))
    eye = _iota((n, n), 0) == _iota((n, n), 1)
    lane = _iota((SUBLANES, n), 1)

    gates = {}
    for b in range(batch):
        for d in range(2):
            reverse = d == 1
            last = 0 if reverse else n - 1
            g_r = gt_refs[d][b][...] + bcr_ref[...]
            bc = _cumsum_lanes(_log_sigmoid(g_r), reverse)[SUBLANES:]
            ig = g_r[:SUBLANES]
            u = ig - bc
            pm = u
            for sh in (1, 2, 4, 8, 16, 32, 64):
                shifted = pltpu.roll(pm, (n - sh) if reverse else sh, 1)
                valid = (lane < n - sh) if reverse else (lane >= sh)
                pm = jnp.maximum(pm, jnp.where(valid, shifted, -jnp.inf))
            m_prev = m_ref[b * 2 + d]
            floor = jnp.where((lane > 0) if reverse else (lane < n - 1), MASK_NEG, -jnp.inf)
            m_t = jnp.maximum(jnp.maximum(bc + pm, bc + m_prev), floor)
            b_last = bc[:, last:last + 1]
            m_new = jnp.maximum(b_last + m_prev[:, 0:1], b_last + pm[:, last:last + 1])
            w_s = jnp.exp(b_last - bc + ig - m_new)
            dec = jnp.exp(b_last + m_prev[:, 0:1] - m_new)
            m_ref[b * 2 + d] = jnp.broadcast_to(m_new, (SUBLANES, LANES))
            gates[b, d] = (u, _cols_of([bc, m_t]), m_prev, w_s, dec)

    units = [(b, d, h) for h in range(ML_HEADS) for b in range(batch) for d in range(2)]
    for b, d, h in units:
        u, cols, m_prev, w_s, dec = gates[b, d]
        qk_ref, v_ref, o_ref = io_refs[d]
        r = d * ML_HEADS + h
        q = (qk_ref[b, :, h * ML_DH:(h + 1) * ML_DH] * (ML_DH ** -0.5)).astype(BF16)
        k = qk_ref[b, :, ML_WIDTH + h * ML_DH:ML_WIDTH + (h + 1) * ML_DH].astype(BF16)
        ve = jnp.concatenate([v_ref[b, :, h * ML_DH:(h + 1) * ML_DH].astype(BF16), ones], axis=1)
        b_minus_m = jnp.broadcast_to(cols[:, r:r + 1] - cols[:, SUBLANES + r:SUBLANES + r + 1], (n, n))
        m_t = jnp.broadcast_to(cols[:, SUBLANES + r:SUBLANES + r + 1], (n, n))
        w_intra = jnp.exp(jnp.where(tris[d], b_minus_m + u[r:r + 1, :], MASK_NEG))
        w_inter = jnp.exp(b_minus_m + m_prev[r:r + 1, 0:1])
        s = (_dot_nt(q, k) * w_intra).astype(BF16)
        cst = c_ref[b, d, h]
        numden = _dot(s, ve) + jnp.concatenate([w_inter, w_inter], axis=1) * _dot(q, cst.astype(BF16))
        den = numden[:, ML_DH:]
        o_ref[b, :, h * ML_DH:(h + 1) * ML_DH] = numden[:, :ML_DH] / jnp.maximum(jnp.abs(den), jnp.exp(-m_t))
        diag = jnp.where(eye, jnp.broadcast_to(w_s[r:r + 1, :], (n, n)), 0.0).astype(BF16)
        c_ref[b, d, h] = dec[r:r + 1, :] * cst + _dot_tn(k, _dot(diag, ve).astype(BF16))


def _mlstm_scan(mqk_act, proj, gates_t, bias_cr, batch):
    T = proj.shape[0]
    seq = T // batch
    n = seq // SCAN_STEP
    gt_rowblk = GATE_IG // (2 * SUBLANES)
    fwd3 = lambda cb: (lambda i: (0, i, cb))
    bwd3 = lambda cb: (lambda i: (0, n - 1 - i, cb))
    gt_fwd = [pl.BlockSpec((2 * SUBLANES, SCAN_STEP), functools.partial(lambda b, i: (gt_rowblk, b * n + i), b))
              for b in range(batch)]
    gt_bwd = [pl.BlockSpec((2 * SUBLANES, SCAN_STEP),
                           functools.partial(lambda b, i: (gt_rowblk, b * n + n - 1 - i), b))
              for b in range(batch)]
    qk3 = mqk_act.reshape(batch, seq, 2 * ML_WIDTH)
    proj3 = proj.reshape(batch, seq, D_PROJ_PAD)
    o_f, o_b = pl.pallas_call(
        functools.partial(_mlstm_kernel, batch),
        out_shape=(jax.ShapeDtypeStruct((batch, seq, ML_WIDTH), F32),
                   jax.ShapeDtypeStruct((batch, seq, ML_WIDTH), F32)),
        grid=(n,),
        in_specs=[pl.BlockSpec((batch, SCAN_STEP, 2 * ML_WIDTH), fwd3(0)),
                  pl.BlockSpec((batch, SCAN_STEP, ML_WIDTH), fwd3(COLBLK_MV)),
                  pl.BlockSpec((batch, SCAN_STEP, 2 * ML_WIDTH), bwd3(0)),
                  pl.BlockSpec((batch, SCAN_STEP, ML_WIDTH), bwd3(COLBLK_MV))]
                 + gt_fwd + gt_bwd
                 + [pl.BlockSpec((2 * SUBLANES, LANES), lambda i: (0, 0))],
        out_specs=(pl.BlockSpec((batch, SCAN_STEP, ML_WIDTH), fwd3(0)),
                   pl.BlockSpec((batch, SCAN_STEP, ML_WIDTH), bwd3(0))),
        scratch_shapes=[pltpu.VMEM((batch, 2, ML_HEADS, ML_DH, 2 * ML_DH), F32),
                        pltpu.VMEM((batch * 2, SUBLANES, LANES), F32)],
        compiler_params=_cparams(("arbitrary",)),
        name="mlstm_scan",
    )(qk3, proj3, qk3, proj3, *([gates_t] * (2 * batch)), bias_cr)
    return o_f.reshape(T, ML_WIDTH), o_b.reshape(T, ML_WIDTH)


def _outproj_kernel(yf_ref, yb_ref, xs_ref, z_ref, of_ref, ob_ref, hg_ref, mf_ref, mb_ref, mo_ref,
                    h_ref, dskip_ref, ssdw_ref, hgw_ref, mlw_ref, wout_ref, g_ref, b_ref, o_ref):
    y = (yf_ref[...] + yb_ref[...] + xs_ref[...] * dskip_ref[...]) * _silu(z_ref[...])
    gw = SSD_INNER // SSD_GROUPS
    acc = None
    for g in range(SSD_GROUPS):
        seg = y[:, g * gw:(g + 1) * gw]
        seg = seg * lax.rsqrt(jnp.mean(seg * seg, -1, keepdims=True) + RMS_EPS) * ssdw_ref[:, g * gw:(g + 1) * gw]
        part = _dot(seg.astype(BF16), wout_ref[g * gw:(g + 1) * gw, :])
        acc = part if acc is None else acc + part
    o = of_ref[...] + ob_ref[...]
    hgate = _silu(hg_ref[...])
    m = mf_ref[...] + mb_ref[...]
    mgate = _sigmoid(mo_ref[...])
    for h in range(HG_HEADS):
        cols = slice(h * HG_DK, (h + 1) * HG_DK)
        seg = o[:, cols]
        seg = seg * lax.rsqrt(jnp.mean(seg * seg, -1, keepdims=True) + RMS_EPS) * hgw_ref[:, cols] * hgate[:, cols]
        r0 = SSD_INNER + h * HG_DK
        acc = acc + _dot(seg.astype(BF16), wout_ref[r0:r0 + HG_DK, :])
    for h in range(ML_HEADS):
        cols = slice(h * ML_DH, (h + 1) * ML_DH)
        seg = m[:, cols]
        seg = seg - jnp.mean(seg, -1, keepdims=True)
        seg = seg * lax.rsqrt(jnp.mean(seg * seg, -1, keepdims=True) + LN_EPS) * mlw_ref[:, cols] * mgate[:, cols]
        r0 = SSD_INNER + HG_WIDTH + h * ML_DH
        acc = acc + _dot(seg.astype(BF16), wout_ref[r0:r0 + ML_DH, :])
    o_ref[...] = _layer_norm(ALPHA * h_ref[...] + acc, g_ref[...], b_ref[...])


def _outproj(y_f, y_b, xbc_act, proj, o_f, o_b, m_f, m_b, h, dskip, ssd_w, hg_w, ml_w, w_out, g, b):
    T, D = h.shape
    tm = TM_OUT
    row = lambda width, cb: pl.BlockSpec((tm, width), lambda i: (i, cb))
    par = lambda width: pl.BlockSpec((1, width), lambda i: (0, 0))
    return pl.pallas_call(
        _outproj_kernel,
        out_shape=jax.ShapeDtypeStruct((T, D), F32),
        grid=(T // tm,),
        in_specs=[row(SSD_INNER, 0), row(SSD_INNER, 0), row(SSD_INNER, 0), row(SSD_INNER, COLBLK_Z),
                  row(HG_WIDTH, 0), row(HG_WIDTH, 0), row(HG_WIDTH, COLBLK_HG),
                  row(ML_WIDTH, 0), row(ML_WIDTH, 0), row(ML_WIDTH, COLBLK_MO),
                  row(D, 0), par(SSD_INNER), par(SSD_INNER), par(HG_WIDTH), par(ML_WIDTH),
                  pl.BlockSpec((D_MIX, D), lambda i: (0, 0), pipeline_mode=pl.Buffered(1)),
                  par(D), par(D)],
        out_specs=pl.BlockSpec((tm, D), lambda i: (i, 0)),
        compiler_params=_cparams(("parallel",)),
        name="outproj_ln1",
    )(y_f, y_b, xbc_act, proj, o_f, o_b, proj, m_f, m_b, proj, h, dskip, ssd_w, hg_w, ml_w, w_out,
      g.reshape(1, D), b.reshape(1, D))


def _ffn_kernel(h_ref, w1_ref, w3_ref, w2_ref, g_ref, b_ref, o_ref, acc_ref):
    h = h_ref[...]
    hb = h.astype(BF16)
    for c in range(D_FF_DENSE // TF_FFN):
        cols = slice(c * TF_FFN, (c + 1) * TF_FFN)
        a = _dot(hb, w1_ref[:, cols])
        u = _dot(hb, w3_ref[:, cols])
        part = _dot((_silu(a) * u).astype(BF16), w2_ref[cols, :])
        if c == 0:
            acc_ref[...] = part
        else:
            acc_ref[...] += part
    o_ref[...] = _layer_norm(ALPHA * h + acc_ref[...], g_ref[...], b_ref[...])


def _dense_ffn(h, w1, w3, w2, g, b):
    T, D = h.shape
    tm = TM_FFN
    resident = lambda shape: pl.BlockSpec(shape, lambda i: (0, 0), pipeline_mode=pl.Buffered(1))
    return pl.pallas_call(
        _ffn_kernel,
        out_shape=jax.ShapeDtypeStruct((T, D), F32),
        grid=(T // tm,),
        in_specs=[pl.BlockSpec((tm, D), lambda i: (i, 0)),
                  resident((D, D_FF_DENSE)), resident((D, D_FF_DENSE)), resident((D_FF_DENSE, D)),
                  pl.BlockSpec((1, D), lambda i: (0, 0)), pl.BlockSpec((1, D), lambda i: (0, 0))],
        out_specs=pl.BlockSpec((tm, D), lambda i: (i, 0)),
        scratch_shapes=[pltpu.VMEM((tm, D), F32)],
        compiler_params=_cparams(("parallel",)),
        name="dense_ffn_ln2",
    )(h, w1, w3, w2, g.reshape(1, D), b.reshape(1, D))


def _router_kernel(h_ref, w_ref, idx_ref, gate_ref):
    logits = _dot_hi(h_ref[...], w_ref[...])
    lane = _iota(logits.shape, 1)
    lane_f = lane.astype(F32)
    x = jnp.where(lane < N_EXPERTS, logits, -jnp.inf)
    m1 = jnp.max(x, axis=1, keepdims=True)
    i1 = jnp.min(jnp.where(x == m1, lane_f, float(LANES)), axis=1, keepdims=True)
    x2 = jnp.where(lane_f == i1, -jnp.inf, x)
    m2 = jnp.max(x2, axis=1, keepdims=True)
    i2 = jnp.min(jnp.where(x2 == m2, lane_f, float(LANES)), axis=1, keepdims=True)
    e2 = jnp.exp(m2 - m1)
    g1 = 1.0 / (1.0 + e2)
    g2 = e2 / (1.0 + e2)
    idx_ref[...] = jnp.where(lane == 0, i1, jnp.where(lane == 1, i2, 0.0)).astype(jnp.int32)
    gate_ref[...] = jnp.where(lane == 0, g1, jnp.where(lane == 1, g2, 0.0))


def _router(h, w_pad):
    T, D = h.shape
    tm = TM_ROUTER
    return pl.pallas_call(
        _router_kernel,
        out_shape=(jax.ShapeDtypeStruct((T, LANES), jnp.int32),
                   jax.ShapeDtypeStruct((T, LANES), F32)),
        grid=(T // tm,),
        in_specs=[pl.BlockSpec((tm, D), lambda i: (i, 0)),
                  pl.BlockSpec((D, LANES), lambda i: (0, 0))],
        out_specs=(pl.BlockSpec((tm, LANES), lambda i: (i, 0)),
                   pl.BlockSpec((tm, LANES), lambda i: (i, 0))),
        compiler_params=_cparams(("parallel",)),
        name="moe_router",
    )(h, w_pad)


def _gather_rows(src, idx):
    n = idx.shape[0]
    D = src.shape[1]
    per_worker = n // SC_WORKERS
    n_chunks = per_worker // SC_GATHER_CHUNK
    assert n_chunks * SC_GATHER_CHUNK * SC_WORKERS == n and n_chunks % 2 == 0
    mesh = plsc.VectorSubcoreMesh(core_axis_name="c", subcore_axis_name="s")

    @functools.partial(
        pl.kernel, mesh=mesh,
        out_type=jax.ShapeDtypeStruct((n, D), src.dtype),
        scratch_types=[pltpu.VMEM((2, SC_GATHER_CHUNK), jnp.int32),
                       pltpu.VMEM((2, SC_GATHER_CHUNK, D), src.dtype),
                       pltpu.SemaphoreType.DMA((2,))],
        name="moe_row_gather",
    )
    def gather(src_hbm, idx_hbm, out_hbm, idx_v, rows_v, sems):
        worker = lax.axis_index("s") * SC_CORES + lax.axis_index("c")
        base = worker * per_worker

        def gather_copy(slot):
            return pltpu.make_async_copy(src_hbm.at[idx_v.at[slot]], rows_v.at[slot], sems.at[slot])

        def start(j, slot):
            pltpu.sync_copy(idx_hbm.at[pl.ds(base + j * SC_GATHER_CHUNK, SC_GATHER_CHUNK)], idx_v.at[slot])
            gather_copy(slot).start()

        def finish(j, slot):
            gather_copy(slot).wait()
            pltpu.sync_copy(rows_v.at[slot], out_hbm.at[pl.ds(base + j * SC_GATHER_CHUNK, SC_GATHER_CHUNK)])

        start(0, 0)

        @pl.loop(0, n_chunks, step=2)
        def _(j):
            start(j + 1, 1)
            finish(j, 0)

            @pl.when(j + 2 < n_chunks)
            def _():
                start(j + 2, 0)

            finish(j + 1, 1)

    return gather(src, idx)


def _expert_kernel(be_ref, nact_ref, x_ref, w1_ref, w3_ref, w2_ref, o_ref):
    i = pl.program_id(0)
    f = pl.program_id(1)

    @pl.when(f == 0)
    def _():
        o_ref[...] = jnp.zeros_like(o_ref)

    @pl.when(i < nact_ref[0])
    def _():
        xb = x_ref[...].astype(BF16)
        a = _dot(xb, w1_ref[...].astype(BF16))
        u = _dot(xb, w3_ref[...].astype(BF16))
        o_ref[...] += _dot((_silu(a) * u).astype(BF16), w2_ref[...].astype(BF16))


def _expert_ffn(xs, block_expert, n_active, w1, w3, w2, layer):
    n_rows, D = xs.shape
    n_blocks = n_rows // MOE_BM
    n_ff = D_FF_EXPERT // MOE_TF

    def widx(i, f, be, nact):
        live = i < nact[0]
        return be[jnp.minimum(i, nact[0] - 1)], jnp.where(live, f, n_ff - 1)

    def w13_map(i, f, be, nact):
        e, ff = widx(i, f, be, nact)
        return layer, e, 0, ff

    def w2_map(i, f, be, nact):
        e, ff = widx(i, f, be, nact)
        return layer, e, ff, 0

    grid_spec = pltpu.PrefetchScalarGridSpec(
        num_scalar_prefetch=2,
        grid=(n_blocks, n_ff),
        in_specs=[pl.BlockSpec((MOE_BM, D), lambda i, f, be, nact: (i, 0)),
                  pl.BlockSpec((None, None, D, MOE_TF), w13_map),
                  pl.BlockSpec((None, None, D, MOE_TF), w13_map),
                  pl.BlockSpec((None, None, MOE_TF, D), w2_map)],
        out_specs=pl.BlockSpec((MOE_BM, D), lambda i, f, be, nact: (i, 0)),
    )
    return pl.pallas_call(
        _expert_kernel,
        out_shape=jax.ShapeDtypeStruct((n_rows, D), F32),
        grid_spec=grid_spec,
        compiler_params=_cparams(("arbitrary", "arbitrary")),
        name="moe_experts",
    )(block_expert, n_active, xs, w1, w3, w2)


def _combine_kernel(y0_ref, y1_ref, gate_ref, h_ref, g_ref, b_ref, o_ref):
    gate = gate_ref[...]
    f = y0_ref[...] * gate[:, 0:1] + y1_ref[...] * gate[:, 1:2]
    o_ref[...] = _layer_norm(ALPHA * h_ref[...] + f, g_ref[...], b_ref[...])


def _combine(y_slots, gates, h, g, b):
    T, D = h.shape
    tm = TM_COMBINE
    return pl.pallas_call(
        _combine_kernel,
        out_shape=jax.ShapeDtypeStruct((T, D), F32),
        grid=(T // tm,),
        in_specs=[pl.BlockSpec((tm, D), lambda i: (i, 0)),
                  pl.BlockSpec((tm, D), lambda i: (i + T // tm, 0)),
                  pl.BlockSpec((tm, LANES), lambda i: (i, 0)),
                  pl.BlockSpec((tm, D), lambda i: (i, 0)),
                  pl.BlockSpec((1, D), lambda i: (0, 0)),
                  pl.BlockSpec((1, D), lambda i: (0, 0))],
        out_specs=pl.BlockSpec((tm, D), lambda i: (i, 0)),
        compiler_params=_cparams(("parallel",)),
        name="moe_combine_ln2",
    )(y_slots, y_slots, gates, h, g.reshape(1, D), b.reshape(1, D))


def _moe_ffn(h, router_w, w1, w3, w2, layer, g, b):
    T, D = h.shape
    n_pairs = T * TOP_K
    n_blocks = n_pairs // MOE_BM + N_EXPERTS
    n_rows = n_blocks * MOE_BM
    w_pad = jnp.pad(router_w.astype(F32), ((0, 0), (0, LANES - N_EXPERTS)))
    idx_pad, gates_pad = _router(h, w_pad)
    expert_ids = idx_pad[:, :TOP_K].reshape(-1)
    onehot = (expert_ids[:, None] == jnp.arange(N_EXPERTS, dtype=jnp.int32)[None, :]).astype(jnp.int32)
    running = jnp.cumsum(onehot, axis=0)
    counts = running[-1]
    rank = jnp.sum(onehot * running, axis=1) - 1
    padded = (counts + MOE_BM - 1) // MOE_BM * MOE_BM
    ends = jnp.cumsum(padded)
    pos = (ends - padded)[expert_ids] + rank
    token_ids = jnp.arange(n_pairs, dtype=jnp.int32) // TOP_K
    row_token = jnp.zeros((n_rows,), jnp.int32).at[pos].set(token_ids)
    block_start = jnp.arange(n_blocks, dtype=jnp.int32) * MOE_BM
    block_expert = jnp.minimum(jnp.searchsorted(ends, block_start, side='right'),
                               N_EXPERTS - 1).astype(jnp.int32)
    n_active = (ends[-1:] // MOE_BM).astype(jnp.int32)

    xs = _gather_rows(h, row_token)
    ys = _expert_ffn(xs, block_expert, n_active, w1, w3, w2, layer)
    slot_major = pos.astype(jnp.int32).reshape(T, TOP_K).T.reshape(-1)
    y_slots = _gather_rows(ys, slot_major)
    return _combine(y_slots, gates_pad, h, g, b)


def _relayout_w_in(w):
    sizes = (SSD_INNER, SSD_XBC, 2 * SSD_HEADS, HG_WIDTH, 2 * HG_WIDTH, HG_WIDTH, HG_WIDTH,
             2 * ML_WIDTH, ML_WIDTH, ML_WIDTH, 2 * ML_HEADS, 2 * ML_HEADS)
    offs = [0]
    for s in sizes:
        offs.append(offs[-1] + s)
    (z, xbc, dt, hq, hf, hi, hg, mqk, mv, mo, mig, mfg) = [w[:, offs[k]:offs[k + 1]] for k in range(len(sizes))]
    pad = jnp.zeros((w.shape[0], LANES - 2 * SSD_HEADS - 4 * ML_HEADS), w.dtype)
    wn = jnp.concatenate([z, mqk, hf, xbc, hq, hi, hg, mv, mo], axis=1).astype(BF16)
    wgt = jnp.concatenate([dt, mig, mfg, pad], axis=1).astype(BF16).T
    return wn, wgt


def _gate_rows(vals_by_offset):
    row = jnp.zeros((LANES,), F32)
    for off, v in vals_by_offset:
        row = lax.dynamic_update_slice(row, v.reshape(-1).astype(F32), (off,))
    return jnp.broadcast_to(row[:, None], (LANES, LANES))


def kernel(x, ln_in_g, ln_in_b, w_in, ssd_conv_w, ssd_conv_b, ssd_dt_bias, ssd_a_log, ssd_d, ssd_norm_w, hg_lb_logits, hg_norm_w, ml_conv_w, ml_conv_b, ml_ig_bias, ml_fg_bias, ml_norm_w, w_out, ln1_g, ln1_b, ln2_g, ln2_b, ffn_w1, ffn_w3, ffn_w2, moe_router, moe_w1, moe_w3, moe_w2):
    batch, seq_len, D = x.shape
    T = batch * seq_len
    f32 = lambda t: t.astype(F32)
    lb_soft = jax.nn.softmax(f32(hg_lb_logits), axis=0)
    lb_all = jnp.cumsum(lb_soft, axis=0) - lb_soft[0]

    h = _input_ln(f32(x).reshape(T, D), f32(ln_in_g), f32(ln_in_b))
    for l in range(DEPTH):
        w, wgt = _relayout_w_in(w_in[l])
        bias_cr = _gate_rows(((GATE_DT, ssd_dt_bias[l]), (GATE_IG, ml_ig_bias[l]), (GATE_FG, ml_fg_bias[l])))
        alog_cr = _gate_rows(((GATE_DT, ssd_a_log[l]),))

        proj, gates_t = _inproj(h, w, wgt)
        xbc_act = _conv_silu(proj, f32(ssd_conv_w[l]), f32(ssd_conv_b[l]), COLBLK_XBC, SSD_XBC, seq_len)
        mqk_act = _conv_silu(proj, f32(ml_conv_w[l]), f32(ml_conv_b[l]), COLBLK_MQK, 2 * ML_WIDTH, seq_len)
        y_f, y_b = _ssd_scan(xbc_act, gates_t, bias_cr[GATE_DT:GATE_DT + 2 * SSD_HEADS],
                             alog_cr[GATE_DT:GATE_DT + 2 * SSD_HEADS], batch)
        o_f, o_b = _hgrn2_scan(proj, lb_all[l], batch)
        m_f, m_b = _mlstm_scan(mqk_act, proj, gates_t, bias_cr[GATE_IG:GATE_IG + 4 * ML_HEADS], batch)
        dskip = jnp.repeat(f32(ssd_d[l]), SSD_HEAD_DIM).reshape(1, SSD_INNER)
        h = _outproj(y_f, y_b, xbc_act, proj, o_f, o_b, m_f, m_b, h, dskip,
                     f32(ssd_norm_w[l]).reshape(1, -1), f32(hg_norm_w[l]).reshape(1, -1),
                     f32(ml_norm_w[l]).reshape(1, -1), w_out[l].astype(BF16), f32(ln1_g[l]), f32(ln1_b[l]))
        if l % 2 == 0:
            j = l // 2
            h = _dense_ffn(h, ffn_w1[j].astype(BF16), ffn_w3[j].astype(BF16), ffn_w2[j].astype(BF16),
                           f32(ln2_g[l]), f32(ln2_b[l]))
        else:
            j = l // 2
            h = _moe_ffn(h, moe_router[j], f32(moe_w1), f32(moe_w3), f32(moe_w2), j,
                         f32(ln2_g[l]), f32(ln2_b[l]))
    return h.reshape(batch, seq_len, D).astype(x.dtype)
```

```python
import functools

import jax
import jax.numpy as jnp
from jax import lax
from jax.experimental import pallas as pl
from jax.experimental.pallas import tpu as pltpu
from jax.experimental.pallas import tpu_sc as plsc

F32 = jnp.float32
BF16 = jnp.bfloat16
HIGHEST = lax.Precision.HIGHEST

D_MODEL = 1024
DEPTH = 4
D_MIX = 2 * D_MODEL
SSD_HEADS = 16
SSD_INNER = 1024
SSD_HEAD_DIM = 64
SSD_GROUPS = 2
SSD_STATE = 128
SSD_XBC = 1536
SSD_CHUNK = 128
HG_HEADS = 4
HG_WIDTH = 512
HG_DK = 128
ML_HEADS = 4
ML_WIDTH = 512
ML_DH = 128
CONV_WIDTH = 5
D_FF_DENSE = 2816
N_EXPERTS = 8
TOP_K = 2
D_FF_EXPERT = 3584
LN_EPS = 1e-5
RMS_EPS = 1e-6
MASK_NEG = -1e4
ALPHA = (2.0 * DEPTH) ** 0.25
LOG2E = 1.4426950408889634

LANES = 128
SUBLANES = 8
VMEM_LIMIT = 56 * 1024 * 1024

D_PROJ_PAD = 7168
COLBLK_Z = 0
COLBLK_MQK = 1
COLBLK_XBC = 2
COLBLK_HF_FWD = 4
COLBLK_HF_BWD = 5
COLBLK_HQ = 9
COLBLK_HI = 10
COLBLK_HG = 11
COLBLK_MV = 12
COLBLK_MO = 13
GATE_DT = 0
GATE_IG = 32
GATE_FG = 40

TM_LN = 512
TM_INPROJ = 256
TM_CONV = 512
SCAN_STEP = 128
HG_SUB = 64
HG_REFBLK = 4
TM_OUT = 256
TM_FFN = 512
TF_FFN = 256
TM_ROUTER = 512
MOE_BM = 1024
MOE_TF = 512
SC_CORES = 2
SC_WORKERS = 32
SC_GATHER_CHUNK = 32
TM_COMBINE = 256


def _cparams(sem):
    return pltpu.CompilerParams(dimension_semantics=sem, vmem_limit_bytes=VMEM_LIMIT)


def _iota(shape, dim):
    return lax.broadcasted_iota(jnp.int32, shape, dim)


def _sigmoid(x):
    return 1.0 / (1.0 + jnp.exp(-x))


def _silu(x):
    return x * _sigmoid(x)


def _softplus(x):
    return jnp.maximum(x, 0.0) + jnp.log1p(jnp.exp(-jnp.abs(x)))


def _log_sigmoid(x):
    return -_softplus(-x)


def _layer_norm(x, g, b):
    mu = jnp.mean(x, -1, keepdims=True)
    xc = x - mu
    var = jnp.mean(xc * xc, -1, keepdims=True)
    return xc * lax.rsqrt(var + LN_EPS) * g + b


def _causal(n, reverse):
    r = _iota((n, n), 0)
    c = _iota((n, n), 1)
    return (c >= r) if reverse else (c <= r)


def _dot(a, b):
    return jnp.dot(a, b, preferred_element_type=F32)


def _dot_nt(a, b):
    return lax.dot_general(a, b, (((1,), (1,)), ((), ())), preferred_element_type=F32)


def _dot_tn(a, b):
    return lax.dot_general(a, b, (((0,), (0,)), ((), ())), preferred_element_type=F32)


def _dot_hi(a, b):
    return jnp.dot(a, b, precision=HIGHEST, preferred_element_type=F32)


def _ln_kernel(x_ref, g_ref, b_ref, o_ref):
    o_ref[...] = _layer_norm(x_ref[...], g_ref[...], b_ref[...])


def _input_ln(x, g, b):
    T, D = x.shape
    return pl.pallas_call(
        _ln_kernel,
        out_shape=jax.ShapeDtypeStruct((T, D), F32),
        grid=(T // TM_LN,),
        in_specs=[pl.BlockSpec((TM_LN, D), lambda i: (i, 0)),
                  pl.BlockSpec((1, D), lambda i: (0, 0)),
                  pl.BlockSpec((1, D), lambda i: (0, 0))],
        out_specs=pl.BlockSpec((TM_LN, D), lambda i: (i, 0)),
        compiler_params=_cparams(("parallel",)),
        name="input_ln",
    )(x, g.reshape(1, D), b.reshape(1, D))


def _inproj_kernel(h_ref, w_ref, wgt_ref, proj_ref, gt_ref):
    hb = h_ref[...].astype(BF16)
    proj_ref[...] = _dot(hb, w_ref[...])
    gt_ref[...] = _dot_nt(wgt_ref[...], hb)


def _inproj(h, w, wgt):
    T, D = h.shape
    tm = TM_INPROJ
    return pl.pallas_call(
        _inproj_kernel,
        out_shape=(jax.ShapeDtypeStruct((T, D_PROJ_PAD), F32),
                   jax.ShapeDtypeStruct((LANES, T), F32)),
        grid=(T // tm,),
        in_specs=[pl.BlockSpec((tm, D), lambda i: (i, 0)),
                  pl.BlockSpec((D, D_PROJ_PAD), lambda i: (0, 0), pipeline_mode=pl.Buffered(1)),
                  pl.BlockSpec((LANES, D), lambda i: (0, 0), pipeline_mode=pl.Buffered(1))],
        out_specs=(pl.BlockSpec((tm, D_PROJ_PAD), lambda i: (i, 0)),
                   pl.BlockSpec((LANES, tm), lambda i: (0, i))),
        compiler_params=_cparams(("parallel",)),
        name="inproj",
    )(h, w, wgt)


def _conv_kernel(prev_ref, x_ref, next_ref, w_ref, b_ref, o_ref, *, tm, tiles_per_seq):
    i = pl.program_id(0)
    pos = i % tiles_per_seq
    x = x_ref[...]
    prev = jnp.where(pos == 0, 0.0, prev_ref[...])
    nxt = jnp.where(pos == tiles_per_seq - 1, 0.0, next_ref[...])
    xe = jnp.concatenate([prev, x, nxt], axis=0)
    n = tm + 2 * SUBLANES
    acc = b_ref[...] + w_ref[2:3, :] * x
    for k in (0, 1, 3, 4):
        shifted = pltpu.roll(xe, (2 - k) % n, 0)
        acc = acc + w_ref[k:k + 1, :] * shifted[SUBLANES:SUBLANES + tm]
    o_ref[...] = _silu(acc)


def _conv_silu(proj, w, b, colblk, width, seq_len):
    T = proj.shape[0]
    tm = TM_CONV
    rb = tm // SUBLANES
    nrb = T // SUBLANES
    kern = functools.partial(_conv_kernel, tm=tm, tiles_per_seq=seq_len // tm)
    return pl.pallas_call(
        kern,
        out_shape=jax.ShapeDtypeStruct((T, width), F32),
        grid=(T // tm,),
        in_specs=[pl.BlockSpec((SUBLANES, width), lambda i: (jnp.maximum(i * rb - 1, 0), colblk)),
                  pl.BlockSpec((tm, width), lambda i: (i, colblk)),
                  pl.BlockSpec((SUBLANES, width), lambda i: (jnp.minimum((i + 1) * rb, nrb - 1), colblk)),
                  pl.BlockSpec((CONV_WIDTH, width), lambda i: (0, 0)),
                  pl.BlockSpec((1, width), lambda i: (0, 0))],
        out_specs=pl.BlockSpec((tm, width), lambda i: (i, 0)),
        compiler_params=_cparams(("parallel",)),
        name="conv_silu",
    )(proj, proj, proj, w, b.reshape(1, width))


def _cumsum_lanes(x, reverse):
    n = x.shape[1]
    r = _iota((3 * n, n), 0) % n
    c = _iota((3 * n, n), 1)
    tri3 = ((r >= c) if reverse else (r <= c)).astype(BF16)
    return _dot(_split3(x, axis=1), tri3)


def _cols_of(rows_list):
    used = sum(r.shape[0] for r in rows_list)
    pad = jnp.zeros((LANES - used, LANES), F32)
    return jnp.concatenate(list(rows_list) + [pad], axis=0).T


def _ssd_gates(gates_t, bias_cr, alog_cr, reverse):
    last = 0 if reverse else SSD_CHUNK - 1
    dt_r = _softplus(gates_t + bias_cr)
    acs_r = _cumsum_lanes(dt_r * (-jnp.exp(alog_cr)) * LOG2E, reverse)
    a_last = acs_r[:, last:last + 1]
    w_r = dt_r * jnp.exp2(a_last - acs_r)
    return acs_r - jnp.log2(dt_r), acs_r, w_r, jnp.exp2(a_last)


def _ssd_direction(x, gate_rows, acs_cols, st_ref, d, y_ref):
    reverse = d == 1
    n = SSD_CHUNK
    tri = _causal(n, reverse)
    lane_lo = _iota((n, LANES), 1) < SSD_HEAD_DIM
    acs_dt_r, _, w_r, dec = gate_rows

    for g in range(SSD_GROUPS):
        b0 = SSD_INNER + g * SSD_STATE
        c0 = SSD_INNER + SSD_GROUPS * SSD_STATE + g * SSD_STATE
        bg = x[:, b0:b0 + SSD_STATE]
        cgb = x[:, c0:c0 + SSD_STATE].astype(BF16)
        cb = _dot_nt(cgb, bg.astype(BF16))
        bg_t = bg.T
        for pair in range(4):
            p0 = (g * 4 + pair) * LANES
            xp = x[:, p0:p0 + LANES]
            ypair = None
            for hh in range(2):
                h = g * 8 + pair * 2 + hh
                c = d * SSD_HEADS + h
                xm = jnp.where(lane_lo if hh == 0 else jnp.logical_not(lane_lo), xp, 0.0).astype(BF16)
                col = jnp.broadcast_to(acs_cols[:, c:c + 1], (n, n))
                decay_dt = jnp.exp2(jnp.where(tri, col - acs_dt_r[h:h + 1, :], MASK_NEG))
                m = (cb * decay_dt).astype(BF16)
                st = st_ref[d, h]
                y = _dot(m, xm) + jnp.exp2(col) * _dot(cgb, st.astype(BF16))
                bh = (bg_t * w_r[h:h + 1, :]).astype(BF16)
                st_ref[d, h] = st * dec[h:h + 1, :] + _dot(bh, xm)
                ypair = y if ypair is None else ypair + y
            y_ref[:, p0:p0 + LANES] = ypair


def _ssd_kernel(xf_ref, xb_ref, gtf_ref, gtb_ref, bcr_ref, acr_ref, yf_ref, yb_ref, st_ref):
    @pl.when(pl.program_id(1) == 0)
    def _():
        st_ref[...] = jnp.zeros_like(st_ref)

    gf = _ssd_gates(gtf_ref[...], bcr_ref[0:SSD_HEADS, :], acr_ref[0:SSD_HEADS, :], False)
    gb = _ssd_gates(gtb_ref[...], bcr_ref[SSD_HEADS:2 * SSD_HEADS, :], acr_ref[SSD_HEADS:2 * SSD_HEADS, :], True)
    acs_cols = _cols_of([gf[1], gb[1]])
    _ssd_direction(xf_ref[...], gf, acs_cols, st_ref, 0, yf_ref)
    _ssd_direction(xb_ref[...], gb, acs_cols, st_ref, 1, yb_ref)


def _ssd_scan(xbc_act, gates_t, bias_cr, alog_cr, batch):
    T = xbc_act.shape[0]
    n = T // batch // SSD_CHUNK
    fwd = lambda b, i: b * n + i
    bwd = lambda b, i: b * n + (n - 1 - i)
    const = lambda b, i: (0, 0)
    return pl.pallas_call(
        _ssd_kernel,
        out_shape=(jax.ShapeDtypeStruct((T, SSD_INNER), F32),
                   jax.ShapeDtypeStruct((T, SSD_INNER), F32)),
        grid=(batch, n),
        in_specs=[pl.BlockSpec((SSD_CHUNK, SSD_XBC), lambda b, i: (fwd(b, i), 0)),
                  pl.BlockSpec((SSD_CHUNK, SSD_XBC), lambda b, i: (bwd(b, i), 0)),
                  pl.BlockSpec((SSD_HEADS, SSD_CHUNK), lambda b, i: (0, fwd(b, i))),
                  pl.BlockSpec((SSD_HEADS, SSD_CHUNK), lambda b, i: (1, bwd(b, i))),
                  pl.BlockSpec((2 * SSD_HEADS, LANES), const),
                  pl.BlockSpec((2 * SSD_HEADS, LANES), const)],
        out_specs=(pl.BlockSpec((SSD_CHUNK, SSD_INNER), lambda b, i: (fwd(b, i), 0)),
                   pl.BlockSpec((SSD_CHUNK, SSD_INNER), lambda b, i: (bwd(b, i), 0))),
        scratch_shapes=[pltpu.VMEM((2, SSD_HEADS, SSD_STATE, LANES), F32)],
        compiler_params=_cparams(("arbitrary", "arbitrary")),
        name="ssd_scan",
    )(xbc_act, xbc_act, gates_t, gates_t, bias_cr, alog_cr)


def _split3(x, axis=0):
    hi = x.astype(BF16)
    r1 = x - hi.astype(F32)
    mid = r1.astype(BF16)
    lo = (r1 - mid.astype(F32)).astype(BF16)
    return jnp.concatenate([hi, mid, lo], axis=axis)


def _hgrn2_cumsum_lhs(reverse):
    n = HG_SUB
    r_i = _iota((2 * n, n), 0)
    c_i = _iota((2 * n, n), 1)
    t = jnp.where(r_i < n, r_i, r_i - n)
    ref = (t // HG_REFBLK) * HG_REFBLK + (HG_REFBLK - 1 if reverse else 0)
    upto = jnp.where(r_i < n, t, ref)
    m = ((c_i >= upto) if reverse else (c_i <= upto)).astype(BF16)
    return jnp.concatenate([m, m, m], axis=1)


def _hgrn2_intra(q, f, v, lb, qmask, reverse):
    n = HG_SUB
    nblk = n // HG_REFBLK
    slab = 2 * SUBLANES
    tri = _causal(n, reverse)
    last = 0 if reverse else n - 1
    rowblk = _iota((slab, HG_DK), 0) // HG_REFBLK

    qh = _silu(q)
    key = (1.0 - lb) / (1.0 + jnp.exp(f))
    g = jnp.log2(1.0 - key)
    gg = _dot(_hgrn2_cumsum_lhs(reverse), _split3(g))
    gc = gg[:n]
    gref = gg[n:]
    qg = (qh * jnp.exp2(gc - gref)).astype(BF16)
    qs = (qh * jnp.exp2(gc)).astype(BF16)
    g_last = gc[last:last + 1, :]
    hk = gc - jnp.log2(key)
    kd = jnp.exp2(g_last - hk).astype(BF16)
    dec = jnp.exp2(g_last)
    vb = v.astype(BF16)

    o_intra = []
    for h in range(HG_HEADS):
        cols = slice(h * HG_DK, (h + 1) * HG_DK)
        hk_h = hk[:, cols]
        gc_h = gc[:, cols]
        q_ext = jnp.tile(qg[:, cols], (1, nblk)) * qmask
        k_parts = []
        for j in range(nblk):
            ref_row = j * HG_REFBLK + (HG_REFBLK - 1 if reverse else 0)
            rj = gc_h[ref_row:ref_row + 1, :]
            edge = (j * HG_REFBLK) // slab
            slabs = range(edge, n // slab) if reverse else range(0, edge + 1)
            pieces = []
            for sidx in range(n // slab):
                rows = slice(sidx * slab, (sidx + 1) * slab)
                if sidx not in slabs:
                    pieces.append(jnp.zeros((slab, HG_DK), BF16))
                    continue
                arg = rj - hk_h[rows]
                if sidx == edge:
                    jb = j - edge * (slab // HG_REFBLK)
                    live = (rowblk >= jb) if reverse else (rowblk <= jb)
                    arg = jnp.where(live, arg, MASK_NEG)
                pieces.append(jnp.exp2(arg).astype(BF16))
            k_parts.append(jnp.concatenate(pieces, axis=0))
        att = _dot_nt(q_ext, jnp.concatenate(k_parts, axis=1))
        att = jnp.where(tri, att, 0.0).astype(BF16)
        o_intra.append(_dot(att, vb[:, cols]))
    return o_intra, qs, kd, dec, vb


def _hgrn2_state(intra, st_ref, d, o_ref, rows):
    o_intra, qs, kd, dec, vb = intra
    for h in range(HG_HEADS):
        cols = slice(h * HG_DK, (h + 1) * HG_DK)
        st = st_ref[d, h]
        o_ref[rows, cols] = o_intra[h] + _dot_nt(qs[:, cols], st.astype(BF16))
        st_ref[d, h] = st * dec[:, cols] + _dot_tn(vb[:, cols], kd[:, cols])


def _hgrn2_kernel(qf_ref, ff_ref, vf_ref, qb_ref, fb_ref, vb_ref, lb_ref, qmask_ref, of_ref, ob_ref, st_ref):
    @pl.when(pl.program_id(1) == 0)
    def _():
        st_ref[...] = jnp.zeros_like(st_ref)

    lb = lb_ref[...]
    qmask = qmask_ref[...]
    refs = ((qf_ref, ff_ref, vf_ref, of_ref), (qb_ref, fb_ref, vb_ref, ob_ref))
    nsub = SCAN_STEP // HG_SUB
    units = []
    for k in range(nsub):
        units.append((0, k))
        units.append((1, nsub - 1 - k))
    pending = None
    for d, sc in units:
        rows = slice(sc * HG_SUB, (sc + 1) * HG_SUB)
        q_ref, f_ref, v_ref, o_ref = refs[d]
        intra = _hgrn2_intra(q_ref[rows, :], f_ref[rows, :], v_ref[rows, :], lb, qmask, d == 1)
        if pending is not None:
            _hgrn2_state(*pending)
        pending = (intra, st_ref, d, o_ref, rows)
    _hgrn2_state(*pending)


def _hgrn2_scan(proj, lb, batch):
    T = proj.shape[0]
    n = T // batch // SCAN_STEP
    fwd = lambda b, i: b * n + i
    bwd = lambda b, i: b * n + (n - 1 - i)
    blk = (SCAN_STEP, HG_WIDTH)
    nblk = HG_SUB // HG_REFBLK
    qmask = (jnp.arange(HG_SUB)[:, None] // HG_REFBLK == jnp.arange(nblk * HG_DK)[None, :] // HG_DK).astype(BF16)
    return pl.pallas_call(
        _hgrn2_kernel,
        out_shape=(jax.ShapeDtypeStruct((T, HG_WIDTH), F32),
                   jax.ShapeDtypeStruct((T, HG_WIDTH), F32)),
        grid=(batch, n),
        in_specs=[pl.BlockSpec(blk, lambda b, i: (fwd(b, i), COLBLK_HQ)),
                  pl.BlockSpec(blk, lambda b, i: (fwd(b, i), COLBLK_HF_FWD)),
                  pl.BlockSpec(blk, lambda b, i: (fwd(b, i), COLBLK_HI)),
                  pl.BlockSpec(blk, lambda b, i: (bwd(b, i), COLBLK_HQ)),
                  pl.BlockSpec(blk, lambda b, i: (bwd(b, i), COLBLK_HF_BWD)),
                  pl.BlockSpec(blk, lambda b, i: (bwd(b, i), COLBLK_HI)),
                  pl.BlockSpec((1, HG_WIDTH), lambda b, i: (0, 0)),
                  pl.BlockSpec((HG_SUB, nblk * HG_DK), lambda b, i: (0, 0))],
        out_specs=(pl.BlockSpec(blk, lambda b, i: (fwd(b, i), 0)),
                   pl.BlockSpec(blk, lambda b, i: (bwd(b, i), 0))),
        scratch_shapes=[pltpu.VMEM((2, HG_HEADS, HG_DK, HG_DK), F32)],
        compiler_params=_cparams(("arbitrary", "arbitrary")),
        name="hgrn2_scan",
    )(proj, proj, proj, proj, proj, proj, lb.reshape(1, HG_WIDTH), qmask)


def _mlstm_kernel(batch, qkf_ref, vf_ref, qkb_ref, vb_ref, *rest):
    gt_refs = (rest[:batch], rest[batch:2 * batch])
    bcr_ref, of_ref, ob_ref, c_ref, m_ref = rest[2 * batch:]

    @pl.when(pl.program_id(0) == 0)
    def _():
        c_ref[...] = jnp.zeros_like(c_ref)
        m_ref[...] = jnp.zeros_like(m_ref)

    n = SCAN_STEP
    ones = jnp.ones((n, ML_DH), BF16)
    io_refs = ((qkf_ref, vf_ref, of_ref), (qkb_ref, vb_ref, ob_ref))
    tris = (_causal(n, False), _causal(n, True))
    lane = _iota((SUBLANES, n), 1)

    gates = {}
    for b in range(batch):
        for d in range(2):
            reverse = d == 1
            last = 0 if reverse else n - 1
            g_r = gt_refs[d][b][...] + bcr_ref[...]
            bc = _cumsum_lanes(_log_sigmoid(g_r) * LOG2E, reverse)[SUBLANES:]
            ig = g_r[:SUBLANES] * LOG2E
            u = ig - bc
            pm = u
            for sh in (1, 2, 4, 8, 16, 32, 64):
                shifted = pltpu.roll(pm, (n - sh) if reverse else sh, 1)
                valid = (lane < n - sh) if reverse else (lane >= sh)
                pm = jnp.maximum(pm, jnp.where(valid, shifted, -jnp.inf))
            m_prev = m_ref[b * 2 + d]
            floor = jnp.where((lane > 0) if reverse else (lane < n - 1), MASK_NEG * LOG2E, -jnp.inf)
            m_t = jnp.maximum(jnp.maximum(bc + pm, bc + m_prev), floor)
            b_last = bc[:, last:last + 1]
            m_new = jnp.maximum(b_last + m_prev[:, 0:1], b_last + pm[:, last:last + 1])
            w_s = jnp.exp2(b_last - bc + ig - m_new)
            dec = jnp.exp2(b_last + m_prev[:, 0:1] - m_new)
            m_ref[b * 2 + d] = jnp.broadcast_to(m_new, (SUBLANES, LANES))
            gates[b, d] = (u, _cols_of([bc, m_t]), m_prev, w_s, dec)

    units = [(b, d, h) for h in range(ML_HEADS) for b in range(batch) for d in range(2)]
    for b, d, h in units:
        u, cols, m_prev, w_s, dec = gates[b, d]
        qk_ref, v_ref, o_ref = io_refs[d]
        r = d * ML_HEADS + h
        q = (qk_ref[b, :, h * ML_DH:(h + 1) * ML_DH] * (ML_DH ** -0.5)).astype(BF16)
        k = qk_ref[b, :, ML_WIDTH + h * ML_DH:ML_WIDTH + (h + 1) * ML_DH]
        ve = jnp.concatenate([v_ref[b, :, h * ML_DH:(h + 1) * ML_DH].astype(BF16), ones], axis=1)
        b_minus_m = jnp.broadcast_to(cols[:, r:r + 1] - cols[:, SUBLANES + r:SUBLANES + r + 1], (n, n))
        m_t = jnp.broadcast_to(cols[:, SUBLANES + r:SUBLANES + r + 1], (n, n))
        w_intra = jnp.exp2(jnp.where(tris[d], b_minus_m + u[r:r + 1, :], MASK_NEG))
        w_inter = jnp.exp2(b_minus_m + m_prev[r:r + 1, 0:1])
        s = (_dot_nt(q, k.astype(BF16)) * w_intra).astype(BF16)
        cst = c_ref[b, d, h]
        numden = _dot(s, ve) + jnp.concatenate([w_inter, w_inter], axis=1) * _dot(q, cst.astype(BF16))
        den = numden[:, ML_DH:]
        o_ref[b, :, h * ML_DH:(h + 1) * ML_DH] = numden[:, :ML_DH] / jnp.maximum(jnp.abs(den), jnp.exp2(-m_t))
        kw_t = (k.T * w_s[r:r + 1, :]).astype(BF16)
        c_ref[b, d, h] = dec[r:r + 1, :] * cst + _dot(kw_t, ve)


def _mlstm_scan(mqk_act, proj, gates_t, bias_cr, batch):
    T = proj.shape[0]
    seq = T // batch
    n = seq // SCAN_STEP
    gt_rowblk = GATE_IG // (2 * SUBLANES)
    fwd3 = lambda cb: (lambda i: (0, i, cb))
    bwd3 = lambda cb: (lambda i: (0, n - 1 - i, cb))
    gt_fwd = [pl.BlockSpec((2 * SUBLANES, SCAN_STEP), functools.partial(lambda b, i: (gt_rowblk, b * n + i), b))
              for b in range(batch)]
    gt_bwd = [pl.BlockSpec((2 * SUBLANES, SCAN_STEP),
                           functools.partial(lambda b, i: (gt_rowblk, b * n + n - 1 - i), b))
              for b in range(batch)]
    qk3 = mqk_act.reshape(batch, seq, 2 * ML_WIDTH)
    proj3 = proj.reshape(batch, seq, D_PROJ_PAD)
    o_f, o_b = pl.pallas_call(
        functools.partial(_mlstm_kernel, batch),
        out_shape=(jax.ShapeDtypeStruct((batch, seq, ML_WIDTH), F32),
                   jax.ShapeDtypeStruct((batch, seq, ML_WIDTH), F32)),
        grid=(n,),
        in_specs=[pl.BlockSpec((batch, SCAN_STEP, 2 * ML_WIDTH), fwd3(0)),
                  pl.BlockSpec((batch, SCAN_STEP, ML_WIDTH), fwd3(COLBLK_MV)),
                  pl.BlockSpec((batch, SCAN_STEP, 2 * ML_WIDTH), bwd3(0)),
                  pl.BlockSpec((batch, SCAN_STEP, ML_WIDTH), bwd3(COLBLK_MV))]
                 + gt_fwd + gt_bwd
                 + [pl.BlockSpec((2 * SUBLANES, LANES), lambda i: (0, 0))],
        out_specs=(pl.BlockSpec((batch, SCAN_STEP, ML_WIDTH), fwd3(0)),
                   pl.BlockSpec((batch, SCAN_STEP, ML_WIDTH), bwd3(0))),
        scratch_shapes=[pltpu.VMEM((batch, 2, ML_HEADS, ML_DH, 2 * ML_DH), F32),
                        pltpu.VMEM((batch * 2, SUBLANES, LANES), F32)],
        compiler_params=_cparams(("arbitrary",)),
        name="mlstm_scan",
    )(qk3, proj3, qk3, proj3, *([gates_t] * (2 * batch)), bias_cr)
    return o_f.reshape(T, ML_WIDTH), o_b.reshape(T, ML_WIDTH)


def _outproj_kernel(yf_ref, yb_ref, xs_ref, z_ref, of_ref, ob_ref, hg_ref, mf_ref, mb_ref, mo_ref,
                    h_ref, dskip_ref, ssdw_ref, hgw_ref, mlw_ref, wout_ref, g_ref, b_ref, o_ref):
    y = (yf_ref[...] + yb_ref[...] + xs_ref[...] * dskip_ref[...]) * _silu(z_ref[...])
    gw = SSD_INNER // SSD_GROUPS
    acc = None
    for g in range(SSD_GROUPS):
        seg = y[:, g * gw:(g + 1) * gw]
        seg = seg * lax.rsqrt(jnp.mean(seg * seg, -1, keepdims=True) + RMS_EPS) * ssdw_ref[:, g * gw:(g + 1) * gw]
        part = _dot(seg.astype(BF16), wout_ref[g * gw:(g + 1) * gw, :])
        acc = part if acc is None else acc + part
    o = of_ref[...] + ob_ref[...]
    hgate = _silu(hg_ref[...])
    m = mf_ref[...] + mb_ref[...]
    mgate = _sigmoid(mo_ref[...])
    for h in range(HG_HEADS):
        cols = slice(h * HG_DK, (h + 1) * HG_DK)
        seg = o[:, cols]
        seg = seg * lax.rsqrt(jnp.mean(seg * seg, -1, keepdims=True) + RMS_EPS) * hgw_ref[:, cols] * hgate[:, cols]
        r0 = SSD_INNER + h * HG_DK
        acc = acc + _dot(seg.astype(BF16), wout_ref[r0:r0 + HG_DK, :])
    for h in range(ML_HEADS):
        cols = slice(h * ML_DH, (h + 1) * ML_DH)
        seg = m[:, cols]
        seg = seg - jnp.mean(seg, -1, keepdims=True)
        seg = seg * lax.rsqrt(jnp.mean(seg * seg, -1, keepdims=True) + LN_EPS) * mlw_ref[:, cols] * mgate[:, cols]
        r0 = SSD_INNER + HG_WIDTH + h * ML_DH
        acc = acc + _dot(seg.astype(BF16), wout_ref[r0:r0 + ML_DH, :])
    o_ref[...] = _layer_norm(ALPHA * h_ref[...] + acc, g_ref[...], b_ref[...])


def _outproj(y_f, y_b, xbc_act, proj, o_f, o_b, m_f, m_b, h, dskip, ssd_w, hg_w, ml_w, w_out, g, b):
    T, D = h.shape
    tm = TM_OUT
    row = lambda width, cb: pl.BlockSpec((tm, width), lambda i: (i, cb))
    par = lambda width: pl.BlockSpec((1, width), lambda i: (0, 0))
    return pl.pallas_call(
        _outproj_kernel,
        out_shape=jax.ShapeDtypeStruct((T, D), F32),
        grid=(T // tm,),
        in_specs=[row(SSD_INNER, 0), row(SSD_INNER, 0), row(SSD_INNER, 0), row(SSD_INNER, COLBLK_Z),
                  row(HG_WIDTH, 0), row(HG_WIDTH, 0), row(HG_WIDTH, COLBLK_HG),
                  row(ML_WIDTH, 0), row(ML_WIDTH, 0), row(ML_WIDTH, COLBLK_MO),
                  row(D, 0), par(SSD_INNER), par(SSD_INNER), par(HG_WIDTH), par(ML_WIDTH),
                  pl.BlockSpec((D_MIX, D), lambda i: (0, 0), pipeline_mode=pl.Buffered(1)),
                  par(D), par(D)],
        out_specs=pl.BlockSpec((tm, D), lambda i: (i, 0)),
        compiler_params=_cparams(("parallel",)),
        name="outproj_ln1",
    )(y_f, y_b, xbc_act, proj, o_f, o_b, proj, m_f, m_b, proj, h, dskip, ssd_w, hg_w, ml_w, w_out,
      g.reshape(1, D), b.reshape(1, D))


def _ffn_kernel(h_ref, w1_ref, w3_ref, w2_ref, g_ref, b_ref, o_ref, acc_ref):
    h = h_ref[...]
    hb = h.astype(BF16)
    for c in range(D_FF_DENSE // TF_FFN):
        cols = slice(c * TF_FFN, (c + 1) * TF_FFN)
        a = _dot(hb, w1_ref[:, cols])
        u = _dot(hb, w3_ref[:, cols])
        part = _dot((_silu(a) * u).astype(BF16), w2_ref[cols, :])
        if c == 0:
            acc_ref[...] = part
        else:
            acc_ref[...] += part
    o_ref[...] = _layer_norm(ALPHA * h + acc_ref[...], g_ref[...], b_ref[...])


def _dense_ffn(h, w1, w3, w2, g, b):
    T, D = h.shape
    tm = TM_FFN
    resident = lambda shape: pl.BlockSpec(shape, lambda i: (0, 0), pipeline_mode=pl.Buffered(1))
    return pl.pallas_call(
        _ffn_kernel,
        out_shape=jax.ShapeDtypeStruct((T, D), F32),
        grid=(T // tm,),
        in_specs=[pl.BlockSpec((tm, D), lambda i: (i, 0)),
                  resident((D, D_FF_DENSE)), resident((D, D_FF_DENSE)), resident((D_FF_DENSE, D)),
                  pl.BlockSpec((1, D), lambda i: (0, 0)), pl.BlockSpec((1, D), lambda i: (0, 0))],
        out_specs=pl.BlockSpec((tm, D), lambda i: (i, 0)),
        scratch_shapes=[pltpu.VMEM((tm, D), F32)],
        compiler_params=_cparams(("parallel",)),
        name="dense_ffn_ln2",
    )(h, w1, w3, w2, g.reshape(1, D), b.reshape(1, D))


def _router_kernel(h_ref, w_ref, idx_ref, gate_ref):
    logits = _dot_hi(h_ref[...], w_ref[...])
    lane = _iota(logits.shape, 1)
    lane_f = lane.astype(F32)
    x = jnp.where(lane < N_EXPERTS, logits, -jnp.inf)
    m1 = jnp.max(x, axis=1, keepdims=True)
    i1 = jnp.min(jnp.where(x == m1, lane_f, float(LANES)), axis=1, keepdims=True)
    x2 = jnp.where(lane_f == i1, -jnp.inf, x)
    m2 = jnp.max(x2, axis=1, keepdims=True)
    i2 = jnp.min(jnp.where(x2 == m2, lane_f, float(LANES)), axis=1, keepdims=True)
    e2 = jnp.exp(m2 - m1)
    g1 = 1.0 / (1.0 + e2)
    g2 = e2 / (1.0 + e2)
    idx_ref[...] = jnp.where(lane == 0, i1, jnp.where(lane == 1, i2, 0.0)).astype(jnp.int32)
    gate_ref[...] = jnp.where(lane == 0, g1, jnp.where(lane == 1, g2, 0.0))


def _router(h, w_pad):
    T, D = h.shape
    tm = TM_ROUTER
    return pl.pallas_call(
        _router_kernel,
        out_shape=(jax.ShapeDtypeStruct((T, LANES), jnp.int32),
                   jax.ShapeDtypeStruct((T, LANES), F32)),
        grid=(T // tm,),
        in_specs=[pl.BlockSpec((tm, D), lambda i: (i, 0)),
                  pl.BlockSpec((D, LANES), lambda i: (0, 0))],
        out_specs=(pl.BlockSpec((tm, LANES), lambda i: (i, 0)),
                   pl.BlockSpec((tm, LANES), lambda i: (i, 0))),
        compiler_params=_cparams(("parallel",)),
        name="moe_router",
    )(h, w_pad)


def _gather_rows(src, idx):
    n = idx.shape[0]
    D = src.shape[1]
    per_worker = n // SC_WORKERS
    n_chunks = per_worker // SC_GATHER_CHUNK
    assert n_chunks * SC_GATHER_CHUNK * SC_WORKERS == n and n_chunks % 2 == 0
    mesh = plsc.VectorSubcoreMesh(core_axis_name="c", subcore_axis_name="s")

    @functools.partial(
        pl.kernel, mesh=mesh,
        out_type=jax.ShapeDtypeStruct((n, D), src.dtype),
        scratch_types=[pltpu.VMEM((2, SC_GATHER_CHUNK), jnp.int32),
                       pltpu.VMEM((2, SC_GATHER_CHUNK, D), src.dtype),
                       pltpu.SemaphoreType.DMA((2,))],
        name="moe_row_gather",
    )
    def gather(src_hbm, idx_hbm, out_hbm, idx_v, rows_v, sems):
        worker = lax.axis_index("s") * SC_CORES + lax.axis_index("c")
        base = worker * per_worker

        def gather_copy(slot):
            return pltpu.make_async_copy(src_hbm.at[idx_v.at[slot]], rows_v.at[slot], sems.at[slot])

        def start(j, slot):
            pltpu.sync_copy(idx_hbm.at[pl.ds(base + j * SC_GATHER_CHUNK, SC_GATHER_CHUNK)], idx_v.at[slot])
            gather_copy(slot).start()

        def finish(j, slot):
            gather_copy(slot).wait()
            pltpu.sync_copy(rows_v.at[slot], out_hbm.at[pl.ds(base + j * SC_GATHER_CHUNK, SC_GATHER_CHUNK)])

        start(0, 0)

        @pl.loop(0, n_chunks, step=2)
        def _(j):
            start(j + 1, 1)
            finish(j, 0)

            @pl.when(j + 2 < n_chunks)
            def _():
                start(j + 2, 0)

            finish(j + 1, 1)

    return gather(src, idx)


def _expert_kernel(be_ref, nact_ref, x_ref, w1_ref, w3_ref, w2_ref, o_ref):
    i = pl.program_id(0)
    f = pl.program_id(1)

    @pl.when(f == 0)
    def _():
        o_ref[...] = jnp.zeros_like(o_ref)

    @pl.when(i < nact_ref[0])
    def _():
        xb = x_ref[...].astype(BF16)
        a = _dot(xb, w1_ref[...].astype(BF16))
        u = _dot(xb, w3_ref[...].astype(BF16))
        o_ref[...] += _dot((_silu(a) * u).astype(BF16), w2_ref[...].astype(BF16))


def _expert_ffn(xs, block_expert, n_active, w1, w3, w2, layer):
    n_rows, D = xs.shape
    n_blocks = n_rows // MOE_BM
    n_ff = D_FF_EXPERT // MOE_TF

    def widx(i, f, be, nact):
        live = i < nact[0]
        return be[jnp.minimum(i, nact[0] - 1)], jnp.where(live, f, n_ff - 1)

    def w13_map(i, f, be, nact):
        e, ff = widx(i, f, be, nact)
        return layer, e, 0, ff

    def w2_map(i, f, be, nact):
        e, ff = widx(i, f, be, nact)
        return layer, e, ff, 0

    grid_spec = pltpu.PrefetchScalarGridSpec(
        num_scalar_prefetch=2,
        grid=(n_blocks, n_ff),
        in_specs=[pl.BlockSpec((MOE_BM, D), lambda i, f, be, nact: (i, 0)),
                  pl.BlockSpec((None, None, D, MOE_TF), w13_map),
                  pl.BlockSpec((None, None, D, MOE_TF), w13_map),
                  pl.BlockSpec((None, None, MOE_TF, D), w2_map)],
        out_specs=pl.BlockSpec((MOE_BM, D), lambda i, f, be, nact: (i, 0)),
    )
    return pl.pallas_call(
        _expert_kernel,
        out_shape=jax.ShapeDtypeStruct((n_rows, D), F32),
        grid_spec=grid_spec,
        compiler_params=_cparams(("arbitrary", "arbitrary")),
        name="moe_experts",
    )(block_expert, n_active, xs, w1, w3, w2)


def _combine_kernel(y0_ref, y1_ref, gate_ref, h_ref, g_ref, b_ref, o_ref):
    gate = gate_ref[...]
    f = y0_ref[...] * gate[:, 0:1] + y1_ref[...] * gate[:, 1:2]
    o_ref[...] = _layer_norm(ALPHA * h_ref[...] + f, g_ref[...], b_ref[...])


def _combine(y_slots, gates, h, g, b):
    T, D = h.shape
    tm = TM_COMBINE
    return pl.pallas_call(
        _combine_kernel,
        out_shape=jax.ShapeDtypeStruct((T, D), F32),
        grid=(T // tm,),
        in_specs=[pl.BlockSpec((tm, D), lambda i: (i, 0)),
                  pl.BlockSpec((tm, D), lambda i: (i + T // tm, 0)),
                  pl.BlockSpec((tm, LANES), lambda i: (i, 0)),
                  pl.BlockSpec((tm, D), lambda i: (i, 0)),
                  pl.BlockSpec((1, D), lambda i: (0, 0)),
                  pl.BlockSpec((1, D), lambda i: (0, 0))],
        out_specs=pl.BlockSpec((tm, D), lambda i: (i, 0)),
        compiler_params=_cparams(("parallel",)),
        name="moe_combine_ln2",
    )(y_slots, y_slots, gates, h, g.reshape(1, D), b.reshape(1, D))


def _moe_ffn(h, router_w, w1, w3, w2, layer, g, b):
    T, D = h.shape
    n_pairs = T * TOP_K
    n_blocks = n_pairs // MOE_BM + N_EXPERTS
    n_rows = n_blocks * MOE_BM
    w_pad = jnp.pad(router_w.astype(F32), ((0, 0), (0, LANES - N_EXPERTS)))
    idx_pad, gates_pad = _router(h, w_pad)
    expert_ids = idx_pad[:, :TOP_K].reshape(-1)
    onehot = (expert_ids[:, None] == jnp.arange(N_EXPERTS, dtype=jnp.int32)[None, :]).astype(jnp.int32)
    running = jnp.cumsum(onehot, axis=0)
    counts = running[-1]
    rank = jnp.sum(onehot * running, axis=1) - 1
    padded = (counts + MOE_BM - 1) // MOE_BM * MOE_BM
    ends = jnp.cumsum(padded)
    pos = (ends - padded)[expert_ids] + rank
    token_ids = jnp.arange(n_pairs, dtype=jnp.int32) // TOP_K
    row_token = (jnp.arange(n_rows, dtype=jnp.int32) % T).at[pos].set(token_ids)
    block_start = jnp.arange(n_blocks, dtype=jnp.int32) * MOE_BM
    block_expert = jnp.minimum(jnp.searchsorted(ends, block_start, side='right'),
                               N_EXPERTS - 1).astype(jnp.int32)
    n_active = (ends[-1:] // MOE_BM).astype(jnp.int32)

    xs = _gather_rows(h, row_token)
    ys = _expert_ffn(xs, block_expert, n_active, w1, w3, w2, layer)
    slot_major = pos.astype(jnp.int32).reshape(T, TOP_K).T.reshape(-1)
    y_slots = _gather_rows(ys, slot_major)
    return _combine(y_slots, gates_pad, h, g, b)


def _relayout_w_in(w):
    sizes = (SSD_INNER, SSD_XBC, 2 * SSD_HEADS, HG_WIDTH, 2 * HG_WIDTH, HG_WIDTH, HG_WIDTH,
             2 * ML_WIDTH, ML_WIDTH, ML_WIDTH, 2 * ML_HEADS, 2 * ML_HEADS)
    offs = [0]
    for s in sizes:
        offs.append(offs[-1] + s)
    (z, xbc, dt, hq, hf, hi, hg, mqk, mv, mo, mig, mfg) = [w[:, offs[k]:offs[k + 1]] for k in range(len(sizes))]
    pad = jnp.zeros((w.shape[0], LANES - 2 * SSD_HEADS - 4 * ML_HEADS), w.dtype)
    wn = jnp.concatenate([z, mqk, hf, xbc, hq, hi, hg, mv, mo], axis=1).astype(BF16)
    wgt = jnp.concatenate([dt, mig, mfg, pad], axis=1).astype(BF16).T
    return wn, wgt


def _gate_rows(vals_by_offset):
    row = jnp.zeros((LANES,), F32)
    for off, v in vals_by_offset:
        row = lax.dynamic_update_slice(row, v.reshape(-1).astype(F32), (off,))
    return jnp.broadcast_to(row[:, None], (LANES, LANES))


def kernel(x, ln_in_g, ln_in_b, w_in, ssd_conv_w, ssd_conv_b, ssd_dt_bias, ssd_a_log, ssd_d, ssd_norm_w, hg_lb_logits, hg_norm_w, ml_conv_w, ml_conv_b, ml_ig_bias, ml_fg_bias, ml_norm_w, w_out, ln1_g, ln1_b, ln2_g, ln2_b, ffn_w1, ffn_w3, ffn_w2, moe_router, moe_w1, moe_w3, moe_w2):
    batch, seq_len, D = x.shape
    T = batch * seq_len
    f32 = lambda t: t.astype(F32)
    lb_soft = jax.nn.softmax(f32(hg_lb_logits), axis=0)
    lb_all = jnp.cumsum(lb_soft, axis=0) - lb_soft[0]

    h = _input_ln(f32(x).reshape(T, D), f32(ln_in_g), f32(ln_in_b))
    for l in range(DEPTH):
        w, wgt = _relayout_w_in(w_in[l])
        bias_cr = _gate_rows(((GATE_DT, ssd_dt_bias[l]), (GATE_IG, ml_ig_bias[l]), (GATE_FG, ml_fg_bias[l])))
        alog_cr = _gate_rows(((GATE_DT, ssd_a_log[l]),))

        proj, gates_t = _inproj(h, w, wgt)
        xbc_act = _conv_silu(proj, f32(ssd_conv_w[l]), f32(ssd_conv_b[l]), COLBLK_XBC, SSD_XBC, seq_len)
        mqk_act = _conv_silu(proj, f32(ml_conv_w[l]), f32(ml_conv_b[l]), COLBLK_MQK, 2 * ML_WIDTH, seq_len)
        y_f, y_b = _ssd_scan(xbc_act, gates_t, bias_cr[GATE_DT:GATE_DT + 2 * SSD_HEADS],
                             alog_cr[GATE_DT:GATE_DT + 2 * SSD_HEADS], batch)
        o_f, o_b = _hgrn2_scan(proj, lb_all[l], batch)
        m_f, m_b = _mlstm_scan(mqk_act, proj, gates_t, bias_cr[GATE_IG:GATE_IG + 4 * ML_HEADS], batch)
        dskip = jnp.repeat(f32(ssd_d[l]), SSD_HEAD_DIM).reshape(1, SSD_INNER)
        h = _outproj(y_f, y_b, xbc_act, proj, o_f, o_b, m_f, m_b, h, dskip,
                     f32(ssd_norm_w[l]).reshape(1, -1), f32(hg_norm_w[l]).reshape(1, -1),
                     f32(ml_norm_w[l]).reshape(1, -1), w_out[l].astype(BF16), f32(ln1_g[l]), f32(ln1_b[l]))
        if l % 2 == 0:
            j = l // 2
            h = _dense_ffn(h, ffn_w1[j].astype(BF16), ffn_w3[j].astype(BF16), ffn_w2[j].astype(BF16),
                           f32(ln2_g[l]), f32(ln2_b[l]))
        else:
            j = l // 2
            h = _moe_ffn(h, moe_router[j], f32(moe_w1), f32(moe_w3), f32(moe_w2), j,
                         f32(ln2_g[l]), f32(ln2_b[l]))
    return h.reshape(batch, seq_len, D).astype(x.dtype)
```

```python
import functools

import jax
import jax.numpy as jnp
from jax import lax
from jax.experimental import pallas as pl
from jax.experimental.pallas import tpu as pltpu
from jax.experimental.pallas import tpu_sc as plsc

F32 = jnp.float32
BF16 = jnp.bfloat16
HIGHEST = lax.Precision.HIGHEST

D_MODEL = 1024
DEPTH = 4
D_MIX = 2 * D_MODEL
SSD_HEADS = 16
SSD_INNER = 1024
SSD_HEAD_DIM = 64
SSD_GROUPS = 2
SSD_STATE = 128
SSD_XBC = 1536
SSD_CHUNK = 128
HG_HEADS = 4
HG_WIDTH = 512
HG_DK = 128
ML_HEADS = 4
ML_WIDTH = 512
ML_DH = 128
CONV_WIDTH = 5
D_FF_DENSE = 2816
N_EXPERTS = 8
TOP_K = 2
D_FF_EXPERT = 3584
LN_EPS = 1e-5
RMS_EPS = 1e-6
MASK_NEG = -1e4
ALPHA = (2.0 * DEPTH) ** 0.25
LOG2E = 1.4426950408889634

LANES = 128
SUBLANES = 8
VMEM_LIMIT = 56 * 1024 * 1024

D_PROJ_PAD = 7168
COLBLK_Z = 0
COLBLK_MQK = 1
COLBLK_XBC = 2
COLBLK_HF_FWD = 4
COLBLK_HF_BWD = 5
COLBLK_HQ = 9
COLBLK_HI = 10
COLBLK_HG = 11
COLBLK_MV = 12
COLBLK_MO = 13
GATE_DT = 0
GATE_IG = 32
GATE_FG = 40

TM_LN = 512
TM_INPROJ = 512
TM_CONV = 512
SCAN_STEP = 128
HG_SUB = 64
HG_REFBLK = 4
TM_OUT = 512
TM_FFN = 512
TF_FFN = 256
TM_ROUTER = 512
MOE_BM = 1024
MOE_TF = 512
SC_CORES = 2
SC_SUBCORES = 16
SC_WORKERS = 32
SC_SCATTER_CHUNK = 128
SC_ROW_WORDS = 128
SC_INIT_CHUNK = 512
SC_GATHER_CHUNK = 32
TM_COMBINE = 256


def _cparams(sem):
    return pltpu.CompilerParams(dimension_semantics=sem, vmem_limit_bytes=VMEM_LIMIT)


def _iota(shape, dim):
    return lax.broadcasted_iota(jnp.int32, shape, dim)


def _sigmoid(x):
    return 1.0 / (1.0 + jnp.exp(-x))


def _silu(x):
    return x * _sigmoid(x)


def _softplus(x):
    return jnp.maximum(x, 0.0) + jnp.log1p(jnp.exp(-jnp.abs(x)))


def _log_sigmoid(x):
    return -_softplus(-x)


def _layer_norm(x, g, b):
    mu = jnp.mean(x, -1, keepdims=True)
    xc = x - mu
    var = jnp.mean(xc * xc, -1, keepdims=True)
    return xc * lax.rsqrt(var + LN_EPS) * g + b


def _causal(n, reverse):
    r = _iota((n, n), 0)
    c = _iota((n, n), 1)
    return (c >= r) if reverse else (c <= r)


def _dot(a, b):
    return jnp.dot(a, b, preferred_element_type=F32)


def _dot_nt(a, b):
    return lax.dot_general(a, b, (((1,), (1,)), ((), ())), preferred_element_type=F32)


def _dot_tn(a, b):
    return lax.dot_general(a, b, (((0,), (0,)), ((), ())), preferred_element_type=F32)


def _dot_hi(a, b):
    return jnp.dot(a, b, precision=HIGHEST, preferred_element_type=F32)


def _ln_kernel(x_ref, g_ref, b_ref, o_ref):
    o_ref[...] = _layer_norm(x_ref[...], g_ref[...], b_ref[...])


def _input_ln(x, g, b):
    T, D = x.shape
    return pl.pallas_call(
        _ln_kernel,
        out_shape=jax.ShapeDtypeStruct((T, D), F32),
        grid=(T // TM_LN,),
        in_specs=[pl.BlockSpec((TM_LN, D), lambda i: (i, 0)),
                  pl.BlockSpec((1, D), lambda i: (0, 0)),
                  pl.BlockSpec((1, D), lambda i: (0, 0))],
        out_specs=pl.BlockSpec((TM_LN, D), lambda i: (i, 0)),
        compiler_params=_cparams(("parallel",)),
        name="input_ln",
    )(x, g.reshape(1, D), b.reshape(1, D))


def _inproj_kernel(h_ref, w_ref, wgt_ref, proj_ref, gt_ref):
    hb = h_ref[...].astype(BF16)
    proj_ref[...] = _dot(hb, w_ref[...])
    gt_ref[...] = _dot_nt(wgt_ref[...], hb)


def _inproj(h, w, wgt):
    T, D = h.shape
    tm = TM_INPROJ
    return pl.pallas_call(
        _inproj_kernel,
        out_shape=(jax.ShapeDtypeStruct((T, D_PROJ_PAD), F32),
                   jax.ShapeDtypeStruct((LANES, T), F32)),
        grid=(T // tm,),
        in_specs=[pl.BlockSpec((tm, D), lambda i: (i, 0)),
                  pl.BlockSpec((D, D_PROJ_PAD), lambda i: (0, 0), pipeline_mode=pl.Buffered(1)),
                  pl.BlockSpec((LANES, D), lambda i: (0, 0), pipeline_mode=pl.Buffered(1))],
        out_specs=(pl.BlockSpec((tm, D_PROJ_PAD), lambda i: (i, 0)),
                   pl.BlockSpec((LANES, tm), lambda i: (0, i))),
        compiler_params=_cparams(("parallel",)),
        name="inproj",
    )(h, w, wgt)


def _conv_kernel(prev_ref, x_ref, next_ref, w_ref, b_ref, o_ref, *, tm, tiles_per_seq):
    i = pl.program_id(0)
    pos = i % tiles_per_seq
    x = x_ref[...]
    prev = jnp.where(pos == 0, 0.0, prev_ref[...])
    nxt = jnp.where(pos == tiles_per_seq - 1, 0.0, next_ref[...])
    xe = jnp.concatenate([prev, x, nxt], axis=0)
    n = tm + 2 * SUBLANES
    acc = b_ref[...] + w_ref[2:3, :] * x
    for k in (0, 1, 3, 4):
        shifted = pltpu.roll(xe, (2 - k) % n, 0)
        acc = acc + w_ref[k:k + 1, :] * shifted[SUBLANES:SUBLANES + tm]
    o_ref[...] = _silu(acc)


def _conv_silu(proj, w, b, colblk, width, seq_len):
    T = proj.shape[0]
    tm = TM_CONV
    rb = tm // SUBLANES
    nrb = T // SUBLANES
    kern = functools.partial(_conv_kernel, tm=tm, tiles_per_seq=seq_len // tm)
    return pl.pallas_call(
        kern,
        out_shape=jax.ShapeDtypeStruct((T, width), F32),
        grid=(T // tm,),
        in_specs=[pl.BlockSpec((SUBLANES, width), lambda i: (jnp.maximum(i * rb - 1, 0), colblk)),
                  pl.BlockSpec((tm, width), lambda i: (i, colblk)),
                  pl.BlockSpec((SUBLANES, width), lambda i: (jnp.minimum((i + 1) * rb, nrb - 1), colblk)),
                  pl.BlockSpec((CONV_WIDTH, width), lambda i: (0, 0)),
                  pl.BlockSpec((1, width), lambda i: (0, 0))],
        out_specs=pl.BlockSpec((tm, width), lambda i: (i, 0)),
        compiler_params=_cparams(("parallel",)),
        name="conv_silu",
    )(proj, proj, proj, w, b.reshape(1, width))


def _cumsum_lanes(x, reverse):
    n = x.shape[1]
    r = _iota((3 * n, n), 0) % n
    c = _iota((3 * n, n), 1)
    tri3 = ((r >= c) if reverse else (r <= c)).astype(BF16)
    return _dot(_split3(x, axis=1), tri3)


def _cols_of(rows_list):
    used = sum(r.shape[0] for r in rows_list)
    pad = jnp.zeros((LANES - used, LANES), F32)
    return jnp.concatenate(list(rows_list) + [pad], axis=0).T


def _ssd_gates(gates_t, bias_cr, alog_cr, reverse):
    last = 0 if reverse else SSD_CHUNK - 1
    dt_r = _softplus(gates_t + bias_cr)
    acs_r = _cumsum_lanes(dt_r * (-jnp.exp(alog_cr)) * LOG2E, reverse)
    a_last = acs_r[:, last:last + 1]
    w_r = dt_r * jnp.exp2(a_last - acs_r)
    return acs_r - jnp.log2(dt_r), acs_r, w_r, jnp.exp2(a_last)


def _ssd_direction(x, gate_rows, acs_cols, st_ref, d, y_ref):
    reverse = d == 1
    n = SSD_CHUNK
    tri = _causal(n, reverse)
    lane_lo = _iota((n, LANES), 1) < SSD_HEAD_DIM
    acs_dt_r, _, w_r, dec = gate_rows

    for g in range(SSD_GROUPS):
        b0 = SSD_INNER + g * SSD_STATE
        c0 = SSD_INNER + SSD_GROUPS * SSD_STATE + g * SSD_STATE
        bg = x[:, b0:b0 + SSD_STATE]
        cgb = x[:, c0:c0 + SSD_STATE].astype(BF16)
        cb = _dot_nt(cgb, bg.astype(BF16))
        bg_t = bg.T
        for pair in range(4):
            p0 = (g * 4 + pair) * LANES
            xp = x[:, p0:p0 + LANES]
            ypair = None
            for hh in range(2):
                h = g * 8 + pair * 2 + hh
                c = d * SSD_HEADS + h
                xm = jnp.where(lane_lo if hh == 0 else jnp.logical_not(lane_lo), xp, 0.0).astype(BF16)
                col = jnp.broadcast_to(acs_cols[:, c:c + 1], (n, n))
                decay_dt = jnp.exp2(jnp.where(tri, col - acs_dt_r[h:h + 1, :], MASK_NEG))
                m = (cb * decay_dt).astype(BF16)
                st = st_ref[d, h]
                y = _dot(m, xm) + jnp.exp2(col) * _dot(cgb, st.astype(BF16))
                bh = (bg_t * w_r[h:h + 1, :]).astype(BF16)
                st_ref[d, h] = st * dec[h:h + 1, :] + _dot(bh, xm)
                ypair = y if ypair is None else ypair + y
            y_ref[:, p0:p0 + LANES] = ypair


def _ssd_kernel(xf_ref, xb_ref, gtf_ref, gtb_ref, bcr_ref, acr_ref, yf_ref, yb_ref, st_ref):
    @pl.when(pl.program_id(1) == 0)
    def _():
        st_ref[...] = jnp.zeros_like(st_ref)

    gf = _ssd_gates(gtf_ref[...], bcr_ref[0:SSD_HEADS, :], acr_ref[0:SSD_HEADS, :], False)
    gb = _ssd_gates(gtb_ref[...], bcr_ref[SSD_HEADS:2 * SSD_HEADS, :], acr_ref[SSD_HEADS:2 * SSD_HEADS, :], True)
    acs_cols = _cols_of([gf[1], gb[1]])
    _ssd_direction(xf_ref[...], gf, acs_cols, st_ref, 0, yf_ref)
    _ssd_direction(xb_ref[...], gb, acs_cols, st_ref, 1, yb_ref)


def _ssd_scan(xbc_act, gates_t, bias_cr, alog_cr, batch):
    T = xbc_act.shape[0]
    n = T // batch // SSD_CHUNK
    fwd = lambda b, i: b * n + i
    bwd = lambda b, i: b * n + (n - 1 - i)
    const = lambda b, i: (0, 0)
    return pl.pallas_call(
        _ssd_kernel,
        out_shape=(jax.ShapeDtypeStruct((T, SSD_INNER), F32),
                   jax.ShapeDtypeStruct((T, SSD_INNER), F32)),
        grid=(batch, n),
        in_specs=[pl.BlockSpec((SSD_CHUNK, SSD_XBC), lambda b, i: (fwd(b, i), 0)),
                  pl.BlockSpec((SSD_CHUNK, SSD_XBC), lambda b, i: (bwd(b, i), 0)),
                  pl.BlockSpec((SSD_HEADS, SSD_CHUNK), lambda b, i: (0, fwd(b, i))),
                  pl.BlockSpec((SSD_HEADS, SSD_CHUNK), lambda b, i: (1, bwd(b, i))),
                  pl.BlockSpec((2 * SSD_HEADS, LANES), const),
                  pl.BlockSpec((2 * SSD_HEADS, LANES), const)],
        out_specs=(pl.BlockSpec((SSD_CHUNK, SSD_INNER), lambda b, i: (fwd(b, i), 0)),
                   pl.BlockSpec((SSD_CHUNK, SSD_INNER), lambda b, i: (bwd(b, i), 0))),
        scratch_shapes=[pltpu.VMEM((2, SSD_HEADS, SSD_STATE, LANES), F32)],
        compiler_params=_cparams(("arbitrary", "arbitrary")),
        name="ssd_scan",
    )(xbc_act, xbc_act, gates_t, gates_t, bias_cr, alog_cr)


def _split3(x, axis=0):
    hi = x.astype(BF16)
    r1 = x - hi.astype(F32)
    mid = r1.astype(BF16)
    lo = (r1 - mid.astype(F32)).astype(BF16)
    return jnp.concatenate([hi, mid, lo], axis=axis)


def _hgrn2_cumsum_lhs(reverse):
    n = HG_SUB
    r_i = _iota((2 * n, n), 0)
    c_i = _iota((2 * n, n), 1)
    t = jnp.where(r_i < n, r_i, r_i - n)
    ref = (t // HG_REFBLK) * HG_REFBLK + (HG_REFBLK - 1 if reverse else 0)
    upto = jnp.where(r_i < n, t, ref)
    m = ((c_i >= upto) if reverse else (c_i <= upto)).astype(BF16)
    return jnp.concatenate([m, m, m], axis=1)


def _hgrn2_intra(q, f, v, lb, qmask, reverse):
    n = HG_SUB
    nblk = n // HG_REFBLK
    slab = 2 * SUBLANES
    tri = _causal(n, reverse)
    last = 0 if reverse else n - 1
    rowblk = _iota((slab, HG_DK), 0) // HG_REFBLK

    qh = _silu(q)
    key = (1.0 - lb) / (1.0 + jnp.exp(f))
    g = jnp.log2(1.0 - key)
    gg = _dot(_hgrn2_cumsum_lhs(reverse), _split3(g))
    gc = gg[:n]
    gref = gg[n:]
    qg = (qh * jnp.exp2(gc - gref)).astype(BF16)
    qs = (qh * jnp.exp2(gc)).astype(BF16)
    g_last = gc[last:last + 1, :]
    hk = gc - jnp.log2(key)
    kd = jnp.exp2(g_last - hk).astype(BF16)
    dec = jnp.exp2(g_last)
    vb = v.astype(BF16)

    o_intra = []
    for h in range(HG_HEADS):
        cols = slice(h * HG_DK, (h + 1) * HG_DK)
        hk_h = hk[:, cols]
        gc_h = gc[:, cols]
        q_ext = jnp.tile(qg[:, cols], (1, nblk)) * qmask
        k_parts = []
        for j in range(nblk):
            ref_row = j * HG_REFBLK + (HG_REFBLK - 1 if reverse else 0)
            rj = gc_h[ref_row:ref_row + 1, :]
            edge = (j * HG_REFBLK) // slab
            slabs = range(edge, n // slab) if reverse else range(0, edge + 1)
            pieces = []
            for sidx in range(n // slab):
                rows = slice(sidx * slab, (sidx + 1) * slab)
                if sidx not in slabs:
                    pieces.append(jnp.zeros((slab, HG_DK), BF16))
                    continue
                arg = rj - hk_h[rows]
                if sidx == edge:
                    jb = j - edge * (slab // HG_REFBLK)
                    live = (rowblk >= jb) if reverse else (rowblk <= jb)
                    arg = jnp.where(live, arg, MASK_NEG)
                pieces.append(jnp.exp2(arg).astype(BF16))
            k_parts.append(jnp.concatenate(pieces, axis=0))
        att = _dot_nt(q_ext, jnp.concatenate(k_parts, axis=1))
        att = jnp.where(tri, att, 0.0).astype(BF16)
        o_intra.append(_dot(att, vb[:, cols]))
    return o_intra, qs, kd, dec, vb


def _hgrn2_state(intra, st_ref, d, o_ref, rows):
    o_intra, qs, kd, dec, vb = intra
    for h in range(HG_HEADS):
        cols = slice(h * HG_DK, (h + 1) * HG_DK)
        st = st_ref[d, h]
        o_ref[rows, cols] = o_intra[h] + _dot_nt(qs[:, cols], st.astype(BF16))
        st_ref[d, h] = st * dec[:, cols] + _dot_tn(vb[:, cols], kd[:, cols])


def _hgrn2_kernel(qf_ref, ff_ref, vf_ref, qb_ref, fb_ref, vb_ref, lb_ref, qmask_ref, of_ref, ob_ref, st_ref):
    @pl.when(pl.program_id(1) == 0)
    def _():
        st_ref[...] = jnp.zeros_like(st_ref)

    lb = lb_ref[...]
    qmask = qmask_ref[...]
    refs = ((qf_ref, ff_ref, vf_ref, of_ref), (qb_ref, fb_ref, vb_ref, ob_ref))
    nsub = SCAN_STEP // HG_SUB
    units = []
    for k in range(nsub):
        units.append((0, k))
        units.append((1, nsub - 1 - k))
    pending = None
    for d, sc in units:
        rows = slice(sc * HG_SUB, (sc + 1) * HG_SUB)
        q_ref, f_ref, v_ref, o_ref = refs[d]
        intra = _hgrn2_intra(q_ref[rows, :], f_ref[rows, :], v_ref[rows, :], lb, qmask, d == 1)
        if pending is not None:
            _hgrn2_state(*pending)
        pending = (intra, st_ref, d, o_ref, rows)
    _hgrn2_state(*pending)


def _hgrn2_scan(proj, lb, batch):
    T = proj.shape[0]
    n = T // batch // SCAN_STEP
    fwd = lambda b, i: b * n + i
    bwd = lambda b, i: b * n + (n - 1 - i)
    blk = (SCAN_STEP, HG_WIDTH)
    nblk = HG_SUB // HG_REFBLK
    qmask = (jnp.arange(HG_SUB)[:, None] // HG_REFBLK == jnp.arange(nblk * HG_DK)[None, :] // HG_DK).astype(BF16)
    return pl.pallas_call(
        _hgrn2_kernel,
        out_shape=(jax.ShapeDtypeStruct((T, HG_WIDTH), F32),
                   jax.ShapeDtypeStruct((T, HG_WIDTH), F32)),
        grid=(batch, n),
        in_specs=[pl.BlockSpec(blk, lambda b, i: (fwd(b, i), COLBLK_HQ)),
                  pl.BlockSpec(blk, lambda b, i: (fwd(b, i), COLBLK_HF_FWD)),
                  pl.BlockSpec(blk, lambda b, i: (fwd(b, i), COLBLK_HI)),
                  pl.BlockSpec(blk, lambda b, i: (bwd(b, i), COLBLK_HQ)),
                  pl.BlockSpec(blk, lambda b, i: (bwd(b, i), COLBLK_HF_BWD)),
                  pl.BlockSpec(blk, lambda b, i: (bwd(b, i), COLBLK_HI)),
                  pl.BlockSpec((1, HG_WIDTH), lambda b, i: (0, 0)),
                  pl.BlockSpec((HG_SUB, nblk * HG_DK), lambda b, i: (0, 0))],
        out_specs=(pl.BlockSpec(blk, lambda b, i: (fwd(b, i), 0)),
                   pl.BlockSpec(blk, lambda b, i: (bwd(b, i), 0))),
        scratch_shapes=[pltpu.VMEM((2, HG_HEADS, HG_DK, HG_DK), F32)],
        compiler_params=_cparams(("arbitrary", "arbitrary")),
        name="hgrn2_scan",
    )(proj, proj, proj, proj, proj, proj, lb.reshape(1, HG_WIDTH), qmask)


def _mlstm_kernel(batch, qkf_ref, vf_ref, qkb_ref, vb_ref, *rest):
    gt_refs = (rest[:batch], rest[batch:2 * batch])
    bcr_ref, of_ref, ob_ref, c_ref, m_ref = rest[2 * batch:]

    @pl.when(pl.program_id(0) == 0)
    def _():
        c_ref[...] = jnp.zeros_like(c_ref)
        m_ref[...] = jnp.zeros_like(m_ref)

    n = SCAN_STEP
    ones = jnp.ones((n, ML_DH), BF16)
    io_refs = ((qkf_ref, vf_ref, of_ref), (qkb_ref, vb_ref, ob_ref))
    tris = (_causal(n, False), _causal(n, True))
    lane = _iota((SUBLANES, n), 1)

    gates = {}
    for b in range(batch):
        for d in range(2):
            reverse = d == 1
            last = 0 if reverse else n - 1
            g_r = gt_refs[d][b][...] + bcr_ref[...]
            bc = _cumsum_lanes(_log_sigmoid(g_r) * LOG2E, reverse)[SUBLANES:]
            ig = g_r[:SUBLANES] * LOG2E
            u = ig - bc
            pm = u
            for sh in (1, 2, 4, 8, 16, 32, 64):
                shifted = pltpu.roll(pm, (n - sh) if reverse else sh, 1)
                valid = (lane < n - sh) if reverse else (lane >= sh)
                pm = jnp.maximum(pm, jnp.where(valid, shifted, -jnp.inf))
            m_prev = m_ref[b * 2 + d]
            floor = jnp.where((lane > 0) if reverse else (lane < n - 1), MASK_NEG * LOG2E, -jnp.inf)
            m_t = jnp.maximum(jnp.maximum(bc + pm, bc + m_prev), floor)
            b_last = bc[:, last:last + 1]
            m_new = jnp.maximum(b_last + m_prev[:, 0:1], b_last + pm[:, last:last + 1])
            w_s = jnp.exp2(b_last - bc + ig - m_new)
            dec = jnp.exp2(b_last + m_prev[:, 0:1] - m_new)
            m_ref[b * 2 + d] = jnp.broadcast_to(m_new, (SUBLANES, LANES))
            gates[b, d] = (u, _cols_of([bc, m_t]), m_prev, w_s, dec)

    units = [(b, d, h) for h in range(ML_HEADS) for b in range(batch) for d in range(2)]
    for b, d, h in units:
        u, cols, m_prev, w_s, dec = gates[b, d]
        qk_ref, v_ref, o_ref = io_refs[d]
        r = d * ML_HEADS + h
        q = (qk_ref[b, :, h * ML_DH:(h + 1) * ML_DH] * (ML_DH ** -0.5)).astype(BF16)
        k = qk_ref[b, :, ML_WIDTH + h * ML_DH:ML_WIDTH + (h + 1) * ML_DH]
        ve = jnp.concatenate([v_ref[b, :, h * ML_DH:(h + 1) * ML_DH].astype(BF16), ones], axis=1)
        b_minus_m = jnp.broadcast_to(cols[:, r:r + 1] - cols[:, SUBLANES + r:SUBLANES + r + 1], (n, n))
        m_t = jnp.broadcast_to(cols[:, SUBLANES + r:SUBLANES + r + 1], (n, n))
        w_intra = jnp.exp2(jnp.where(tris[d], b_minus_m + u[r:r + 1, :], MASK_NEG))
        w_inter = jnp.exp2(b_minus_m + m_prev[r:r + 1, 0:1])
        s = (_dot_nt(q, k.astype(BF16)) * w_intra).astype(BF16)
        cst = c_ref[b, d, h]
        numden = _dot(s, ve) + jnp.concatenate([w_inter, w_inter], axis=1) * _dot(q, cst.astype(BF16))
        den = numden[:, ML_DH:]
        o_ref[b, :, h * ML_DH:(h + 1) * ML_DH] = numden[:, :ML_DH] / jnp.maximum(jnp.abs(den), jnp.exp2(-m_t))
        kw_t = (k.T * w_s[r:r + 1, :]).astype(BF16)
        c_ref[b, d, h] = dec[r:r + 1, :] * cst + _dot(kw_t, ve)


def _mlstm_scan(mqk_act, proj, gates_t, bias_cr, batch):
    T = proj.shape[0]
    seq = T // batch
    n = seq // SCAN_STEP
    gt_rowblk = GATE_IG // (2 * SUBLANES)
    fwd3 = lambda cb: (lambda i: (0, i, cb))
    bwd3 = lambda cb: (lambda i: (0, n - 1 - i, cb))
    gt_fwd = [pl.BlockSpec((2 * SUBLANES, SCAN_STEP), functools.partial(lambda b, i: (gt_rowblk, b * n + i), b))
              for b in range(batch)]
    gt_bwd = [pl.BlockSpec((2 * SUBLANES, SCAN_STEP),
                           functools.partial(lambda b, i: (gt_rowblk, b * n + n - 1 - i), b))
              for b in range(batch)]
    qk3 = mqk_act.reshape(batch, seq, 2 * ML_WIDTH)
    proj3 = proj.reshape(batch, seq, D_PROJ_PAD)
    o_f, o_b = pl.pallas_call(
        functools.partial(_mlstm_kernel, batch),
        out_shape=(jax.ShapeDtypeStruct((batch, seq, ML_WIDTH), F32),
                   jax.ShapeDtypeStruct((batch, seq, ML_WIDTH), F32)),
        grid=(n,),
        in_specs=[pl.BlockSpec((batch, SCAN_STEP, 2 * ML_WIDTH), fwd3(0)),
                  pl.BlockSpec((batch, SCAN_STEP, ML_WIDTH), fwd3(COLBLK_MV)),
                  pl.BlockSpec((batch, SCAN_STEP, 2 * ML_WIDTH), bwd3(0)),
                  pl.BlockSpec((batch, SCAN_STEP, ML_WIDTH), bwd3(COLBLK_MV))]
                 + gt_fwd + gt_bwd
                 + [pl.BlockSpec((2 * SUBLANES, LANES), lambda i: (0, 0))],
        out_specs=(pl.BlockSpec((batch, SCAN_STEP, ML_WIDTH), fwd3(0)),
                   pl.BlockSpec((batch, SCAN_STEP, ML_WIDTH), bwd3(0))),
        scratch_shapes=[pltpu.VMEM((batch, 2, ML_HEADS, ML_DH, 2 * ML_DH), F32),
                        pltpu.VMEM((batch * 2, SUBLANES, LANES), F32)],
        compiler_params=_cparams(("arbitrary",)),
        name="mlstm_scan",
    )(qk3, proj3, qk3, proj3, *([gates_t] * (2 * batch)), bias_cr)
    return o_f.reshape(T, ML_WIDTH), o_b.reshape(T, ML_WIDTH)


def _outproj_kernel(yf_ref, yb_ref, xs_ref, z_ref, of_ref, ob_ref, hg_ref, mf_ref, mb_ref, mo_ref,
                    h_ref, dskip_ref, ssdw_ref, hgw_ref, mlw_ref, wout_ref, g_ref, b_ref, o_ref):
    y = (yf_ref[...] + yb_ref[...] + xs_ref[...] * dskip_ref[...]) * _silu(z_ref[...])
    gw = SSD_INNER // SSD_GROUPS
    acc = None
    for g in range(SSD_GROUPS):
        seg = y[:, g * gw:(g + 1) * gw]
        seg = seg * lax.rsqrt(jnp.mean(seg * seg, -1, keepdims=True) + RMS_EPS) * ssdw_ref[:, g * gw:(g + 1) * gw]
        part = _dot(seg.astype(BF16), wout_ref[g * gw:(g + 1) * gw, :])
        acc = part if acc is None else acc + part
    o = of_ref[...] + ob_ref[...]
    hgate = _silu(hg_ref[...])
    m = mf_ref[...] + mb_ref[...]
    mgate = _sigmoid(mo_ref[...])
    for h in range(HG_HEADS):
        cols = slice(h * HG_DK, (h + 1) * HG_DK)
        seg = o[:, cols]
        seg = seg * lax.rsqrt(jnp.mean(seg * seg, -1, keepdims=True) + RMS_EPS) * hgw_ref[:, cols] * hgate[:, cols]
        r0 = SSD_INNER + h * HG_DK
        acc = acc + _dot(seg.astype(BF16), wout_ref[r0:r0 + HG_DK, :])
    for h in range(ML_HEADS):
        cols = slice(h * ML_DH, (h + 1) * ML_DH)
        seg = m[:, cols]
        seg = seg - jnp.mean(seg, -1, keepdims=True)
        seg = seg * lax.rsqrt(jnp.mean(seg * seg, -1, keepdims=True) + LN_EPS) * mlw_ref[:, cols] * mgate[:, cols]
        r0 = SSD_INNER + HG_WIDTH + h * ML_DH
        acc = acc + _dot(seg.astype(BF16), wout_ref[r0:r0 + ML_DH, :])
    o_ref[...] = _layer_norm(ALPHA * h_ref[...] + acc, g_ref[...], b_ref[...])


def _outproj(y_f, y_b, xbc_act, proj, o_f, o_b, m_f, m_b, h, dskip, ssd_w, hg_w, ml_w, w_out, g, b):
    T, D = h.shape
    tm = TM_OUT
    row = lambda width, cb: pl.BlockSpec((tm, width), lambda i: (i, cb))
    par = lambda width: pl.BlockSpec((1, width), lambda i: (0, 0))
    return pl.pallas_call(
        _outproj_kernel,
        out_shape=jax.ShapeDtypeStruct((T, D), F32),
        grid=(T // tm,),
        in_specs=[row(SSD_INNER, 0), row(SSD_INNER, 0), row(SSD_INNER, 0), row(SSD_INNER, COLBLK_Z),
                  row(HG_WIDTH, 0), row(HG_WIDTH, 0), row(HG_WIDTH, COLBLK_HG),
                  row(ML_WIDTH, 0), row(ML_WIDTH, 0), row(ML_WIDTH, COLBLK_MO),
                  row(D, 0), par(SSD_INNER), par(SSD_INNER), par(HG_WIDTH), par(ML_WIDTH),
                  pl.BlockSpec((D_MIX, D), lambda i: (0, 0), pipeline_mode=pl.Buffered(1)),
                  par(D), par(D)],
        out_specs=pl.BlockSpec((tm, D), lambda i: (i, 0)),
        compiler_params=_cparams(("parallel",)),
        name="outproj_ln1",
    )(y_f, y_b, xbc_act, proj, o_f, o_b, proj, m_f, m_b, proj, h, dskip, ssd_w, hg_w, ml_w, w_out,
      g.reshape(1, D), b.reshape(1, D))


def _ffn_kernel(h_ref, w1_ref, w3_ref, w2_ref, g_ref, b_ref, o_ref, acc_ref):
    h = h_ref[...]
    hb = h.astype(BF16)
    for c in range(D_FF_DENSE // TF_FFN):
        cols = slice(c * TF_FFN, (c + 1) * TF_FFN)
        a = _dot(hb, w1_ref[:, cols])
        u = _dot(hb, w3_ref[:, cols])
        part = _dot((_silu(a) * u).astype(BF16), w2_ref[cols, :])
        if c == 0:
            acc_ref[...] = part
        else:
            acc_ref[...] += part
    o_ref[...] = _layer_norm(ALPHA * h + acc_ref[...], g_ref[...], b_ref[...])


def _dense_ffn(h, w1, w3, w2, g, b):
    T, D = h.shape
    tm = TM_FFN
    resident = lambda shape: pl.BlockSpec(shape, lambda i: (0, 0), pipeline_mode=pl.Buffered(1))
    return pl.pallas_call(
        _ffn_kernel,
        out_shape=jax.ShapeDtypeStruct((T, D), F32),
        grid=(T // tm,),
        in_specs=[pl.BlockSpec((tm, D), lambda i: (i, 0)),
                  resident((D, D_FF_DENSE)), resident((D, D_FF_DENSE)), resident((D_FF_DENSE, D)),
                  pl.BlockSpec((1, D), lambda i: (0, 0)), pl.BlockSpec((1, D), lambda i: (0, 0))],
        out_specs=pl.BlockSpec((tm, D), lambda i: (i, 0)),
        scratch_shapes=[pltpu.VMEM((tm, D), F32)],
        compiler_params=_cparams(("parallel",)),
        name="dense_ffn_ln2",
    )(h, w1, w3, w2, g.reshape(1, D), b.reshape(1, D))


def _router_kernel(h_ref, w_ref, idx_ref, gate_ref):
    logits = _dot_hi(h_ref[...], w_ref[...])
    lane = _iota(logits.shape, 1)
    lane_f = lane.astype(F32)
    x = jnp.where(lane < N_EXPERTS, logits, -jnp.inf)
    m1 = jnp.max(x, axis=1, keepdims=True)
    i1 = jnp.min(jnp.where(x == m1, lane_f, float(LANES)), axis=1, keepdims=True)
    x2 = jnp.where(lane_f == i1, -jnp.inf, x)
    m2 = jnp.max(x2, axis=1, keepdims=True)
    i2 = jnp.min(jnp.where(x2 == m2, lane_f, float(LANES)), axis=1, keepdims=True)
    e2 = jnp.exp(m2 - m1)
    g1 = 1.0 / (1.0 + e2)
    g2 = e2 / (1.0 + e2)
    idx_ref[...] = jnp.where(lane == 0, i1, jnp.where(lane == 1, i2, 0.0)).astype(jnp.int32)
    gate_ref[...] = jnp.where(lane == 0, g1, jnp.where(lane == 1, g2, 0.0))


def _router(h, w_pad):
    T, D = h.shape
    tm = TM_ROUTER
    return pl.pallas_call(
        _router_kernel,
        out_shape=(jax.ShapeDtypeStruct((T, LANES), jnp.int32),
                   jax.ShapeDtypeStruct((T, LANES), F32)),
        grid=(T // tm,),
        in_specs=[pl.BlockSpec((tm, D), lambda i: (i, 0)),
                  pl.BlockSpec((D, LANES), lambda i: (0, 0))],
        out_specs=(pl.BlockSpec((tm, LANES), lambda i: (i, 0)),
                   pl.BlockSpec((tm, LANES), lambda i: (i, 0))),
        compiler_params=_cparams(("parallel",)),
        name="moe_router",
    )(h, w_pad)


def _gather_rows(src, idx):
    n = idx.shape[0]
    D = src.shape[1]
    per_worker = n // SC_WORKERS
    n_chunks = per_worker // SC_GATHER_CHUNK
    assert n_chunks * SC_GATHER_CHUNK * SC_WORKERS == n and n_chunks % 2 == 0
    mesh = plsc.VectorSubcoreMesh(core_axis_name="c", subcore_axis_name="s")

    @functools.partial(
        pl.kernel, mesh=mesh,
        out_type=jax.ShapeDtypeStruct((n, D), src.dtype),
        scratch_types=[pltpu.VMEM((2, SC_GATHER_CHUNK), jnp.int32),
                       pltpu.VMEM((2, SC_GATHER_CHUNK, D), src.dtype),
                       pltpu.SemaphoreType.DMA((2,))],
        name="moe_row_gather",
    )
    def gather(src_hbm, idx_hbm, out_hbm, idx_v, rows_v, sems):
        worker = lax.axis_index("s") * SC_CORES + lax.axis_index("c")
        base = worker * per_worker

        def gather_copy(slot):
            return pltpu.make_async_copy(src_hbm.at[idx_v.at[slot]], rows_v.at[slot], sems.at[slot])

        def start(j, slot):
            pltpu.sync_copy(idx_hbm.at[pl.ds(base + j * SC_GATHER_CHUNK, SC_GATHER_CHUNK)], idx_v.at[slot])
            gather_copy(slot).start()

        def finish(j, slot):
            gather_copy(slot).wait()
            pltpu.sync_copy(rows_v.at[slot], out_hbm.at[pl.ds(base + j * SC_GATHER_CHUNK, SC_GATHER_CHUNK)])

        start(0, 0)

        @pl.loop(0, n_chunks, step=2)
        def _(j):
            start(j + 1, 1)
            finish(j, 0)

            @pl.when(j + 2 < n_chunks)
            def _():
                start(j + 2, 0)

            finish(j + 1, 1)

    return gather(src, idx)


def _scatter_rows(init, vals, idx):
    n_out, W = init.shape
    n = idx.shape[0]
    out_per = n_out // SC_SUBCORES
    per = n // SC_SUBCORES
    n_chunks = per // SC_SCATTER_CHUNK
    init_chunks = out_per // SC_INIT_CHUNK
    assert init_chunks * SC_INIT_CHUNK * SC_SUBCORES == n_out and n_chunks * SC_SCATTER_CHUNK * SC_SUBCORES == n
    mesh = plsc.VectorSubcoreMesh(core_axis_name="c", subcore_axis_name="s", num_cores=1)

    @functools.partial(
        pl.kernel, mesh=mesh,
        out_type=jax.ShapeDtypeStruct((n_out, W), init.dtype),
        scratch_types=[pltpu.VMEM((SC_INIT_CHUNK, W), init.dtype),
                       pltpu.VMEM((SC_SCATTER_CHUNK,), jnp.int32),
                       pltpu.VMEM((SC_SCATTER_CHUNK, W), init.dtype)],
        name="moe_row_scatter",
    )
    def scatter(init_hbm, vals_hbm, idx_hbm, out_hbm, init_v, idx_v, vals_v):
        worker = lax.axis_index("s")

        @pl.loop(0, init_chunks)
        def _(j):
            off = worker * out_per + j * SC_INIT_CHUNK
            pltpu.sync_copy(init_hbm.at[pl.ds(off, SC_INIT_CHUNK)], init_v)
            pltpu.sync_copy(init_v, out_hbm.at[pl.ds(off, SC_INIT_CHUNK)])

        plsc.subcore_barrier()

        @pl.loop(0, n_chunks)
        def _(j):
            off = worker * per + j * SC_SCATTER_CHUNK
            pltpu.sync_copy(idx_hbm.at[pl.ds(off, SC_SCATTER_CHUNK)], idx_v)
            pltpu.sync_copy(vals_hbm.at[pl.ds(off, SC_SCATTER_CHUNK)], vals_v)
            pltpu.sync_copy(vals_v, out_hbm.at[idx_v])

    return scatter(init, vals, idx)


def _expert_kernel(be_ref, nact_ref, x_ref, w1_ref, w3_ref, w2_ref, o_ref):
    i = pl.program_id(0)
    f = pl.program_id(1)

    @pl.when(f == 0)
    def _():
        o_ref[...] = jnp.zeros_like(o_ref)

    @pl.when(i < nact_ref[0])
    def _():
        xb = x_ref[...].astype(BF16)
        a = _dot(xb, w1_ref[...].astype(BF16))
        u = _dot(xb, w3_ref[...].astype(BF16))
        o_ref[...] += _dot((_silu(a) * u).astype(BF16), w2_ref[...].astype(BF16))


def _expert_ffn(xs, block_expert, n_active, w1, w3, w2, layer):
    n_rows, D = xs.shape
    n_blocks = n_rows // MOE_BM
    n_ff = D_FF_EXPERT // MOE_TF

    def widx(i, f, be, nact):
        live = i < nact[0]
        return be[jnp.minimum(i, nact[0] - 1)], jnp.where(live, f, n_ff - 1)

    def w13_map(i, f, be, nact):
        e, ff = widx(i, f, be, nact)
        return layer, e, 0, ff

    def w2_map(i, f, be, nact):
        e, ff = widx(i, f, be, nact)
        return layer, e, ff, 0

    grid_spec = pltpu.PrefetchScalarGridSpec(
        num_scalar_prefetch=2,
        grid=(n_blocks, n_ff),
        in_specs=[pl.BlockSpec((MOE_BM, D), lambda i, f, be, nact: (i, 0)),
                  pl.BlockSpec((None, None, D, MOE_TF), w13_map),
                  pl.BlockSpec((None, None, D, MOE_TF), w13_map),
                  pl.BlockSpec((None, None, MOE_TF, D), w2_map)],
        out_specs=pl.BlockSpec((MOE_BM, D), lambda i, f, be, nact: (i, 0)),
    )
    return pl.pallas_call(
        _expert_kernel,
        out_shape=jax.ShapeDtypeStruct((n_rows, D), F32),
        grid_spec=grid_spec,
        compiler_params=_cparams(("arbitrary", "arbitrary")),
        name="moe_experts",
    )(block_expert, n_active, xs, w1, w3, w2)


def _combine_kernel(y0_ref, y1_ref, gate_ref, h_ref, g_ref, b_ref, o_ref):
    gate = gate_ref[...]
    f = y0_ref[...] * gate[:, 0:1] + y1_ref[...] * gate[:, 1:2]
    o_ref[...] = _layer_norm(ALPHA * h_ref[...] + f, g_ref[...], b_ref[...])


def _combine(y_slots, gates, h, g, b):
    T, D = h.shape
    tm = TM_COMBINE
    return pl.pallas_call(
        _combine_kernel,
        out_shape=jax.ShapeDtypeStruct((T, D), F32),
        grid=(T // tm,),
        in_specs=[pl.BlockSpec((tm, D), lambda i: (i, 0)),
                  pl.BlockSpec((tm, D), lambda i: (i + T // tm, 0)),
                  pl.BlockSpec((tm, LANES), lambda i: (i, 0)),
                  pl.BlockSpec((tm, D), lambda i: (i, 0)),
                  pl.BlockSpec((1, D), lambda i: (0, 0)),
                  pl.BlockSpec((1, D), lambda i: (0, 0))],
        out_specs=pl.BlockSpec((tm, D), lambda i: (i, 0)),
        compiler_params=_cparams(("parallel",)),
        name="moe_combine_ln2",
    )(y_slots, y_slots, gates, h, g.reshape(1, D), b.reshape(1, D))


def _moe_ffn(h, router_w, w1, w3, w2, layer, g, b):
    T, D = h.shape
    n_pairs = T * TOP_K
    n_blocks = n_pairs // MOE_BM + N_EXPERTS
    n_rows = n_blocks * MOE_BM
    w_pad = jnp.pad(router_w.astype(F32), ((0, 0), (0, LANES - N_EXPERTS)))
    idx_pad, gates_pad = _router(h, w_pad)
    expert_ids = idx_pad[:, :TOP_K].reshape(-1)
    onehot = (expert_ids[:, None] == jnp.arange(N_EXPERTS, dtype=jnp.int32)[None, :]).astype(jnp.int32)
    running = jnp.cumsum(onehot, axis=0)
    counts = running[-1]
    rank = jnp.sum(onehot * running, axis=1) - 1
    padded = (counts + MOE_BM - 1) // MOE_BM * MOE_BM
    ends = jnp.cumsum(padded)
    pos = (ends - padded)[expert_ids] + rank
    token_ids = jnp.arange(n_pairs, dtype=jnp.int32) // TOP_K
    default = jnp.broadcast_to((jnp.arange(n_rows, dtype=jnp.int32) % T)[:, None], (n_rows, SC_ROW_WORDS))
    row_token = _scatter_rows(default, jnp.broadcast_to(token_ids[:, None], (n_pairs, SC_ROW_WORDS)),
                              pos.astype(jnp.int32))[:, 0]
    block_start = jnp.arange(n_blocks, dtype=jnp.int32) * MOE_BM
    block_expert = jnp.minimum(jnp.searchsorted(ends, block_start, side='right'),
                               N_EXPERTS - 1).astype(jnp.int32)
    n_active = (ends[-1:] // MOE_BM).astype(jnp.int32)

    xs = _gather_rows(h, row_token)
    ys = _expert_ffn(xs, block_expert, n_active, w1, w3, w2, layer)
    slot_major = pos.astype(jnp.int32).reshape(T, TOP_K).T.reshape(-1)
    y_slots = _gather_rows(ys, slot_major)
    return _combine(y_slots, gates_pad, h, g, b)


def _relayout_w_in(w):
    sizes = (SSD_INNER, SSD_XBC, 2 * SSD_HEADS, HG_WIDTH, 2 * HG_WIDTH, HG_WIDTH, HG_WIDTH,
             2 * ML_WIDTH, ML_WIDTH, ML_WIDTH, 2 * ML_HEADS, 2 * ML_HEADS)
    offs = [0]
    for s in sizes:
        offs.append(offs[-1] + s)
    (z, xbc, dt, hq, hf, hi, hg, mqk, mv, mo, mig, mfg) = [w[:, offs[k]:offs[k + 1]] for k in range(len(sizes))]
    pad = jnp.zeros((w.shape[0], LANES - 2 * SSD_HEADS - 4 * ML_HEADS), w.dtype)
    wn = jnp.concatenate([z, mqk, hf, xbc, hq, hi, hg, mv, mo], axis=1).astype(BF16)
    wgt = jnp.concatenate([dt, mig, mfg, pad], axis=1).astype(BF16).T
    return wn, wgt


def _gate_rows(vals_by_offset):
    row = jnp.zeros((LANES,), F32)
    for off, v in vals_by_offset:
        row = lax.dynamic_update_slice(row, v.reshape(-1).astype(F32), (off,))
    return jnp.broadcast_to(row[:, None], (LANES, LANES))


def kernel(x, ln_in_g, ln_in_b, w_in, ssd_conv_w, ssd_conv_b, ssd_dt_bias, ssd_a_log, ssd_d, ssd_norm_w, hg_lb_logits, hg_norm_w, ml_conv_w, ml_conv_b, ml_ig_bias, ml_fg_bias, ml_norm_w, w_out, ln1_g, ln1_b, ln2_g, ln2_b, ffn_w1, ffn_w3, ffn_w2, moe_router, moe_w1, moe_w3, moe_w2):
    batch, seq_len, D = x.shape
    T = batch * seq_len
    f32 = lambda t: t.astype(F32)
    lb_soft = jax.nn.softmax(f32(hg_lb_logits), axis=0)
    lb_all = jnp.cumsum(lb_soft, axis=0) - lb_soft[0]

    h = _input_ln(f32(x).reshape(T, D), f32(ln_in_g), f32(ln_in_b))
    for l in range(DEPTH):
        w, wgt = _relayout_w_in(w_in[l])
        bias_cr = _gate_rows(((GATE_DT, ssd_dt_bias[l]), (GATE_IG, ml_ig_bias[l]), (GATE_FG, ml_fg_bias[l])))
        alog_cr = _gate_rows(((GATE_DT, ssd_a_log[l]),))

        proj, gates_t = _inproj(h, w, wgt)
        xbc_act = _conv_silu(proj, f32(ssd_conv_w[l]), f32(ssd_conv_b[l]), COLBLK_XBC, SSD_XBC, seq_len)
        mqk_act = _conv_silu(proj, f32(ml_conv_w[l]), f32(ml_conv_b[l]), COLBLK_MQK, 2 * ML_WIDTH, seq_len)
        y_f, y_b = _ssd_scan(xbc_act, gates_t, bias_cr[GATE_DT:GATE_DT + 2 * SSD_HEADS],
                             alog_cr[GATE_DT:GATE_DT + 2 * SSD_HEADS], batch)
        o_f, o_b = _hgrn2_scan(proj, lb_all[l], batch)
        m_f, m_b = _mlstm_scan(mqk_act, proj, gates_t, bias_cr[GATE_IG:GATE_IG + 4 * ML_HEADS], batch)
        dskip = jnp.repeat(f32(ssd_d[l]), SSD_HEAD_DIM).reshape(1, SSD_INNER)
        h = _outproj(y_f, y_b, xbc_act, proj, o_f, o_b, m_f, m_b, h, dskip,
                     f32(ssd_norm_w[l]).reshape(1, -1), f32(hg_norm_w[l]).reshape(1, -1),
                     f32(ml_norm_w[l]).reshape(1, -1), w_out[l].astype(BF16), f32(ln1_g[l]), f32(ln1_b[l]))
        if l % 2 == 0:
            j = l // 2
            h = _dense_ffn(h, ffn_w1[j].astype(BF16), ffn_w3[j].astype(BF16), ffn_w2[j].astype(BF16),
                           f32(ln2_g[l]), f32(ln2_b[l]))
        else:
            j = l // 2
            h = _moe_ffn(h, moe_router[j], f32(moe_w1), f32(moe_w3), f32(moe_w2), j,
                         f32(ln2_g[l]), f32(ln2_b[l]))
    return h.reshape(batch, seq_len, D).astype(x.dtype)
```

```python
import functools

import jax
import jax.numpy as jnp
from jax import lax
from jax.experimental import pallas as pl
from jax.experimental.pallas import tpu as pltpu
from jax.experimental.pallas import tpu_sc as plsc

F32 = jnp.float32
BF16 = jnp.bfloat16
HIGHEST = lax.Precision.HIGHEST

D_MODEL = 1024
DEPTH = 4
D_MIX = 2 * D_MODEL
SSD_HEADS = 16
SSD_INNER = 1024
SSD_HEAD_DIM = 64
SSD_GROUPS = 2
SSD_STATE = 128
SSD_XBC = 1536
SSD_CHUNK = 128
HG_HEADS = 4
HG_WIDTH = 512
HG_DK = 128
ML_HEADS = 4
ML_WIDTH = 512
ML_DH = 128
CONV_WIDTH = 5
D_FF_DENSE = 2816
N_EXPERTS = 8
TOP_K = 2
D_FF_EXPERT = 3584
LN_EPS = 1e-5
RMS_EPS = 1e-6
MASK_NEG = -1e4
ALPHA = (2.0 * DEPTH) ** 0.25
LOG2E = 1.4426950408889634

LANES = 128
SUBLANES = 8
VMEM_LIMIT = 56 * 1024 * 1024

D_PROJ_PAD = 7168
COLBLK_Z = 0
COLBLK_MQK = 1
COLBLK_XBC = 2
COLBLK_HF_FWD = 4
COLBLK_HF_BWD = 5
COLBLK_HQ = 9
COLBLK_HI = 10
COLBLK_HG = 11
COLBLK_MV = 12
COLBLK_MO = 13
GATE_DT = 0
GATE_IG = 32
GATE_FG = 40

TM_LN = 512
TM_INPROJ = 512
TM_CONV = 512
SCAN_STEP = 128
HG_SUB = 64
HG_REFBLK = 4
TM_OUT = 512
TM_FFN = 512
TF_FFN = 256
TM_ROUTER = 512
MOE_BM = 1024
MOE_TF = 512
SC_CORES = 2
SC_SUBCORES = 16
SC_WORKERS = 32
SC_SCATTER_CHUNK = 128
SC_ROW_WORDS = 128
SC_INIT_CHUNK = 512
SC_GATHER_CHUNK = 32
TM_COMBINE = 256


def _cparams(sem):
    return pltpu.CompilerParams(dimension_semantics=sem, vmem_limit_bytes=VMEM_LIMIT)


def _iota(shape, dim):
    return lax.broadcasted_iota(jnp.int32, shape, dim)


def _sigmoid(x):
    return 1.0 / (1.0 + jnp.exp(-x))


def _silu(x):
    return x * _sigmoid(x)


def _softplus(x):
    return jnp.maximum(x, 0.0) + jnp.log1p(jnp.exp(-jnp.abs(x)))


def _log_sigmoid(x):
    return -_softplus(-x)


def _layer_norm(x, g, b):
    mu = jnp.mean(x, -1, keepdims=True)
    xc = x - mu
    var = jnp.mean(xc * xc, -1, keepdims=True)
    return xc * lax.rsqrt(var + LN_EPS) * g + b


def _causal(n, reverse):
    r = _iota((n, n), 0)
    c = _iota((n, n), 1)
    return (c >= r) if reverse else (c <= r)


def _dot(a, b):
    return jnp.dot(a, b, preferred_element_type=F32)


def _dot_nt(a, b):
    return lax.dot_general(a, b, (((1,), (1,)), ((), ())), preferred_element_type=F32)


def _dot_tn(a, b):
    return lax.dot_general(a, b, (((0,), (0,)), ((), ())), preferred_element_type=F32)


def _dot_hi(a, b):
    return jnp.dot(a, b, precision=HIGHEST, preferred_element_type=F32)


def _ln_kernel(x_ref, g_ref, b_ref, o_ref):
    o_ref[...] = _layer_norm(x_ref[...], g_ref[...], b_ref[...])


def _input_ln(x, g, b):
    T, D = x.shape
    return pl.pallas_call(
        _ln_kernel,
        out_shape=jax.ShapeDtypeStruct((T, D), F32),
        grid=(T // TM_LN,),
        in_specs=[pl.BlockSpec((TM_LN, D), lambda i: (i, 0)),
                  pl.BlockSpec((1, D), lambda i: (0, 0)),
                  pl.BlockSpec((1, D), lambda i: (0, 0))],
        out_specs=pl.BlockSpec((TM_LN, D), lambda i: (i, 0)),
        compiler_params=_cparams(("parallel",)),
        name="input_ln",
    )(x, g.reshape(1, D), b.reshape(1, D))


def _inproj_kernel(h_ref, w_ref, wgt_ref, proj_ref, gt_ref):
    hb = h_ref[...].astype(BF16)
    proj_ref[...] = _dot(hb, w_ref[...])
    gt_ref[...] = _dot_nt(wgt_ref[...], hb)


def _inproj(h, w, wgt):
    T, D = h.shape
    tm = TM_INPROJ
    return pl.pallas_call(
        _inproj_kernel,
        out_shape=(jax.ShapeDtypeStruct((T, D_PROJ_PAD), F32),
                   jax.ShapeDtypeStruct((LANES, T), F32)),
        grid=(T // tm,),
        in_specs=[pl.BlockSpec((tm, D), lambda i: (i, 0)),
                  pl.BlockSpec((D, D_PROJ_PAD), lambda i: (0, 0), pipeline_mode=pl.Buffered(1)),
                  pl.BlockSpec((LANES, D), lambda i: (0, 0), pipeline_mode=pl.Buffered(1))],
        out_specs=(pl.BlockSpec((tm, D_PROJ_PAD), lambda i: (i, 0)),
                   pl.BlockSpec((LANES, tm), lambda i: (0, i))),
        compiler_params=_cparams(("parallel",)),
        name="inproj",
    )(h, w, wgt)


def _conv_kernel(prev_ref, x_ref, next_ref, w_ref, b_ref, o_ref, *, tm, tiles_per_seq):
    i = pl.program_id(0)
    pos = i % tiles_per_seq
    x = x_ref[...]
    prev = jnp.where(pos == 0, 0.0, prev_ref[...])
    nxt = jnp.where(pos == tiles_per_seq - 1, 0.0, next_ref[...])
    xe = jnp.concatenate([prev, x, nxt], axis=0)
    n = tm + 2 * SUBLANES
    acc = b_ref[...] + w_ref[2:3, :] * x
    for k in (0, 1, 3, 4):
        shifted = pltpu.roll(xe, (2 - k) % n, 0)
        acc = acc + w_ref[k:k + 1, :] * shifted[SUBLANES:SUBLANES + tm]
    o_ref[...] = _silu(acc)


def _conv_silu(proj, w, b, colblk, width, seq_len):
    T = proj.shape[0]
    tm = TM_CONV
    rb = tm // SUBLANES
    nrb = T // SUBLANES
    kern = functools.partial(_conv_kernel, tm=tm, tiles_per_seq=seq_len // tm)
    return pl.pallas_call(
        kern,
        out_shape=jax.ShapeDtypeStruct((T, width), F32),
        grid=(T // tm,),
        in_specs=[pl.BlockSpec((SUBLANES, width), lambda i: (jnp.maximum(i * rb - 1, 0), colblk)),
                  pl.BlockSpec((tm, width), lambda i: (i, colblk)),
                  pl.BlockSpec((SUBLANES, width), lambda i: (jnp.minimum((i + 1) * rb, nrb - 1), colblk)),
                  pl.BlockSpec((CONV_WIDTH, width), lambda i: (0, 0)),
                  pl.BlockSpec((1, width), lambda i: (0, 0))],
        out_specs=pl.BlockSpec((tm, width), lambda i: (i, 0)),
        compiler_params=_cparams(("parallel",)),
        name="conv_silu",
    )(proj, proj, proj, w, b.reshape(1, width))


def _cumsum_lanes(x, reverse):
    n = x.shape[1]
    r = _iota((3 * n, n), 0) % n
    c = _iota((3 * n, n), 1)
    tri3 = ((r >= c) if reverse else (r <= c)).astype(BF16)
    return _dot(_split3(x, axis=1), tri3)


def _cols_of(rows_list):
    used = sum(r.shape[0] for r in rows_list)
    pad = jnp.zeros((LANES - used, LANES), F32)
    return jnp.concatenate(list(rows_list) + [pad], axis=0).T


def _ssd_gates(gates_t, bias_cr, alog_cr, reverse):
    last = 0 if reverse else SSD_CHUNK - 1
    dt_r = _softplus(gates_t + bias_cr)
    acs_r = _cumsum_lanes(dt_r * (-jnp.exp(alog_cr)) * LOG2E, reverse)
    a_last = acs_r[:, last:last + 1]
    w_r = dt_r * jnp.exp2(a_last - acs_r)
    return acs_r - jnp.log2(dt_r), acs_r, w_r, jnp.exp2(a_last)


def _ssd_direction(x, gate_rows, acs_cols, st_ref, d, y_ref):
    reverse = d == 1
    n = SSD_CHUNK
    tri = _causal(n, reverse)
    lane_lo = _iota((n, LANES), 1) < SSD_HEAD_DIM
    acs_dt_r, _, w_r, dec = gate_rows

    for g in range(SSD_GROUPS):
        b0 = SSD_INNER + g * SSD_STATE
        c0 = SSD_INNER + SSD_GROUPS * SSD_STATE + g * SSD_STATE
        bg = x[:, b0:b0 + SSD_STATE]
        cgb = x[:, c0:c0 + SSD_STATE].astype(BF16)
        cb = _dot_nt(cgb, bg.astype(BF16))
        bg_t = bg.T
        for pair in range(4):
            p0 = (g * 4 + pair) * LANES
            xp = x[:, p0:p0 + LANES]
            ypair = None
            for hh in range(2):
                h = g * 8 + pair * 2 + hh
                c = d * SSD_HEADS + h
                xm = jnp.where(lane_lo if hh == 0 else jnp.logical_not(lane_lo), xp, 0.0).astype(BF16)
                col = jnp.broadcast_to(acs_cols[:, c:c + 1], (n, n))
                decay_dt = jnp.exp2(jnp.where(tri, col - acs_dt_r[h:h + 1, :], MASK_NEG))
                m = (cb * decay_dt).astype(BF16)
                st = st_ref[d, h]
                y = _dot(m, xm) + jnp.exp2(col) * _dot(cgb, st.astype(BF16))
                bh = (bg_t * w_r[h:h + 1, :]).astype(BF16)
                st_ref[d, h] = st * dec[h:h + 1, :] + _dot(bh, xm)
                ypair = y if ypair is None else ypair + y
            y_ref[:, p0:p0 + LANES] = ypair


def _ssd_kernel(xf_ref, xb_ref, gtf_ref, gtb_ref, bcr_ref, acr_ref, yf_ref, yb_ref, st_ref):
    @pl.when(pl.program_id(1) == 0)
    def _():
        st_ref[...] = jnp.zeros_like(st_ref)

    gf = _ssd_gates(gtf_ref[...], bcr_ref[0:SSD_HEADS, :], acr_ref[0:SSD_HEADS, :], False)
    gb = _ssd_gates(gtb_ref[...], bcr_ref[SSD_HEADS:2 * SSD_HEADS, :], acr_ref[SSD_HEADS:2 * SSD_HEADS, :], True)
    acs_cols = _cols_of([gf[1], gb[1]])
    _ssd_direction(xf_ref[...], gf, acs_cols, st_ref, 0, yf_ref)
    _ssd_direction(xb_ref[...], gb, acs_cols, st_ref, 1, yb_ref)


def _ssd_scan(xbc_act, gates_t, bias_cr, alog_cr, batch):
    T = xbc_act.shape[0]
    n = T // batch // SSD_CHUNK
    fwd = lambda b, i: b * n + i
    bwd = lambda b, i: b * n + (n - 1 - i)
    const = lambda b, i: (0, 0)
    return pl.pallas_call(
        _ssd_kernel,
        out_shape=(jax.ShapeDtypeStruct((T, SSD_INNER), F32),
                   jax.ShapeDtypeStruct((T, SSD_INNER), F32)),
        grid=(batch, n),
        in_specs=[pl.BlockSpec((SSD_CHUNK, SSD_XBC), lambda b, i: (fwd(b, i), 0)),
                  pl.BlockSpec((SSD_CHUNK, SSD_XBC), lambda b, i: (bwd(b, i), 0)),
                  pl.BlockSpec((SSD_HEADS, SSD_CHUNK), lambda b, i: (0, fwd(b, i))),
                  pl.BlockSpec((SSD_HEADS, SSD_CHUNK), lambda b, i: (1, bwd(b, i))),
                  pl.BlockSpec((2 * SSD_HEADS, LANES), const),
                  pl.BlockSpec((2 * SSD_HEADS, LANES), const)],
        out_specs=(pl.BlockSpec((SSD_CHUNK, SSD_INNER), lambda b, i: (fwd(b, i), 0)),
                   pl.BlockSpec((SSD_CHUNK, SSD_INNER), lambda b, i: (bwd(b, i), 0))),
        scratch_shapes=[pltpu.VMEM((2, SSD_HEADS, SSD_STATE, LANES), F32)],
        compiler_params=_cparams(("arbitrary", "arbitrary")),
        name="ssd_scan",
    )(xbc_act, xbc_act, gates_t, gates_t, bias_cr, alog_cr)


def _split3(x, axis=0):
    hi = x.astype(BF16)
    r1 = x - hi.astype(F32)
    mid = r1.astype(BF16)
    lo = (r1 - mid.astype(F32)).astype(BF16)
    return jnp.concatenate([hi, mid, lo], axis=axis)


def _hgrn2_cumsum_lhs(reverse):
    n = HG_SUB
    r_i = _iota((2 * n, n), 0)
    c_i = _iota((2 * n, n), 1)
    t = jnp.where(r_i < n, r_i, r_i - n)
    ref = (t // HG_REFBLK) * HG_REFBLK + (HG_REFBLK - 1 if reverse else 0)
    upto = jnp.where(r_i < n, t, ref)
    m = ((c_i >= upto) if reverse else (c_i <= upto)).astype(BF16)
    return jnp.concatenate([m, m, m], axis=1)


def _hgrn2_intra(q, f, v, lb, qmask, reverse):
    n = HG_SUB
    nblk = n // HG_REFBLK
    slab = 2 * SUBLANES
    tri = _causal(n, reverse)
    last = 0 if reverse else n - 1
    rowblk = _iota((slab, HG_DK), 0) // HG_REFBLK

    qh = _silu(q)
    key = (1.0 - lb) / (1.0 + jnp.exp(f))
    g = jnp.log2(1.0 - key)
    gg = _dot(_hgrn2_cumsum_lhs(reverse), _split3(g))
    gc = gg[:n]
    gref = gg[n:]
    qg = (qh * jnp.exp2(gc - gref)).astype(BF16)
    qs = (qh * jnp.exp2(gc)).astype(BF16)
    g_last = gc[last:last + 1, :]
    hk = gc - jnp.log2(key)
    kd = jnp.exp2(g_last - hk).astype(BF16)
    dec = jnp.exp2(g_last)
    vb = v.astype(BF16)

    o_intra = []
    for h in range(HG_HEADS):
        cols = slice(h * HG_DK, (h + 1) * HG_DK)
        hk_h = hk[:, cols]
        gc_h = gc[:, cols]
        q_ext = jnp.tile(qg[:, cols], (1, nblk)) * qmask
        k_parts = []
        for j in range(nblk):
            ref_row = j * HG_REFBLK + (HG_REFBLK - 1 if reverse else 0)
            rj = gc_h[ref_row:ref_row + 1, :]
            edge = (j * HG_REFBLK) // slab
            slabs = range(edge, n // slab) if reverse else range(0, edge + 1)
            pieces = []
            for sidx in range(n // slab):
                rows = slice(sidx * slab, (sidx + 1) * slab)
                if sidx not in slabs:
                    pieces.append(jnp.zeros((slab, HG_DK), BF16))
                    continue
                arg = rj - hk_h[rows]
                if sidx == edge:
                    jb = j - edge * (slab // HG_REFBLK)
                    live = (rowblk >= jb) if reverse else (rowblk <= jb)
                    arg = jnp.where(live, arg, MASK_NEG)
                pieces.append(jnp.exp2(arg).astype(BF16))
            k_parts.append(jnp.concatenate(pieces, axis=0))
        att = _dot_nt(q_ext, jnp.concatenate(k_parts, axis=1))
        att = jnp.where(tri, att, 0.0).astype(BF16)
        o_intra.append(_dot(att, vb[:, cols]))
    return o_intra, qs, kd, dec, vb


def _hgrn2_state(intra, st_ref, d, o_ref, rows):
    o_intra, qs, kd, dec, vb = intra
    for h in range(HG_HEADS):
        cols = slice(h * HG_DK, (h + 1) * HG_DK)
        st = st_ref[d, h]
        o_ref[rows, cols] = o_intra[h] + _dot_nt(qs[:, cols], st.astype(BF16))
        st_ref[d, h] = st * dec[:, cols] + _dot_tn(vb[:, cols], kd[:, cols])


def _hgrn2_kernel(qf_ref, ff_ref, vf_ref, qb_ref, fb_ref, vb_ref, lb_ref, qmask_ref, of_ref, ob_ref, st_ref):
    @pl.when(pl.program_id(1) == 0)
    def _():
        st_ref[...] = jnp.zeros_like(st_ref)

    lb = lb_ref[...]
    qmask = qmask_ref[...]
    refs = ((qf_ref, ff_ref, vf_ref, of_ref), (qb_ref, fb_ref, vb_ref, ob_ref))
    nsub = SCAN_STEP // HG_SUB
    units = []
    for k in range(nsub):
        units.append((0, k))
        units.append((1, nsub - 1 - k))
    pending = None
    for d, sc in units:
        rows = slice(sc * HG_SUB, (sc + 1) * HG_SUB)
        q_ref, f_ref, v_ref, o_ref = refs[d]
        intra = _hgrn2_intra(q_ref[rows, :], f_ref[rows, :], v_ref[rows, :], lb, qmask, d == 1)
        if pending is not None:
            _hgrn2_state(*pending)
        pending = (intra, st_ref, d, o_ref, rows)
    _hgrn2_state(*pending)


def _hgrn2_scan(proj, lb, batch):
    T = proj.shape[0]
    n = T // batch // SCAN_STEP
    fwd = lambda b, i: b * n + i
    bwd = lambda b, i: b * n + (n - 1 - i)
    blk = (SCAN_STEP, HG_WIDTH)
    nblk = HG_SUB // HG_REFBLK
    qmask = (jnp.arange(HG_SUB)[:, None] // HG_REFBLK == jnp.arange(nblk * HG_DK)[None, :] // HG_DK).astype(BF16)
    return pl.pallas_call(
        _hgrn2_kernel,
        out_shape=(jax.ShapeDtypeStruct((T, HG_WIDTH), F32),
                   jax.ShapeDtypeStruct((T, HG_WIDTH), F32)),
        grid=(batch, n),
        in_specs=[pl.BlockSpec(blk, lambda b, i: (fwd(b, i), COLBLK_HQ)),
                  pl.BlockSpec(blk, lambda b, i: (fwd(b, i), COLBLK_HF_FWD)),
                  pl.BlockSpec(blk, lambda b, i: (fwd(b, i), COLBLK_HI)),
                  pl.BlockSpec(blk, lambda b, i: (bwd(b, i), COLBLK_HQ)),
                  pl.BlockSpec(blk, lambda b, i: (bwd(b, i), COLBLK_HF_BWD)),
                  pl.BlockSpec(blk, lambda b, i: (bwd(b, i), COLBLK_HI)),
                  pl.BlockSpec((1, HG_WIDTH), lambda b, i: (0, 0)),
                  pl.BlockSpec((HG_SUB, nblk * HG_DK), lambda b, i: (0, 0))],
        out_specs=(pl.BlockSpec(blk, lambda b, i: (fwd(b, i), 0)),
                   pl.BlockSpec(blk, lambda b, i: (bwd(b, i), 0))),
        scratch_shapes=[pltpu.VMEM((2, HG_HEADS, HG_DK, HG_DK), F32)],
        compiler_params=_cparams(("arbitrary", "arbitrary")),
        name="hgrn2_scan",
    )(proj, proj, proj, proj, proj, proj, lb.reshape(1, HG_WIDTH), qmask)


def _mlstm_kernel(batch, qkf_ref, vf_ref, qkb_ref, vb_ref, *rest):
    gt_refs = (rest[:batch], rest[batch:2 * batch])
    bcr_ref, of_ref, ob_ref, c_ref, m_ref, mlane_ref = rest[2 * batch:]

    @pl.when(pl.program_id(0) == 0)
    def _():
        c_ref[...] = jnp.zeros_like(c_ref)
        m_ref[...] = jnp.zeros_like(m_ref)
        mlane_ref[...] = jnp.zeros_like(mlane_ref)

    n = SCAN_STEP
    ones = jnp.ones((n, ML_DH), BF16)
    io_refs = ((qkf_ref, vf_ref, of_ref), (qkb_ref, vb_ref, ob_ref))
    tris = (_causal(n, False), _causal(n, True))
    row_i = _iota((n, LANES), 0)

    gates = {}
    for d in range(2):
        reverse = d == 1
        last = 0 if reverse else n - 1
        rows = []
        for b in range(batch):
            g_r = gt_refs[d][b][...] + bcr_ref[...]
            bc = _cumsum_lanes(_log_sigmoid(g_r) * LOG2E, reverse)[SUBLANES:]
            ig = g_r[:SUBLANES] * LOG2E
            rows.append((bc, ig, ig - bc))
        tb = _cols_of([r[0] for r in rows])
        pm = _cols_of([r[2] for r in rows])
        for sh in (1, 2, 4, 8, 16, 32, 64):
            shifted = pltpu.roll(pm, (n - sh) if reverse else sh, 0)
            valid = (row_i < n - sh) if reverse else (row_i >= sh)
            pm = jnp.maximum(pm, jnp.where(valid, shifted, -jnp.inf))
        m_prev_lane = mlane_ref[d][0:1, :]
        floor = jnp.where((row_i > 0) if reverse else (row_i < n - 1), MASK_NEG * LOG2E, -jnp.inf)
        mt = jnp.maximum(jnp.maximum(tb + pm, tb + m_prev_lane), floor)
        b_last_lane = tb[last:last + 1, :]
        m_new_lane = jnp.maximum(b_last_lane + m_prev_lane, b_last_lane + pm[last:last + 1, :])
        mlane_ref[d] = jnp.broadcast_to(m_new_lane, (SUBLANES, LANES))
        bm = tb - mt
        for b in range(batch):
            bc, ig, u = rows[b]
            m_prev = m_ref[b * 2 + d]
            b_last = bc[:, last:last + 1]
            m_new = jnp.maximum(b_last + m_prev[:, 0:1], b_last + jnp.max(u, axis=1, keepdims=True))
            w_s = jnp.exp2(b_last - bc + ig - m_new)
            dec = jnp.exp2(b_last + m_prev[:, 0:1] - m_new)
            m_ref[b * 2 + d] = jnp.broadcast_to(m_new, (SUBLANES, LANES))
            gates[b, d] = (u, bm, mt, m_prev, w_s, dec)

    units = [(b, d, h) for h in range(ML_HEADS) for b in range(batch) for d in range(2)]
    for b, d, h in units:
        u, bm, mt, m_prev, w_s, dec = gates[b, d]
        qk_ref, v_ref, o_ref = io_refs[d]
        r = d * ML_HEADS + h
        c = b * SUBLANES + r
        qf = qk_ref[b, :, h * ML_DH:(h + 1) * ML_DH] * (ML_DH ** -0.5)
        k = qk_ref[b, :, ML_WIDTH + h * ML_DH:ML_WIDTH + (h + 1) * ML_DH]
        ve = jnp.concatenate([v_ref[b, :, h * ML_DH:(h + 1) * ML_DH].astype(BF16), ones], axis=1)
        b_minus_m = jnp.broadcast_to(bm[:, c:c + 1], (n, n))
        m_t = jnp.broadcast_to(mt[:, c:c + 1], (n, n))
        w_intra = jnp.exp2(jnp.where(tris[d], b_minus_m + u[r:r + 1, :], MASK_NEG))
        w_inter = jnp.exp2(b_minus_m + m_prev[r:r + 1, 0:1])
        s = (_dot_nt(qf.astype(BF16), k.astype(BF16)) * w_intra).astype(BF16)
        cst = c_ref[b, d, h]
        numden = _dot(jnp.concatenate([s, (qf * w_inter).astype(BF16)], axis=1),
                      jnp.concatenate([ve, cst.astype(BF16)], axis=0))
        den = numden[:, ML_DH:]
        o_ref[b, :, h * ML_DH:(h + 1) * ML_DH] = numden[:, :ML_DH] / jnp.maximum(jnp.abs(den), jnp.exp2(-m_t))
        kw_t = (k.T * w_s[r:r + 1, :]).astype(BF16)
        c_ref[b, d, h] = dec[r:r + 1, :] * cst + _dot(kw_t, ve)


def _mlstm_scan(mqk_act, proj, gates_t, bias_cr, batch):
    T = proj.shape[0]
    seq = T // batch
    n = seq // SCAN_STEP
    gt_rowblk = GATE_IG // (2 * SUBLANES)
    fwd3 = lambda cb: (lambda i: (0, i, cb))
    bwd3 = lambda cb: (lambda i: (0, n - 1 - i, cb))
    gt_fwd = [pl.BlockSpec((2 * SUBLANES, SCAN_STEP), functools.partial(lambda b, i: (gt_rowblk, b * n + i), b))
              for b in range(batch)]
    gt_bwd = [pl.BlockSpec((2 * SUBLANES, SCAN_STEP),
                           functools.partial(lambda b, i: (gt_rowblk, b * n + n - 1 - i), b))
              for b in range(batch)]
    qk3 = mqk_act.reshape(batch, seq, 2 * ML_WIDTH)
    proj3 = proj.reshape(batch, seq, D_PROJ_PAD)
    o_f, o_b = pl.pallas_call(
        functools.partial(_mlstm_kernel, batch),
        out_shape=(jax.ShapeDtypeStruct((batch, seq, ML_WIDTH), F32),
                   jax.ShapeDtypeStruct((batch, seq, ML_WIDTH), F32)),
        grid=(n,),
        in_specs=[pl.BlockSpec((batch, SCAN_STEP, 2 * ML_WIDTH), fwd3(0)),
                  pl.BlockSpec((batch, SCAN_STEP, ML_WIDTH), fwd3(COLBLK_MV)),
                  pl.BlockSpec((batch, SCAN_STEP, 2 * ML_WIDTH), bwd3(0)),
                  pl.BlockSpec((batch, SCAN_STEP, ML_WIDTH), bwd3(COLBLK_MV))]
                 + gt_fwd + gt_bwd
                 + [pl.BlockSpec((2 * SUBLANES, LANES), lambda i: (0, 0))],
        out_specs=(pl.BlockSpec((batch, SCAN_STEP, ML_WIDTH), fwd3(0)),
                   pl.BlockSpec((batch, SCAN_STEP, ML_WIDTH), bwd3(0))),
        scratch_shapes=[pltpu.VMEM((batch, 2, ML_HEADS, ML_DH, 2 * ML_DH), F32),
                        pltpu.VMEM((batch * 2, SUBLANES, LANES), F32),
                        pltpu.VMEM((2, SUBLANES, LANES), F32)],
        compiler_params=_cparams(("arbitrary",)),
        name="mlstm_scan",
    )(qk3, proj3, qk3, proj3, *([gates_t] * (2 * batch)), bias_cr)
    return o_f.reshape(T, ML_WIDTH), o_b.reshape(T, ML_WIDTH)


def _outproj_kernel(yf_ref, yb_ref, xs_ref, z_ref, of_ref, ob_ref, hg_ref, mf_ref, mb_ref, mo_ref,
                    h_ref, dskip_ref, ssdw_ref, hgw_ref, mlw_ref, wout_ref, g_ref, b_ref, o_ref):
    y = (yf_ref[...] + yb_ref[...] + xs_ref[...] * dskip_ref[...]) * _silu(z_ref[...])
    gw = SSD_INNER // SSD_GROUPS
    acc = None
    for g in range(SSD_GROUPS):
        seg = y[:, g * gw:(g + 1) * gw]
        seg = seg * lax.rsqrt(jnp.mean(seg * seg, -1, keepdims=True) + RMS_EPS) * ssdw_ref[:, g * gw:(g + 1) * gw]
        part = _dot(seg.astype(BF16), wout_ref[g * gw:(g + 1) * gw, :])
        acc = part if acc is None else acc + part
    o = of_ref[...] + ob_ref[...]
    hgate = _silu(hg_ref[...])
    m = mf_ref[...] + mb_ref[...]
    mgate = _sigmoid(mo_ref[...])
    for h in range(HG_HEADS):
        cols = slice(h * HG_DK, (h + 1) * HG_DK)
        seg = o[:, cols]
        seg = seg * lax.rsqrt(jnp.mean(seg * seg, -1, keepdims=True) + RMS_EPS) * hgw_ref[:, cols] * hgate[:, cols]
        r0 = SSD_INNER + h * HG_DK
        acc = acc + _dot(seg.astype(BF16), wout_ref[r0:r0 + HG_DK, :])
    for h in range(ML_HEADS):
        cols = slice(h * ML_DH, (h + 1) * ML_DH)
        seg = m[:, cols]
        seg = seg - jnp.mean(seg, -1, keepdims=True)
        seg = seg * lax.rsqrt(jnp.mean(seg * seg, -1, keepdims=True) + LN_EPS) * mlw_ref[:, cols] * mgate[:, cols]
        r0 = SSD_INNER + HG_WIDTH + h * ML_DH
        acc = acc + _dot(seg.astype(BF16), wout_ref[r0:r0 + ML_DH, :])
    o_ref[...] = _layer_norm(ALPHA * h_ref[...] + acc, g_ref[...], b_ref[...])


def _outproj(y_f, y_b, xbc_act, proj, o_f, o_b, m_f, m_b, h, dskip, ssd_w, hg_w, ml_w, w_out, g, b):
    T, D = h.shape
    tm = TM_OUT
    row = lambda width, cb: pl.BlockSpec((tm, width), lambda i: (i, cb))
    par = lambda width: pl.BlockSpec((1, width), lambda i: (0, 0))
    return pl.pallas_call(
        _outproj_kernel,
        out_shape=jax.ShapeDtypeStruct((T, D), F32),
        grid=(T // tm,),
        in_specs=[row(SSD_INNER, 0), row(SSD_INNER, 0), row(SSD_INNER, 0), row(SSD_INNER, COLBLK_Z),
                  row(HG_WIDTH, 0), row(HG_WIDTH, 0), row(HG_WIDTH, COLBLK_HG),
                  row(ML_WIDTH, 0), row(ML_WIDTH, 0), row(ML_WIDTH, COLBLK_MO),
                  row(D, 0), par(SSD_INNER), par(SSD_INNER), par(HG_WIDTH), par(ML_WIDTH),
                  pl.BlockSpec((D_MIX, D), lambda i: (0, 0), pipeline_mode=pl.Buffered(1)),
                  par(D), par(D)],
        out_specs=pl.BlockSpec((tm, D), lambda i: (i, 0)),
        compiler_params=_cparams(("parallel",)),
        name="outproj_ln1",
    )(y_f, y_b, xbc_act, proj, o_f, o_b, proj, m_f, m_b, proj, h, dskip, ssd_w, hg_w, ml_w, w_out,
      g.reshape(1, D), b.reshape(1, D))


def _ffn_kernel(h_ref, w1_ref, w3_ref, w2_ref, g_ref, b_ref, o_ref, acc_ref):
    h = h_ref[...]
    hb = h.astype(BF16)
    for c in range(D_FF_DENSE // TF_FFN):
        cols = slice(c * TF_FFN, (c + 1) * TF_FFN)
        a = _dot(hb, w1_ref[:, cols])
        u = _dot(hb, w3_ref[:, cols])
        part = _dot((_silu(a) * u).astype(BF16), w2_ref[cols, :])
        if c == 0:
            acc_ref[...] = part
        else:
            acc_ref[...] += part
    o_ref[...] = _layer_norm(ALPHA * h + acc_ref[...], g_ref[...], b_ref[...])


def _dense_ffn(h, w1, w3, w2, g, b):
    T, D = h.shape
    tm = TM_FFN
    resident = lambda shape: pl.BlockSpec(shape, lambda i: (0, 0), pipeline_mode=pl.Buffered(1))
    return pl.pallas_call(
        _ffn_kernel,
        out_shape=jax.ShapeDtypeStruct((T, D), F32),
        grid=(T // tm,),
        in_specs=[pl.BlockSpec((tm, D), lambda i: (i, 0)),
                  resident((D, D_FF_DENSE)), resident((D, D_FF_DENSE)), resident((D_FF_DENSE, D)),
                  pl.BlockSpec((1, D), lambda i: (0, 0)), pl.BlockSpec((1, D), lambda i: (0, 0))],
        out_specs=pl.BlockSpec((tm, D), lambda i: (i, 0)),
        scratch_shapes=[pltpu.VMEM((tm, D), F32)],
        compiler_params=_cparams(("parallel",)),
        name="dense_ffn_ln2",
    )(h, w1, w3, w2, g.reshape(1, D), b.reshape(1, D))


def _router_kernel(h_ref, w_ref, idx_ref, gate_ref):
    logits = _dot_hi(h_ref[...], w_ref[...])
    lane = _iota(logits.shape, 1)
    lane_f = lane.astype(F32)
    x = jnp.where(lane < N_EXPERTS, logits, -jnp.inf)
    m1 = jnp.max(x, axis=1, keepdims=True)
    i1 = jnp.min(jnp.where(x == m1, lane_f, float(LANES)), axis=1, keepdims=True)
    x2 = jnp.where(lane_f == i1, -jnp.inf, x)
    m2 = jnp.max(x2, axis=1, keepdims=True)
    i2 = jnp.min(jnp.where(x2 == m2, lane_f, float(LANES)), axis=1, keepdims=True)
    e2 = jnp.exp(m2 - m1)
    g1 = 1.0 / (1.0 + e2)
    g2 = e2 / (1.0 + e2)
    idx_ref[...] = jnp.where(lane == 0, i1, jnp.where(lane == 1, i2, 0.0)).astype(jnp.int32)
    gate_ref[...] = jnp.where(lane == 0, g1, jnp.where(lane == 1, g2, 0.0))


def _router(h, w_pad):
    T, D = h.shape
    tm = TM_ROUTER
    return pl.pallas_call(
        _router_kernel,
        out_shape=(jax.ShapeDtypeStruct((T, LANES), jnp.int32),
                   jax.ShapeDtypeStruct((T, LANES), F32)),
        grid=(T // tm,),
        in_specs=[pl.BlockSpec((tm, D), lambda i: (i, 0)),
                  pl.BlockSpec((D, LANES), lambda i: (0, 0))],
        out_specs=(pl.BlockSpec((tm, LANES), lambda i: (i, 0)),
                   pl.BlockSpec((tm, LANES), lambda i: (i, 0))),
        compiler_params=_cparams(("parallel",)),
        name="moe_router",
    )(h, w_pad)


def _gather_rows(src, idx):
    n = idx.shape[0]
    D = src.shape[1]
    per_worker = n // SC_WORKERS
    n_chunks = per_worker // SC_GATHER_CHUNK
    assert n_chunks * SC_GATHER_CHUNK * SC_WORKERS == n and n_chunks % 2 == 0
    mesh = plsc.VectorSubcoreMesh(core_axis_name="c", subcore_axis_name="s")

    @functools.partial(
        pl.kernel, mesh=mesh,
        out_type=jax.ShapeDtypeStruct((n, D), src.dtype),
        scratch_types=[pltpu.VMEM((2, SC_GATHER_CHUNK), jnp.int32),
                       pltpu.VMEM((2, SC_GATHER_CHUNK, D), src.dtype),
                       pltpu.SemaphoreType.DMA((2,))],
        name="moe_row_gather",
    )
    def gather(src_hbm, idx_hbm, out_hbm, idx_v, rows_v, sems):
        worker = lax.axis_index("s") * SC_CORES + lax.axis_index("c")
        base = worker * per_worker

        def gather_copy(slot):
            return pltpu.make_async_copy(src_hbm.at[idx_v.at[slot]], rows_v.at[slot], sems.at[slot])

        def start(j, slot):
            pltpu.sync_copy(idx_hbm.at[pl.ds(base + j * SC_GATHER_CHUNK, SC_GATHER_CHUNK)], idx_v.at[slot])
            gather_copy(slot).start()

        def finish(j, slot):
            gather_copy(slot).wait()
            pltpu.sync_copy(rows_v.at[slot], out_hbm.at[pl.ds(base + j * SC_GATHER_CHUNK, SC_GATHER_CHUNK)])

        start(0, 0)

        @pl.loop(0, n_chunks, step=2)
        def _(j):
            start(j + 1, 1)
            finish(j, 0)

            @pl.when(j + 2 < n_chunks)
            def _():
                start(j + 2, 0)

            finish(j + 1, 1)

    return gather(src, idx)


def _scatter_rows(init, vals, idx):
    n_out, W = init.shape
    n = idx.shape[0]
    out_per = n_out // SC_SUBCORES
    per = n // SC_SUBCORES
    n_chunks = per // SC_SCATTER_CHUNK
    init_chunks = out_per // SC_INIT_CHUNK
    assert init_chunks * SC_INIT_CHUNK * SC_SUBCORES == n_out and n_chunks * SC_SCATTER_CHUNK * SC_SUBCORES == n
    mesh = plsc.VectorSubcoreMesh(core_axis_name="c", subcore_axis_name="s", num_cores=1)

    @functools.partial(
        pl.kernel, mesh=mesh,
        out_type=jax.ShapeDtypeStruct((n_out, W), init.dtype),
        scratch_types=[pltpu.VMEM((SC_INIT_CHUNK, W), init.dtype),
                       pltpu.VMEM((SC_SCATTER_CHUNK,), jnp.int32),
                       pltpu.VMEM((SC_SCATTER_CHUNK, W), init.dtype)],
        name="moe_row_scatter",
    )
    def scatter(init_hbm, vals_hbm, idx_hbm, out_hbm, init_v, idx_v, vals_v):
        worker = lax.axis_index("s")

        @pl.loop(0, init_chunks)
        def _(j):
            off = worker * out_per + j * SC_INIT_CHUNK
            pltpu.sync_copy(init_hbm.at[pl.ds(off, SC_INIT_CHUNK)], init_v)
            pltpu.sync_copy(init_v, out_hbm.at[pl.ds(off, SC_INIT_CHUNK)])

        plsc.subcore_barrier()

        @pl.loop(0, n_chunks)
        def _(j):
            off = worker * per + j * SC_SCATTER_CHUNK
            pltpu.sync_copy(idx_hbm.at[pl.ds(off, SC_SCATTER_CHUNK)], idx_v)
            pltpu.sync_copy(vals_hbm.at[pl.ds(off, SC_SCATTER_CHUNK)], vals_v)
            pltpu.sync_copy(vals_v, out_hbm.at[idx_v])

    return scatter(init, vals, idx)


def _expert_kernel(be_ref, nact_ref, x_ref, w1_ref, w3_ref, w2_ref, o_ref):
    i = pl.program_id(0)
    f = pl.program_id(1)

    @pl.when(f == 0)
    def _():
        o_ref[...] = jnp.zeros_like(o_ref)

    @pl.when(i < nact_ref[0])
    def _():
        xb = x_ref[...].astype(BF16)
        a = _dot(xb, w1_ref[...].astype(BF16))
        u = _dot(xb, w3_ref[...].astype(BF16))
        o_ref[...] += _dot((_silu(a) * u).astype(BF16), w2_ref[...].astype(BF16))


def _expert_ffn(xs, block_expert, n_active, w1, w3, w2, layer):
    n_rows, D = xs.shape
    n_blocks = n_rows // MOE_BM
    n_ff = D_FF_EXPERT // MOE_TF

    def widx(i, f, be, nact):
        live = i < nact[0]
        return be[jnp.minimum(i, nact[0] - 1)], jnp.where(live, f, n_ff - 1)

    def w13_map(i, f, be, nact):
        e, ff = widx(i, f, be, nact)
        return layer, e, 0, ff

    def w2_map(i, f, be, nact):
        e, ff = widx(i, f, be, nact)
        return layer, e, ff, 0

    grid_spec = pltpu.PrefetchScalarGridSpec(
        num_scalar_prefetch=2,
        grid=(n_blocks, n_ff),
        in_specs=[pl.BlockSpec((MOE_BM, D), lambda i, f, be, nact: (i, 0)),
                  pl.BlockSpec((None, None, D, MOE_TF), w13_map),
                  pl.BlockSpec((None, None, D, MOE_TF), w13_map),
                  pl.BlockSpec((None, None, MOE_TF, D), w2_map)],
        out_specs=pl.BlockSpec((MOE_BM, D), lambda i, f, be, nact: (i, 0)),
    )
    return pl.pallas_call(
        _expert_kernel,
        out_shape=jax.ShapeDtypeStruct((n_rows, D), F32),
        grid_spec=grid_spec,
        compiler_params=_cparams(("arbitrary", "arbitrary")),
        name="moe_experts",
    )(block_expert, n_active, xs, w1, w3, w2)


def _combine_kernel(y0_ref, y1_ref, gate_ref, h_ref, g_ref, b_ref, o_ref):
    gate = gate_ref[...]
    f = y0_ref[...] * gate[:, 0:1] + y1_ref[...] * gate[:, 1:2]
    o_ref[...] = _layer_norm(ALPHA * h_ref[...] + f, g_ref[...], b_ref[...])


def _combine(y_slots, gates, h, g, b):
    T, D = h.shape
    tm = TM_COMBINE
    return pl.pallas_call(
        _combine_kernel,
        out_shape=jax.ShapeDtypeStruct((T, D), F32),
        grid=(T // tm,),
        in_specs=[pl.BlockSpec((tm, D), lambda i: (i, 0)),
                  pl.BlockSpec((tm, D), lambda i: (i + T // tm, 0)),
                  pl.BlockSpec((tm, LANES), lambda i: (i, 0)),
                  pl.BlockSpec((tm, D), lambda i: (i, 0)),
                  pl.BlockSpec((1, D), lambda i: (0, 0)),
                  pl.BlockSpec((1, D), lambda i: (0, 0))],
        out_specs=pl.BlockSpec((tm, D), lambda i: (i, 0)),
        compiler_params=_cparams(("parallel",)),
        name="moe_combine_ln2",
    )(y_slots, y_slots, gates, h, g.reshape(1, D), b.reshape(1, D))


def _moe_ffn(h, router_w, w1, w3, w2, layer, g, b):
    T, D = h.shape
    n_pairs = T * TOP_K
    n_blocks = n_pairs // MOE_BM + N_EXPERTS
    n_rows = n_blocks * MOE_BM
    w_pad = jnp.pad(router_w.astype(F32), ((0, 0), (0, LANES - N_EXPERTS)))
    idx_pad, gates_pad = _router(h, w_pad)
    expert_ids = idx_pad[:, :TOP_K].reshape(-1)
    onehot = (expert_ids[:, None] == jnp.arange(N_EXPERTS, dtype=jnp.int32)[None, :]).astype(jnp.int32)
    running = jnp.cumsum(onehot, axis=0)
    counts = running[-1]
    rank = jnp.sum(onehot * running, axis=1) - 1
    padded = (counts + MOE_BM - 1) // MOE_BM * MOE_BM
    ends = jnp.cumsum(padded)
    pos = (ends - padded)[expert_ids] + rank
    token_ids = jnp.arange(n_pairs, dtype=jnp.int32) // TOP_K
    default = jnp.broadcast_to((jnp.arange(n_rows, dtype=jnp.int32) % T)[:, None], (n_rows, SC_ROW_WORDS))
    row_token = _scatter_rows(default, jnp.broadcast_to(token_ids[:, None], (n_pairs, SC_ROW_WORDS)),
                              pos.astype(jnp.int32))[:, 0]
    block_start = jnp.arange(n_blocks, dtype=jnp.int32) * MOE_BM
    block_expert = jnp.minimum(jnp.searchsorted(ends, block_start, side='right'),
                               N_EXPERTS - 1).astype(jnp.int32)
    n_active = (ends[-1:] // MOE_BM).astype(jnp.int32)

    xs = _gather_rows(h, row_token)
    ys = _expert_ffn(xs, block_expert, n_active, w1, w3, w2, layer)
    slot_major = pos.astype(jnp.int32).reshape(T, TOP_K).T.reshape(-1)
    y_slots = _gather_rows(ys, slot_major)
    return _combine(y_slots, gates_pad, h, g, b)


def _relayout_w_in(w):
    sizes = (SSD_INNER, SSD_XBC, 2 * SSD_HEADS, HG_WIDTH, 2 * HG_WIDTH, HG_WIDTH, HG_WIDTH,
             2 * ML_WIDTH, ML_WIDTH, ML_WIDTH, 2 * ML_HEADS, 2 * ML_HEADS)
    offs = [0]
    for s in sizes:
        offs.append(offs[-1] + s)
    (z, xbc, dt, hq, hf, hi, hg, mqk, mv, mo, mig, mfg) = [w[:, offs[k]:offs[k + 1]] for k in range(len(sizes))]
    pad = jnp.zeros((w.shape[0], LANES - 2 * SSD_HEADS - 4 * ML_HEADS), w.dtype)
    wn = jnp.concatenate([z, mqk, hf, xbc, hq, hi, hg, mv, mo], axis=1).astype(BF16)
    wgt = jnp.concatenate([dt, mig, mfg, pad], axis=1).astype(BF16).T
    return wn, wgt


def _gate_rows(vals_by_offset):
    row = jnp.zeros((LANES,), F32)
    for off, v in vals_by_offset:
        row = lax.dynamic_update_slice(row, v.reshape(-1).astype(F32), (off,))
    return jnp.broadcast_to(row[:, None], (LANES, LANES))


def kernel(x, ln_in_g, ln_in_b, w_in, ssd_conv_w, ssd_conv_b, ssd_dt_bias, ssd_a_log, ssd_d, ssd_norm_w, hg_lb_logits, hg_norm_w, ml_conv_w, ml_conv_b, ml_ig_bias, ml_fg_bias, ml_norm_w, w_out, ln1_g, ln1_b, ln2_g, ln2_b, ffn_w1, ffn_w3, ffn_w2, moe_router, moe_w1, moe_w3, moe_w2):
    batch, seq_len, D = x.shape
    T = batch * seq_len
    f32 = lambda t: t.astype(F32)
    lb_soft = jax.nn.softmax(f32(hg_lb_logits), axis=0)
    lb_all = jnp.cumsum(lb_soft, axis=0) - lb_soft[0]

    h = _input_ln(f32(x).reshape(T, D), f32(ln_in_g), f32(ln_in_b))
    for l in range(DEPTH):
        w, wgt = _relayout_w_in(w_in[l])
        bias_cr = _gate_rows(((GATE_DT, ssd_dt_bias[l]), (GATE_IG, ml_ig_bias[l]), (GATE_FG, ml_fg_bias[l])))
        alog_cr = _gate_rows(((GATE_DT, ssd_a_log[l]),))

        proj, gates_t = _inproj(h, w, wgt)
        xbc_act = _conv_silu(proj, f32(ssd_conv_w[l]), f32(ssd_conv_b[l]), COLBLK_XBC, SSD_XBC, seq_len)
        mqk_act = _conv_silu(proj, f32(ml_conv_w[l]), f32(ml_conv_b[l]), COLBLK_MQK, 2 * ML_WIDTH, seq_len)
        y_f, y_b = _ssd_scan(xbc_act, gates_t, bias_cr[GATE_DT:GATE_DT + 2 * SSD_HEADS],
                             alog_cr[GATE_DT:GATE_DT + 2 * SSD_HEADS], batch)
        o_f, o_b = _hgrn2_scan(proj, lb_all[l], batch)
        m_f, m_b = _mlstm_scan(mqk_act, proj, gates_t, bias_cr[GATE_IG:GATE_IG + 4 * ML_HEADS], batch)
        dskip = jnp.repeat(f32(ssd_d[l]), SSD_HEAD_DIM).reshape(1, SSD_INNER)
        h = _outproj(y_f, y_b, xbc_act, proj, o_f, o_b, m_f, m_b, h, dskip,
                     f32(ssd_norm_w[l]).reshape(1, -1), f32(hg_norm_w[l]).reshape(1, -1),
                     f32(ml_norm_w[l]).reshape(1, -1), w_out[l].astype(BF16), f32(ln1_g[l]), f32(ln1_b[l]))
        if l % 2 == 0:
            j = l // 2
            h = _dense_ffn(h, ffn_w1[j].astype(BF16), ffn_w3[j].astype(BF16), ffn_w2[j].astype(BF16),
                           f32(ln2_g[l]), f32(ln2_b[l]))
        else:
            j = l // 2
            h = _moe_ffn(h, moe_router[j], f32(moe_w1), f32(moe_w3), f32(moe_w2), j,
                         f32(ln2_g[l]), f32(ln2_b[l]))
    return h.reshape(batch, seq_len, D).astype(x.dtype)
```

```python
import functools

import jax
import jax.numpy as jnp
from jax import lax
from jax.experimental import pallas as pl
from jax.experimental.pallas import tpu as pltpu
from jax.experimental.pallas import tpu_sc as plsc

F32 = jnp.float32
BF16 = jnp.bfloat16
HIGHEST = lax.Precision.HIGHEST

D_MODEL = 1024
DEPTH = 4
D_MIX = 2 * D_MODEL
SSD_HEADS = 16
SSD_INNER = 1024
SSD_HEAD_DIM = 64
SSD_GROUPS = 2
SSD_STATE = 128
SSD_XBC = 1536
SSD_CHUNK = 128
HG_HEADS = 4
HG_WIDTH = 512
HG_DK = 128
ML_HEADS = 4
ML_WIDTH = 512
ML_DH = 128
CONV_WIDTH = 5
D_FF_DENSE = 2816
N_EXPERTS = 8
TOP_K = 2
D_FF_EXPERT = 3584
LN_EPS = 1e-5
RMS_EPS = 1e-6
MASK_NEG = -1e4
ALPHA = (2.0 * DEPTH) ** 0.25
LOG2E = 1.4426950408889634

LANES = 128
SUBLANES = 8
VMEM_LIMIT = 56 * 1024 * 1024

D_PROJ_PAD = 7168
COLBLK_Z = 0
COLBLK_MQK = 1
COLBLK_XBC = 2
COLBLK_HF_FWD = 4
COLBLK_HF_BWD = 5
COLBLK_HQ = 9
COLBLK_HI = 10
COLBLK_HG = 11
COLBLK_MV = 12
COLBLK_MO = 13
GATE_DT = 0
GATE_IG = 32
GATE_FG = 40

TM_LN = 512
TM_INPROJ = 512
TM_CONV = 512
SCAN_STEP = 128
HG_SUB = 64
HG_REFBLK = 4
TM_OUT = 512
TM_FFN = 512
TF_FFN = 256
TM_ROUTER = 512
MOE_BM = 1024
MOE_TF = 512
SC_CORES = 2
SC_SUBCORES = 16
SC_WORKERS = 32
SC_SCATTER_CHUNK = 128
SC_ROW_WORDS = 128
SC_INIT_CHUNK = 512
SC_GATHER_CHUNK = 32
TM_COMBINE = 256


def _cparams(sem):
    return pltpu.CompilerParams(dimension_semantics=sem, vmem_limit_bytes=VMEM_LIMIT)


def _iota(shape, dim):
    return lax.broadcasted_iota(jnp.int32, shape, dim)


def _sigmoid(x):
    return 1.0 / (1.0 + jnp.exp(-x))


def _silu(x):
    return x * _sigmoid(x)


def _softplus(x):
    return jnp.maximum(x, 0.0) + jnp.log1p(jnp.exp(-jnp.abs(x)))


def _log_sigmoid(x):
    return -_softplus(-x)


def _layer_norm(x, g, b):
    mu = jnp.mean(x, -1, keepdims=True)
    xc = x - mu
    var = jnp.mean(xc * xc, -1, keepdims=True)
    return xc * lax.rsqrt(var + LN_EPS) * g + b


def _causal(n, reverse):
    r = _iota((n, n), 0)
    c = _iota((n, n), 1)
    return (c >= r) if reverse else (c <= r)


def _dot(a, b):
    return jnp.dot(a, b, preferred_element_type=F32)


def _dot_nt(a, b):
    return lax.dot_general(a, b, (((1,), (1,)), ((), ())), preferred_element_type=F32)


def _dot_tn(a, b):
    return lax.dot_general(a, b, (((0,), (0,)), ((), ())), preferred_element_type=F32)


def _dot_hi(a, b):
    return jnp.dot(a, b, precision=HIGHEST, preferred_element_type=F32)


def _ln_kernel(x_ref, g_ref, b_ref, o_ref):
    o_ref[...] = _layer_norm(x_ref[...], g_ref[...], b_ref[...])


def _input_ln(x, g, b):
    T, D = x.shape
    return pl.pallas_call(
        _ln_kernel,
        out_shape=jax.ShapeDtypeStruct((T, D), F32),
        grid=(T // TM_LN,),
        in_specs=[pl.BlockSpec((TM_LN, D), lambda i: (i, 0)),
                  pl.BlockSpec((1, D), lambda i: (0, 0)),
                  pl.BlockSpec((1, D), lambda i: (0, 0))],
        out_specs=pl.BlockSpec((TM_LN, D), lambda i: (i, 0)),
        compiler_params=_cparams(("parallel",)),
        name="input_ln",
    )(x, g.reshape(1, D), b.reshape(1, D))


def _inproj_kernel(h_ref, w_ref, wgt_ref, proj_ref, gt_ref):
    hb = h_ref[...].astype(BF16)
    proj_ref[...] = _dot(hb, w_ref[...])
    gt_ref[...] = _dot_nt(wgt_ref[...], hb)


def _inproj(h, w, wgt):
    T, D = h.shape
    tm = TM_INPROJ
    return pl.pallas_call(
        _inproj_kernel,
        out_shape=(jax.ShapeDtypeStruct((T, D_PROJ_PAD), F32),
                   jax.ShapeDtypeStruct((LANES, T), F32)),
        grid=(T // tm,),
        in_specs=[pl.BlockSpec((tm, D), lambda i: (i, 0)),
                  pl.BlockSpec((D, D_PROJ_PAD), lambda i: (0, 0), pipeline_mode=pl.Buffered(1)),
                  pl.BlockSpec((LANES, D), lambda i: (0, 0), pipeline_mode=pl.Buffered(1))],
        out_specs=(pl.BlockSpec((tm, D_PROJ_PAD), lambda i: (i, 0)),
                   pl.BlockSpec((LANES, tm), lambda i: (0, i))),
        compiler_params=_cparams(("parallel",)),
        name="inproj",
    )(h, w, wgt)


def _conv_kernel(prev_ref, x_ref, next_ref, w_ref, b_ref, o_ref, *, tm, tiles_per_seq):
    i = pl.program_id(0)
    pos = i % tiles_per_seq
    x = x_ref[...]
    prev = jnp.where(pos == 0, 0.0, prev_ref[...])
    nxt = jnp.where(pos == tiles_per_seq - 1, 0.0, next_ref[...])
    xe = jnp.concatenate([prev, x, nxt], axis=0)
    n = tm + 2 * SUBLANES
    acc = b_ref[...] + w_ref[2:3, :] * x
    for k in (0, 1, 3, 4):
        shifted = pltpu.roll(xe, (2 - k) % n, 0)
        acc = acc + w_ref[k:k + 1, :] * shifted[SUBLANES:SUBLANES + tm]
    o_ref[...] = _silu(acc)


def _conv_silu(proj, w, b, colblk, width, seq_len):
    T = proj.shape[0]
    tm = TM_CONV
    rb = tm // SUBLANES
    nrb = T // SUBLANES
    kern = functools.partial(_conv_kernel, tm=tm, tiles_per_seq=seq_len // tm)
    return pl.pallas_call(
        kern,
        out_shape=jax.ShapeDtypeStruct((T, width), F32),
        grid=(T // tm,),
        in_specs=[pl.BlockSpec((SUBLANES, width), lambda i: (jnp.maximum(i * rb - 1, 0), colblk)),
                  pl.BlockSpec((tm, width), lambda i: (i, colblk)),
                  pl.BlockSpec((SUBLANES, width), lambda i: (jnp.minimum((i + 1) * rb, nrb - 1), colblk)),
                  pl.BlockSpec((CONV_WIDTH, width), lambda i: (0, 0)),
                  pl.BlockSpec((1, width), lambda i: (0, 0))],
        out_specs=pl.BlockSpec((tm, width), lambda i: (i, 0)),
        compiler_params=_cparams(("parallel",)),
        name="conv_silu",
    )(proj, proj, proj, w, b.reshape(1, width))


def _cumsum_lanes(x, reverse):
    n = x.shape[1]
    r = _iota((3 * n, n), 0) % n
    c = _iota((3 * n, n), 1)
    tri3 = ((r >= c) if reverse else (r <= c)).astype(BF16)
    return _dot(_split3(x, axis=1), tri3)


def _cols_of(rows_list):
    used = sum(r.shape[0] for r in rows_list)
    pad = jnp.zeros((LANES - used, LANES), F32)
    return jnp.concatenate(list(rows_list) + [pad], axis=0).T


def _ssd_gates(gates_t, bias_cr, alog_cr, reverse):
    last = 0 if reverse else SSD_CHUNK - 1
    dt_r = _softplus(gates_t + bias_cr)
    acs_r = _cumsum_lanes(dt_r * (-jnp.exp(alog_cr)) * LOG2E, reverse)
    a_last = acs_r[:, last:last + 1]
    w_r = dt_r * jnp.exp2(a_last - acs_r)
    return acs_r - jnp.log2(dt_r), acs_r, w_r, jnp.exp2(a_last)


def _ssd_direction(x, gate_rows, acs_cols, st_ref, d, y_ref):
    reverse = d == 1
    n = SSD_CHUNK
    tri = _causal(n, reverse)
    lane_lo = _iota((n, LANES), 1) < SSD_HEAD_DIM
    acs_dt_r, _, w_r, dec = gate_rows

    for g in range(SSD_GROUPS):
        b0 = SSD_INNER + g * SSD_STATE
        c0 = SSD_INNER + SSD_GROUPS * SSD_STATE + g * SSD_STATE
        bg = x[:, b0:b0 + SSD_STATE]
        cgb = x[:, c0:c0 + SSD_STATE].astype(BF16)
        cb = _dot_nt(cgb, bg.astype(BF16))
        bg_t = bg.T
        for pair in range(4):
            p0 = (g * 4 + pair) * LANES
            xp = x[:, p0:p0 + LANES]
            ypair = None
            for hh in range(2):
                h = g * 8 + pair * 2 + hh
                c = d * SSD_HEADS + h
                xm = jnp.where(lane_lo if hh == 0 else jnp.logical_not(lane_lo), xp, 0.0).astype(BF16)
                col = jnp.broadcast_to(acs_cols[:, c:c + 1], (n, n))
                decay_dt = jnp.exp2(jnp.where(tri, col - acs_dt_r[h:h + 1, :], MASK_NEG))
                m = (cb * decay_dt).astype(BF16)
                st = st_ref[d, h]
                y = _dot(m, xm) + jnp.exp2(col) * _dot(cgb, st.astype(BF16))
                bh = (bg_t * w_r[h:h + 1, :]).astype(BF16)
                st_ref[d, h] = st * dec[h:h + 1, :] + _dot(bh, xm)
                ypair = y if ypair is None else ypair + y
            y_ref[:, p0:p0 + LANES] = ypair


def _ssd_kernel(xf_ref, xb_ref, gtf_ref, gtb_ref, bcr_ref, acr_ref, yf_ref, yb_ref, st_ref):
    @pl.when(pl.program_id(1) == 0)
    def _():
        st_ref[...] = jnp.zeros_like(st_ref)

    gf = _ssd_gates(gtf_ref[...], bcr_ref[0:SSD_HEADS, :], acr_ref[0:SSD_HEADS, :], False)
    gb = _ssd_gates(gtb_ref[...], bcr_ref[SSD_HEADS:2 * SSD_HEADS, :], acr_ref[SSD_HEADS:2 * SSD_HEADS, :], True)
    acs_cols = _cols_of([gf[1], gb[1]])
    _ssd_direction(xf_ref[...], gf, acs_cols, st_ref, 0, yf_ref)
    _ssd_direction(xb_ref[...], gb, acs_cols, st_ref, 1, yb_ref)


def _ssd_scan(xbc_act, gates_t, bias_cr, alog_cr, batch):
    T = xbc_act.shape[0]
    n = T // batch // SSD_CHUNK
    fwd = lambda b, i: b * n + i
    bwd = lambda b, i: b * n + (n - 1 - i)
    const = lambda b, i: (0, 0)
    return pl.pallas_call(
        _ssd_kernel,
        out_shape=(jax.ShapeDtypeStruct((T, SSD_INNER), F32),
                   jax.ShapeDtypeStruct((T, SSD_INNER), F32)),
        grid=(batch, n),
        in_specs=[pl.BlockSpec((SSD_CHUNK, SSD_XBC), lambda b, i: (fwd(b, i), 0)),
                  pl.BlockSpec((SSD_CHUNK, SSD_XBC), lambda b, i: (bwd(b, i), 0)),
                  pl.BlockSpec((SSD_HEADS, SSD_CHUNK), lambda b, i: (0, fwd(b, i))),
                  pl.BlockSpec((SSD_HEADS, SSD_CHUNK), lambda b, i: (1, bwd(b, i))),
                  pl.BlockSpec((2 * SSD_HEADS, LANES), const),
                  pl.BlockSpec((2 * SSD_HEADS, LANES), const)],
        out_specs=(pl.BlockSpec((SSD_CHUNK, SSD_INNER), lambda b, i: (fwd(b, i), 0)),
                   pl.BlockSpec((SSD_CHUNK, SSD_INNER), lambda b, i: (bwd(b, i), 0))),
        scratch_shapes=[pltpu.VMEM((2, SSD_HEADS, SSD_STATE, LANES), F32)],
        compiler_params=_cparams(("arbitrary", "arbitrary")),
        name="ssd_scan",
    )(xbc_act, xbc_act, gates_t, gates_t, bias_cr, alog_cr)


def _split3(x, axis=0):
    hi = x.astype(BF16)
    r1 = x - hi.astype(F32)
    mid = r1.astype(BF16)
    lo = (r1 - mid.astype(F32)).astype(BF16)
    return jnp.concatenate([hi, mid, lo], axis=axis)


def _hgrn2_cumsum_lhs(reverse):
    n = HG_SUB
    r_i = _iota((2 * n, n), 0)
    c_i = _iota((2 * n, n), 1)
    t = jnp.where(r_i < n, r_i, r_i - n)
    ref = (t // HG_REFBLK) * HG_REFBLK + (HG_REFBLK - 1 if reverse else 0)
    upto = jnp.where(r_i < n, t, ref)
    m = ((c_i >= upto) if reverse else (c_i <= upto)).astype(BF16)
    return jnp.concatenate([m, m, m], axis=1)


def _hgrn2_intra(q, f, v, lb, qmask, reverse):
    n = HG_SUB
    nblk = n // HG_REFBLK
    slab = 2 * SUBLANES
    tri = _causal(n, reverse)
    last = 0 if reverse else n - 1
    rowblk = _iota((slab, HG_DK), 0) // HG_REFBLK

    qh = _silu(q)
    key = (1.0 - lb) / (1.0 + jnp.exp(f))
    g = jnp.log2(1.0 - key)
    gg = _dot(_hgrn2_cumsum_lhs(reverse), _split3(g))
    gc = gg[:n]
    gref = gg[n:]
    qg = (qh * jnp.exp2(gc - gref)).astype(BF16)
    qs = (qh * jnp.exp2(gc)).astype(BF16)
    g_last = gc[last:last + 1, :]
    hk = gc - jnp.log2(key)
    kd = jnp.exp2(g_last - hk).astype(BF16)
    dec = jnp.exp2(g_last)
    vb = v.astype(BF16)

    o_intra = []
    for h in range(HG_HEADS):
        cols = slice(h * HG_DK, (h + 1) * HG_DK)
        hk_h = hk[:, cols]
        gc_h = gc[:, cols]
        k_parts = [None] * nblk
        far = []
        prev_edge = prev_r = prev_near = None
        for j in (range(nblk - 1, -1, -1) if reverse else range(nblk)):
            ref_row = j * HG_REFBLK + (HG_REFBLK - 1 if reverse else 0)
            rj = gc_h[ref_row:ref_row + 1, :]
            edge = (j * HG_REFBLK) // slab
            jb = j - edge * (slab // HG_REFBLK)
            if prev_r is not None:
                if edge != prev_edge:
                    far.append(prev_near)
                step = jnp.exp2(rj - prev_r)
                far = [p * step for p in far]
            live = (rowblk >= jb) if reverse else (rowblk <= jb)
            near = jnp.exp2(jnp.where(live, rj - hk_h[edge * slab:(edge + 1) * slab], MASK_NEG))
            pieces = [near] + far[::-1] if reverse else far + [near]
            k_parts[j] = jnp.concatenate([p.astype(BF16) for p in pieces], axis=0)
            prev_edge, prev_r, prev_near = edge, rj, near
        per_slab = slab // HG_REFBLK
        att_slabs = []
        for m in range(n // slab):
            q_m = jnp.tile(qg[m * slab:(m + 1) * slab, cols], (1, per_slab)) * qmask
            k_m = jnp.concatenate(k_parts[m * per_slab:(m + 1) * per_slab], axis=1)
            att_m = _dot_nt(q_m, k_m)
            dead = jnp.zeros((slab, n - att_m.shape[1]), F32)
            if dead.shape[1]:
                att_m = jnp.concatenate([dead, att_m] if reverse else [att_m, dead], axis=1)
            att_slabs.append(att_m)
        att = jnp.where(tri, jnp.concatenate(att_slabs, axis=0), 0.0).astype(BF16)
        o_intra.append(_dot(att, vb[:, cols]))
    return o_intra, qs, kd, dec, vb


def _hgrn2_state(intra, st_ref, d, o_ref, rows):
    o_intra, qs, kd, dec, vb = intra
    for h in range(HG_HEADS):
        cols = slice(h * HG_DK, (h + 1) * HG_DK)
        st = st_ref[d, h]
        o_ref[rows, cols] = o_intra[h] + _dot_nt(qs[:, cols], st.astype(BF16))
        st_ref[d, h] = st * dec[:, cols] + _dot_tn(vb[:, cols], kd[:, cols])


def _hgrn2_kernel(qf_ref, ff_ref, vf_ref, qb_ref, fb_ref, vb_ref, lb_ref, qmask_ref, of_ref, ob_ref, st_ref):
    @pl.when(pl.program_id(1) == 0)
    def _():
        st_ref[...] = jnp.zeros_like(st_ref)

    lb = lb_ref[...]
    qmask = qmask_ref[...]
    refs = ((qf_ref, ff_ref, vf_ref, of_ref), (qb_ref, fb_ref, vb_ref, ob_ref))
    nsub = SCAN_STEP // HG_SUB
    units = []
    for k in range(nsub):
        units.append((0, k))
        units.append((1, nsub - 1 - k))
    pending = None
    for d, sc in units:
        rows = slice(sc * HG_SUB, (sc + 1) * HG_SUB)
        q_ref, f_ref, v_ref, o_ref = refs[d]
        intra = _hgrn2_intra(q_ref[rows, :], f_ref[rows, :], v_ref[rows, :], lb, qmask, d == 1)
        if pending is not None:
            _hgrn2_state(*pending)
        pending = (intra, st_ref, d, o_ref, rows)
    _hgrn2_state(*pending)


def _hgrn2_scan(proj, lb, batch):
    T = proj.shape[0]
    n = T // batch // SCAN_STEP
    fwd = lambda b, i: b * n + i
    bwd = lambda b, i: b * n + (n - 1 - i)
    blk = (SCAN_STEP, HG_WIDTH)
    slab = 2 * SUBLANES
    nblk = slab // HG_REFBLK
    qmask = (jnp.arange(slab)[:, None] // HG_REFBLK == jnp.arange(nblk * HG_DK)[None, :] // HG_DK).astype(BF16)
    return pl.pallas_call(
        _hgrn2_kernel,
        out_shape=(jax.ShapeDtypeStruct((T, HG_WIDTH), F32),
                   jax.ShapeDtypeStruct((T, HG_WIDTH), F32)),
        grid=(batch, n),
        in_specs=[pl.BlockSpec(blk, lambda b, i: (fwd(b, i), COLBLK_HQ)),
                  pl.BlockSpec(blk, lambda b, i: (fwd(b, i), COLBLK_HF_FWD)),
                  pl.BlockSpec(blk, lambda b, i: (fwd(b, i), COLBLK_HI)),
                  pl.BlockSpec(blk, lambda b, i: (bwd(b, i), COLBLK_HQ)),
                  pl.BlockSpec(blk, lambda b, i: (bwd(b, i), COLBLK_HF_BWD)),
                  pl.BlockSpec(blk, lambda b, i: (bwd(b, i), COLBLK_HI)),
                  pl.BlockSpec((1, HG_WIDTH), lambda b, i: (0, 0)),
                  pl.BlockSpec((slab, nblk * HG_DK), lambda b, i: (0, 0))],
        out_specs=(pl.BlockSpec(blk, lambda b, i: (fwd(b, i), 0)),
                   pl.BlockSpec(blk, lambda b, i: (bwd(b, i), 0))),
        scratch_shapes=[pltpu.VMEM((2, HG_HEADS, HG_DK, HG_DK), F32)],
        compiler_params=_cparams(("arbitrary", "arbitrary")),
        name="hgrn2_scan",
    )(proj, proj, proj, proj, proj, proj, lb.reshape(1, HG_WIDTH), qmask)


def _mlstm_kernel(batch, qkf_ref, vf_ref, qkb_ref, vb_ref, *rest):
    gt_refs = (rest[:batch], rest[batch:2 * batch])
    bcr_ref, of_ref, ob_ref, c_ref, m_ref, mlane_ref = rest[2 * batch:]

    @pl.when(pl.program_id(0) == 0)
    def _():
        c_ref[...] = jnp.zeros_like(c_ref)
        m_ref[...] = jnp.zeros_like(m_ref)
        mlane_ref[...] = jnp.zeros_like(mlane_ref)

    n = SCAN_STEP
    ones = jnp.ones((n, ML_DH), BF16)
    io_refs = ((qkf_ref, vf_ref, of_ref), (qkb_ref, vb_ref, ob_ref))
    tris = (_causal(n, False), _causal(n, True))
    row_i = _iota((n, LANES), 0)

    gates = {}
    for d in range(2):
        reverse = d == 1
        last = 0 if reverse else n - 1
        rows = []
        for b in range(batch):
            g_r = gt_refs[d][b][...] + bcr_ref[...]
            bc = _cumsum_lanes(_log_sigmoid(g_r) * LOG2E, reverse)[SUBLANES:]
            ig = g_r[:SUBLANES] * LOG2E
            rows.append((bc, ig, ig - bc))
        tb = _cols_of([r[0] for r in rows])
        pm = _cols_of([r[2] for r in rows])
        for sh in (1, 2, 4, 8, 16, 32, 64):
            shifted = pltpu.roll(pm, (n - sh) if reverse else sh, 0)
            valid = (row_i < n - sh) if reverse else (row_i >= sh)
            pm = jnp.maximum(pm, jnp.where(valid, shifted, -jnp.inf))
        m_prev_lane = mlane_ref[d][0:1, :]
        floor = jnp.where((row_i > 0) if reverse else (row_i < n - 1), MASK_NEG * LOG2E, -jnp.inf)
        mt = jnp.maximum(jnp.maximum(tb + pm, tb + m_prev_lane), floor)
        b_last_lane = tb[last:last + 1, :]
        m_new_lane = jnp.maximum(b_last_lane + m_prev_lane, b_last_lane + pm[last:last + 1, :])
        mlane_ref[d] = jnp.broadcast_to(m_new_lane, (SUBLANES, LANES))
        bm = tb - mt
        for b in range(batch):
            bc, ig, u = rows[b]
            m_prev = m_ref[b * 2 + d]
            b_last = bc[:, last:last + 1]
            m_new = jnp.maximum(b_last + m_prev[:, 0:1], b_last + jnp.max(u, axis=1, keepdims=True))
            w_s = jnp.exp2(b_last - bc + ig - m_new)
            dec = jnp.exp2(b_last + m_prev[:, 0:1] - m_new)
            m_ref[b * 2 + d] = jnp.broadcast_to(m_new, (SUBLANES, LANES))
            gates[b, d] = (u, bm, mt, m_prev, w_s, dec)

    units = [(b, d, h) for h in range(ML_HEADS) for b in range(batch) for d in range(2)]
    for b, d, h in units:
        u, bm, mt, m_prev, w_s, dec = gates[b, d]
        qk_ref, v_ref, o_ref = io_refs[d]
        r = d * ML_HEADS + h
        c = b * SUBLANES + r
        qf = qk_ref[b, :, h * ML_DH:(h + 1) * ML_DH] * (ML_DH ** -0.5)
        k = qk_ref[b, :, ML_WIDTH + h * ML_DH:ML_WIDTH + (h + 1) * ML_DH]
        ve = jnp.concatenate([v_ref[b, :, h * ML_DH:(h + 1) * ML_DH].astype(BF16), ones], axis=1)
        b_minus_m = jnp.broadcast_to(bm[:, c:c + 1], (n, n))
        m_t = jnp.broadcast_to(mt[:, c:c + 1], (n, n))
        w_intra = jnp.exp2(jnp.where(tris[d], b_minus_m + u[r:r + 1, :], MASK_NEG))
        w_inter = jnp.exp2(b_minus_m + m_prev[r:r + 1, 0:1])
        s = (_dot_nt(qf.astype(BF16), k.astype(BF16)) * w_intra).astype(BF16)
        cst = c_ref[b, d, h]
        numden = _dot(jnp.concatenate([s, (qf * w_inter).astype(BF16)], axis=1),
                      jnp.concatenate([ve, cst.astype(BF16)], axis=0))
        den = numden[:, ML_DH:]
        o_ref[b, :, h * ML_DH:(h + 1) * ML_DH] = numden[:, :ML_DH] / jnp.maximum(jnp.abs(den), jnp.exp2(-m_t))
        kw_t = (k.T * w_s[r:r + 1, :]).astype(BF16)
        c_ref[b, d, h] = dec[r:r + 1, :] * cst + _dot(kw_t, ve)


def _mlstm_scan(mqk_act, proj, gates_t, bias_cr, batch):
    T = proj.shape[0]
    seq = T // batch
    n = seq // SCAN_STEP
    gt_rowblk = GATE_IG // (2 * SUBLANES)
    fwd3 = lambda cb: (lambda i: (0, i, cb))
    bwd3 = lambda cb: (lambda i: (0, n - 1 - i, cb))
    gt_fwd = [pl.BlockSpec((2 * SUBLANES, SCAN_STEP), functools.partial(lambda b, i: (gt_rowblk, b * n + i), b))
              for b in range(batch)]
    gt_bwd = [pl.BlockSpec((2 * SUBLANES, SCAN_STEP),
                           functools.partial(lambda b, i: (gt_rowblk, b * n + n - 1 - i), b))
              for b in range(batch)]
    qk3 = mqk_act.reshape(batch, seq, 2 * ML_WIDTH)
    proj3 = proj.reshape(batch, seq, D_PROJ_PAD)
    o_f, o_b = pl.pallas_call(
        functools.partial(_mlstm_kernel, batch),
        out_shape=(jax.ShapeDtypeStruct((batch, seq, ML_WIDTH), F32),
                   jax.ShapeDtypeStruct((batch, seq, ML_WIDTH), F32)),
        grid=(n,),
        in_specs=[pl.BlockSpec((batch, SCAN_STEP, 2 * ML_WIDTH), fwd3(0)),
                  pl.BlockSpec((batch, SCAN_STEP, ML_WIDTH), fwd3(COLBLK_MV)),
                  pl.BlockSpec((batch, SCAN_STEP, 2 * ML_WIDTH), bwd3(0)),
                  pl.BlockSpec((batch, SCAN_STEP, ML_WIDTH), bwd3(COLBLK_MV))]
                 + gt_fwd + gt_bwd
                 + [pl.BlockSpec((2 * SUBLANES, LANES), lambda i: (0, 0))],
        out_specs=(pl.BlockSpec((batch, SCAN_STEP, ML_WIDTH), fwd3(0)),
                   pl.BlockSpec((batch, SCAN_STEP, ML_WIDTH), bwd3(0))),
        scratch_shapes=[pltpu.VMEM((batch, 2, ML_HEADS, ML_DH, 2 * ML_DH), F32),
                        pltpu.VMEM((batch * 2, SUBLANES, LANES), F32),
                        pltpu.VMEM((2, SUBLANES, LANES), F32)],
        compiler_params=_cparams(("arbitrary",)),
        name="mlstm_scan",
    )(qk3, proj3, qk3, proj3, *([gates_t] * (2 * batch)), bias_cr)
    return o_f.reshape(T, ML_WIDTH), o_b.reshape(T, ML_WIDTH)


def _outproj_kernel(yf_ref, yb_ref, xs_ref, z_ref, of_ref, ob_ref, hg_ref, mf_ref, mb_ref, mo_ref,
                    h_ref, dskip_ref, ssdw_ref, hgw_ref, mlw_ref, wout_ref, g_ref, b_ref, o_ref):
    y = (yf_ref[...] + yb_ref[...] + xs_ref[...] * dskip_ref[...]) * _silu(z_ref[...])
    gw = SSD_INNER // SSD_GROUPS
    acc = None
    for g in range(SSD_GROUPS):
        seg = y[:, g * gw:(g + 1) * gw]
        seg = seg * lax.rsqrt(jnp.mean(seg * seg, -1, keepdims=True) + RMS_EPS) * ssdw_ref[:, g * gw:(g + 1) * gw]
        part = _dot(seg.astype(BF16), wout_ref[g * gw:(g + 1) * gw, :])
        acc = part if acc is None else acc + part
    o = of_ref[...] + ob_ref[...]
    hgate = _silu(hg_ref[...])
    m = mf_ref[...] + mb_ref[...]
    mgate = _sigmoid(mo_ref[...])
    for h in range(HG_HEADS):
        cols = slice(h * HG_DK, (h + 1) * HG_DK)
        seg = o[:, cols]
        seg = seg * lax.rsqrt(jnp.mean(seg * seg, -1, keepdims=True) + RMS_EPS) * hgw_ref[:, cols] * hgate[:, cols]
        r0 = SSD_INNER + h * HG_DK
        acc = acc + _dot(seg.astype(BF16), wout_ref[r0:r0 + HG_DK, :])
    for h in range(ML_HEADS):
        cols = slice(h * ML_DH, (h + 1) * ML_DH)
        seg = m[:, cols]
        seg = seg - jnp.mean(seg, -1, keepdims=True)
        seg = seg * lax.rsqrt(jnp.mean(seg * seg, -1, keepdims=True) + LN_EPS) * mlw_ref[:, cols] * mgate[:, cols]
        r0 = SSD_INNER + HG_WIDTH + h * ML_DH
        acc = acc + _dot(seg.astype(BF16), wout_ref[r0:r0 + ML_DH, :])
    o_ref[...] = _layer_norm(ALPHA * h_ref[...] + acc, g_ref[...], b_ref[...])


def _outproj(y_f, y_b, xbc_act, proj, o_f, o_b, m_f, m_b, h, dskip, ssd_w, hg_w, ml_w, w_out, g, b):
    T, D = h.shape
    tm = TM_OUT
    row = lambda width, cb: pl.BlockSpec((tm, width), lambda i: (i, cb))
    par = lambda width: pl.BlockSpec((1, width), lambda i: (0, 0))
    return pl.pallas_call(
        _outproj_kernel,
        out_shape=jax.ShapeDtypeStruct((T, D), F32),
        grid=(T // tm,),
        in_specs=[row(SSD_INNER, 0), row(SSD_INNER, 0), row(SSD_INNER, 0), row(SSD_INNER, COLBLK_Z),
                  row(HG_WIDTH, 0), row(HG_WIDTH, 0), row(HG_WIDTH, COLBLK_HG),
                  row(ML_WIDTH, 0), row(ML_WIDTH, 0), row(ML_WIDTH, COLBLK_MO),
                  row(D, 0), par(SSD_INNER), par(SSD_INNER), par(HG_WIDTH), par(ML_WIDTH),
                  pl.BlockSpec((D_MIX, D), lambda i: (0, 0), pipeline_mode=pl.Buffered(1)),
                  par(D), par(D)],
        out_specs=pl.BlockSpec((tm, D), lambda i: (i, 0)),
        compiler_params=_cparams(("parallel",)),
        name="outproj_ln1",
    )(y_f, y_b, xbc_act, proj, o_f, o_b, proj, m_f, m_b, proj, h, dskip, ssd_w, hg_w, ml_w, w_out,
      g.reshape(1, D), b.reshape(1, D))


def _ffn_kernel(h_ref, w1_ref, w3_ref, w2_ref, g_ref, b_ref, o_ref, acc_ref):
    h = h_ref[...]
    hb = h.astype(BF16)
    for c in range(D_FF_DENSE // TF_FFN):
        cols = slice(c * TF_FFN, (c + 1) * TF_FFN)
        a = _dot(hb, w1_ref[:, cols])
        u = _dot(hb, w3_ref[:, cols])
        part = _dot((_silu(a) * u).astype(BF16), w2_ref[cols, :])
        if c == 0:
            acc_ref[...] = part
        else:
            acc_ref[...] += part
    o_ref[...] = _layer_norm(ALPHA * h + acc_ref[...], g_ref[...], b_ref[...])


def _dense_ffn(h, w1, w3, w2, g, b):
    T, D = h.shape
    tm = TM_FFN
    resident = lambda shape: pl.BlockSpec(shape, lambda i: (0, 0), pipeline_mode=pl.Buffered(1))
    return pl.pallas_call(
        _ffn_kernel,
        out_shape=jax.ShapeDtypeStruct((T, D), F32),
        grid=(T // tm,),
        in_specs=[pl.BlockSpec((tm, D), lambda i: (i, 0)),
                  resident((D, D_FF_DENSE)), resident((D, D_FF_DENSE)), resident((D_FF_DENSE, D)),
                  pl.BlockSpec((1, D), lambda i: (0, 0)), pl.BlockSpec((1, D), lambda i: (0, 0))],
        out_specs=pl.BlockSpec((tm, D), lambda i: (i, 0)),
        scratch_shapes=[pltpu.VMEM((tm, D), F32)],
        compiler_params=_cparams(("parallel",)),
        name="dense_ffn_ln2",
    )(h, w1, w3, w2, g.reshape(1, D), b.reshape(1, D))


def _router_kernel(h_ref, w_ref, idx_ref, gate_ref):
    logits = _dot_hi(h_ref[...], w_ref[...])
    lane = _iota(logits.shape, 1)
    lane_f = lane.astype(F32)
    x = jnp.where(lane < N_EXPERTS, logits, -jnp.inf)
    m1 = jnp.max(x, axis=1, keepdims=True)
    i1 = jnp.min(jnp.where(x == m1, lane_f, float(LANES)), axis=1, keepdims=True)
    x2 = jnp.where(lane_f == i1, -jnp.inf, x)
    m2 = jnp.max(x2, axis=1, keepdims=True)
    i2 = jnp.min(jnp.where(x2 == m2, lane_f, float(LANES)), axis=1, keepdims=True)
    e2 = jnp.exp(m2 - m1)
    g1 = 1.0 / (1.0 + e2)
    g2 = e2 / (1.0 + e2)
    idx_ref[...] = jnp.where(lane == 0, i1, jnp.where(lane == 1, i2, 0.0)).astype(jnp.int32)
    gate_ref[...] = jnp.where(lane == 0, g1, jnp.where(lane == 1, g2, 0.0))


def _router(h, w_pad):
    T, D = h.shape
    tm = TM_ROUTER
    return pl.pallas_call(
        _router_kernel,
        out_shape=(jax.ShapeDtypeStruct((T, LANES), jnp.int32),
                   jax.ShapeDtypeStruct((T, LANES), F32)),
        grid=(T // tm,),
        in_specs=[pl.BlockSpec((tm, D), lambda i: (i, 0)),
                  pl.BlockSpec((D, LANES), lambda i: (0, 0))],
        out_specs=(pl.BlockSpec((tm, LANES), lambda i: (i, 0)),
                   pl.BlockSpec((tm, LANES), lambda i: (i, 0))),
        compiler_params=_cparams(("parallel",)),
        name="moe_router",
    )(h, w_pad)


def _gather_rows(src, idx):
    n = idx.shape[0]
    D = src.shape[1]
    per_worker = n // SC_WORKERS
    n_chunks = per_worker // SC_GATHER_CHUNK
    assert n_chunks * SC_GATHER_CHUNK * SC_WORKERS == n and n_chunks % 2 == 0
    mesh = plsc.VectorSubcoreMesh(core_axis_name="c", subcore_axis_name="s")

    @functools.partial(
        pl.kernel, mesh=mesh,
        out_type=jax.ShapeDtypeStruct((n, D), src.dtype),
        scratch_types=[pltpu.VMEM((2, SC_GATHER_CHUNK), jnp.int32),
                       pltpu.VMEM((2, SC_GATHER_CHUNK, D), src.dtype),
                       pltpu.SemaphoreType.DMA((2,))],
        name="moe_row_gather",
    )
    def gather(src_hbm, idx_hbm, out_hbm, idx_v, rows_v, sems):
        worker = lax.axis_index("s") * SC_CORES + lax.axis_index("c")
        base = worker * per_worker

        def gather_copy(slot):
            return pltpu.make_async_copy(src_hbm.at[idx_v.at[slot]], rows_v.at[slot], sems.at[slot])

        def start(j, slot):
            pltpu.sync_copy(idx_hbm.at[pl.ds(base + j * SC_GATHER_CHUNK, SC_GATHER_CHUNK)], idx_v.at[slot])
            gather_copy(slot).start()

        def finish(j, slot):
            gather_copy(slot).wait()
            pltpu.sync_copy(rows_v.at[slot], out_hbm.at[pl.ds(base + j * SC_GATHER_CHUNK, SC_GATHER_CHUNK)])

        start(0, 0)

        @pl.loop(0, n_chunks, step=2)
        def _(j):
            start(j + 1, 1)
            finish(j, 0)

            @pl.when(j + 2 < n_chunks)
            def _():
                start(j + 2, 0)

            finish(j + 1, 1)

    return gather(src, idx)


def _scatter_rows(init, vals, idx):
    n_out, W = init.shape
    n = idx.shape[0]
    out_per = n_out // SC_SUBCORES
    per = n // SC_SUBCORES
    n_chunks = per // SC_SCATTER_CHUNK
    init_chunks = out_per // SC_INIT_CHUNK
    assert init_chunks * SC_INIT_CHUNK * SC_SUBCORES == n_out and n_chunks * SC_SCATTER_CHUNK * SC_SUBCORES == n
    mesh = plsc.VectorSubcoreMesh(core_axis_name="c", subcore_axis_name="s", num_cores=1)

    @functools.partial(
        pl.kernel, mesh=mesh,
        out_type=jax.ShapeDtypeStruct((n_out, W), init.dtype),
        scratch_types=[pltpu.VMEM((SC_INIT_CHUNK, W), init.dtype),
                       pltpu.VMEM((SC_SCATTER_CHUNK,), jnp.int32),
                       pltpu.VMEM((SC_SCATTER_CHUNK, W), init.dtype)],
        name="moe_row_scatter",
    )
    def scatter(init_hbm, vals_hbm, idx_hbm, out_hbm, init_v, idx_v, vals_v):
        worker = lax.axis_index("s")

        @pl.loop(0, init_chunks)
        def _(j):
            off = worker * out_per + j * SC_INIT_CHUNK
            pltpu.sync_copy(init_hbm.at[pl.ds(off, SC_INIT_CHUNK)], init_v)
            pltpu.sync_copy(init_v, out_hbm.at[pl.ds(off, SC_INIT_CHUNK)])

        plsc.subcore_barrier()

        @pl.loop(0, n_chunks)
        def _(j):
            off = worker * per + j * SC_SCATTER_CHUNK
            pltpu.sync_copy(idx_hbm.at[pl.ds(off, SC_SCATTER_CHUNK)], idx_v)
            pltpu.sync_copy(vals_hbm.at[pl.ds(off, SC_SCATTER_CHUNK)], vals_v)
            pltpu.sync_copy(vals_v, out_hbm.at[idx_v])

    return scatter(init, vals, idx)


def _expert_kernel(be_ref, nact_ref, x_ref, w1_ref, w3_ref, w2_ref, o_ref):
    i = pl.program_id(0)
    f = pl.program_id(1)

    @pl.when(f == 0)
    def _():
        o_ref[...] = jnp.zeros_like(o_ref)

    @pl.when(i < nact_ref[0])
    def _():
        xb = x_ref[...].astype(BF16)
        a = _dot(xb, w1_ref[...].astype(BF16))
        u = _dot(xb, w3_ref[...].astype(BF16))
        o_ref[...] += _dot((_silu(a) * u).astype(BF16), w2_ref[...].astype(BF16))


def _expert_ffn(xs, block_expert, n_active, w1, w3, w2, layer):
    n_rows, D = xs.shape
    n_blocks = n_rows // MOE_BM
    n_ff = D_FF_EXPERT // MOE_TF

    def widx(i, f, be, nact):
        live = i < nact[0]
        return be[jnp.minimum(i, nact[0] - 1)], jnp.where(live, f, n_ff - 1)

    def w13_map(i, f, be, nact):
        e, ff = widx(i, f, be, nact)
        return layer, e, 0, ff

    def w2_map(i, f, be, nact):
        e, ff = widx(i, f, be, nact)
        return layer, e, ff, 0

    grid_spec = pltpu.PrefetchScalarGridSpec(
        num_scalar_prefetch=2,
        grid=(n_blocks, n_ff),
        in_specs=[pl.BlockSpec((MOE_BM, D), lambda i, f, be, nact: (i, 0)),
                  pl.BlockSpec((None, None, D, MOE_TF), w13_map),
                  pl.BlockSpec((None, None, D, MOE_TF), w13_map),
                  pl.BlockSpec((None, None, MOE_TF, D), w2_map)],
        out_specs=pl.BlockSpec((MOE_BM, D), lambda i, f, be, nact: (i, 0)),
    )
    return pl.pallas_call(
        _expert_kernel,
        out_shape=jax.ShapeDtypeStruct((n_rows, D), F32),
        grid_spec=grid_spec,
        compiler_params=_cparams(("arbitrary", "arbitrary")),
        name="moe_experts",
    )(block_expert, n_active, xs, w1, w3, w2)


def _combine_kernel(y0_ref, y1_ref, gate_ref, h_ref, g_ref, b_ref, o_ref):
    gate = gate_ref[...]
    f = y0_ref[...] * gate[:, 0:1] + y1_ref[...] * gate[:, 1:2]
    o_ref[...] = _layer_norm(ALPHA * h_ref[...] + f, g_ref[...], b_ref[...])


def _combine(y_slots, gates, h, g, b):
    T, D = h.shape
    tm = TM_COMBINE
    return pl.pallas_call(
        _combine_kernel,
        out_shape=jax.ShapeDtypeStruct((T, D), F32),
        grid=(T // tm,),
        in_specs=[pl.BlockSpec((tm, D), lambda i: (i, 0)),
                  pl.BlockSpec((tm, D), lambda i: (i + T // tm, 0)),
                  pl.BlockSpec((tm, LANES), lambda i: (i, 0)),
                  pl.BlockSpec((tm, D), lambda i: (i, 0)),
                  pl.BlockSpec((1, D), lambda i: (0, 0)),
                  pl.BlockSpec((1, D), lambda i: (0, 0))],
        out_specs=pl.BlockSpec((tm, D), lambda i: (i, 0)),
        compiler_params=_cparams(("parallel",)),
        name="moe_combine_ln2",
    )(y_slots, y_slots, gates, h, g.reshape(1, D), b.reshape(1, D))


def _moe_ffn(h, router_w, w1, w3, w2, layer, g, b):
    T, D = h.shape
    n_pairs = T * TOP_K
    n_blocks = n_pairs // MOE_BM + N_EXPERTS
    n_rows = n_blocks * MOE_BM
    w_pad = jnp.pad(router_w.astype(F32), ((0, 0), (0, LANES - N_EXPERTS)))
    idx_pad, gates_pad = _router(h, w_pad)
    expert_ids = idx_pad[:, :TOP_K].reshape(-1)
    onehot = (expert_ids[:, None] == jnp.arange(N_EXPERTS, dtype=jnp.int32)[None, :]).astype(jnp.int32)
    running = jnp.cumsum(onehot, axis=0)
    counts = running[-1]
    rank = jnp.sum(onehot * running, axis=1) - 1
    padded = (counts + MOE_BM - 1) // MOE_BM * MOE_BM
    ends = jnp.cumsum(padded)
    pos = (ends - padded)[expert_ids] + rank
    token_ids = jnp.arange(n_pairs, dtype=jnp.int32) // TOP_K
    default = jnp.broadcast_to((jnp.arange(n_rows, dtype=jnp.int32) % T)[:, None], (n_rows, SC_ROW_WORDS))
    row_token = _scatter_rows(default, jnp.broadcast_to(token_ids[:, None], (n_pairs, SC_ROW_WORDS)),
                              pos.astype(jnp.int32))[:, 0]
    block_start = jnp.arange(n_blocks, dtype=jnp.int32) * MOE_BM
    block_expert = jnp.minimum(jnp.searchsorted(ends, block_start, side='right'),
                               N_EXPERTS - 1).astype(jnp.int32)
    n_active = (ends[-1:] // MOE_BM).astype(jnp.int32)

    xs = _gather_rows(h, row_token)
    ys = _expert_ffn(xs, block_expert, n_active, w1, w3, w2, layer)
    slot_major = pos.astype(jnp.int32).reshape(T, TOP_K).T.reshape(-1)
    y_slots = _gather_rows(ys, slot_major)
    return _combine(y_slots, gates_pad, h, g, b)


def _relayout_w_in(w):
    sizes = (SSD_INNER, SSD_XBC, 2 * SSD_HEADS, HG_WIDTH, 2 * HG_WIDTH, HG_WIDTH, HG_WIDTH,
             2 * ML_WIDTH, ML_WIDTH, ML_WIDTH, 2 * ML_HEADS, 2 * ML_HEADS)
    offs = [0]
    for s in sizes:
        offs.append(offs[-1] + s)
    (z, xbc, dt, hq, hf, hi, hg, mqk, mv, mo, mig, mfg) = [w[:, offs[k]:offs[k + 1]] for k in range(len(sizes))]
    pad = jnp.zeros((w.shape[0], LANES - 2 * SSD_HEADS - 4 * ML_HEADS), w.dtype)
    wn = jnp.concatenate([z, mqk, hf, xbc, hq, hi, hg, mv, mo], axis=1).astype(BF16)
    wgt = jnp.concatenate([dt, mig, mfg, pad], axis=1).astype(BF16).T
    return wn, wgt


def _gate_rows(vals_by_offset):
    row = jnp.zeros((LANES,), F32)
    for off, v in vals_by_offset:
        row = lax.dynamic_update_slice(row, v.reshape(-1).astype(F32), (off,))
    return jnp.broadcast_to(row[:, None], (LANES, LANES))


def kernel(x, ln_in_g, ln_in_b, w_in, ssd_conv_w, ssd_conv_b, ssd_dt_bias, ssd_a_log, ssd_d, ssd_norm_w, hg_lb_logits, hg_norm_w, ml_conv_w, ml_conv_b, ml_ig_bias, ml_fg_bias, ml_norm_w, w_out, ln1_g, ln1_b, ln2_g, ln2_b, ffn_w1, ffn_w3, ffn_w2, moe_router, moe_w1, moe_w3, moe_w2):
    batch, seq_len, D = x.shape
    T = batch * seq_len
    f32 = lambda t: t.astype(F32)
    lb_soft = jax.nn.softmax(f32(hg_lb_logits), axis=0)
    lb_all = jnp.cumsum(lb_soft, axis=0) - lb_soft[0]

    h = _input_ln(f32(x).reshape(T, D), f32(ln_in_g), f32(ln_in_b))
    for l in range(DEPTH):
        w, wgt = _relayout_w_in(w_in[l])
        bias_cr = _gate_rows(((GATE_DT, ssd_dt_bias[l]), (GATE_IG, ml_ig_bias[l]), (GATE_FG, ml_fg_bias[l])))
        alog_cr = _gate_rows(((GATE_DT, ssd_a_log[l]),))

        proj, gates_t = _inproj(h, w, wgt)
        xbc_act = _conv_silu(proj, f32(ssd_conv_w[l]), f32(ssd_conv_b[l]), COLBLK_XBC, SSD_XBC, seq_len)
        mqk_act = _conv_silu(proj, f32(ml_conv_w[l]), f32(ml_conv_b[l]), COLBLK_MQK, 2 * ML_WIDTH, seq_len)
        y_f, y_b = _ssd_scan(xbc_act, gates_t, bias_cr[GATE_DT:GATE_DT + 2 * SSD_HEADS],
                             alog_cr[GATE_DT:GATE_DT + 2 * SSD_HEADS], batch)
        o_f, o_b = _hgrn2_scan(proj, lb_all[l], batch)
        m_f, m_b = _mlstm_scan(mqk_act, proj, gates_t, bias_cr[GATE_IG:GATE_IG + 4 * ML_HEADS], batch)
        dskip = jnp.repeat(f32(ssd_d[l]), SSD_HEAD_DIM).reshape(1, SSD_INNER)
        h = _outproj(y_f, y_b, xbc_act, proj, o_f, o_b, m_f, m_b, h, dskip,
                     f32(ssd_norm_w[l]).reshape(1, -1), f32(hg_norm_w[l]).reshape(1, -1),
                     f32(ml_norm_w[l]).reshape(1, -1), w_out[l].astype(BF16), f32(ln1_g[l]), f32(ln1_b[l]))
        if l % 2 == 0:
            j = l // 2
            h = _dense_ffn(h, ffn_w1[j].astype(BF16), ffn_w3[j].astype(BF16), ffn_w2[j].astype(BF16),
                           f32(ln2_g[l]), f32(ln2_b[l]))
        else:
            j = l // 2
            h = _moe_ffn(h, moe_router[j], f32(moe_w1), f32(moe_w3), f32(moe_w2), j,
                         f32(ln2_g[l]), f32(ln2_b[l]))
    return h.reshape(batch, seq_len, D).astype(x.dtype)
```

```python
import functools

import jax
import jax.numpy as jnp
from jax import lax
from jax.experimental import pallas as pl
from jax.experimental.pallas import tpu as pltpu
from jax.experimental.pallas import tpu_sc as plsc

F32 = jnp.float32
BF16 = jnp.bfloat16
HIGHEST = lax.Precision.HIGHEST

D_MODEL = 1024
DEPTH = 4
D_MIX = 2 * D_MODEL
SSD_HEADS = 16
SSD_INNER = 1024
SSD_HEAD_DIM = 64
SSD_GROUPS = 2
SSD_STATE = 128
SSD_XBC = 1536
SSD_CHUNK = 128
HG_HEADS = 4
HG_WIDTH = 512
HG_DK = 128
ML_HEADS = 4
ML_WIDTH = 512
ML_DH = 128
CONV_WIDTH = 5
D_FF_DENSE = 2816
N_EXPERTS = 8
TOP_K = 2
D_FF_EXPERT = 3584
LN_EPS = 1e-5
RMS_EPS = 1e-6
MASK_NEG = -1e4
ALPHA = (2.0 * DEPTH) ** 0.25
LOG2E = 1.4426950408889634

LANES = 128
SUBLANES = 8
VMEM_LIMIT = 56 * 1024 * 1024

D_PROJ_PAD = 7168
COLBLK_Z = 0
COLBLK_MQK = 1
COLBLK_XBC = 2
COLBLK_HF_FWD = 4
COLBLK_HF_BWD = 5
COLBLK_HQ = 9
COLBLK_HI = 10
COLBLK_HG = 11
COLBLK_MV = 12
COLBLK_MO = 13
GATE_DT = 0
GATE_IG = 32
GATE_FG = 40

TM_LN = 512
TM_INPROJ = 512
TM_CONV = 512
SCAN_STEP = 128
HG_SUB = 64
HG_REFBLK = 4
TM_OUT = 512
TM_FFN = 512
TF_FFN = 256
TM_ROUTER = 512
MOE_BM = 1024
MOE_TF = 512
SC_CORES = 2
SC_SUBCORES = 16
SC_WORKERS = 32
SC_SCATTER_CHUNK = 128
SC_ROW_WORDS = 128
SC_INIT_CHUNK = 512
SC_GATHER_CHUNK = 32
TM_COMBINE = 256


def _cparams(sem):
    return pltpu.CompilerParams(dimension_semantics=sem, vmem_limit_bytes=VMEM_LIMIT)


def _iota(shape, dim):
    return lax.broadcasted_iota(jnp.int32, shape, dim)


def _sigmoid(x):
    return 1.0 / (1.0 + jnp.exp(-x))


def _silu(x):
    return x * _sigmoid(x)


def _softplus(x):
    return jnp.maximum(x, 0.0) + jnp.log1p(jnp.exp(-jnp.abs(x)))


def _log_sigmoid(x):
    return -_softplus(-x)


def _layer_norm(x, g, b):
    mu = jnp.mean(x, -1, keepdims=True)
    xc = x - mu
    var = jnp.mean(xc * xc, -1, keepdims=True)
    return xc * lax.rsqrt(var + LN_EPS) * g + b


def _causal(n, reverse):
    r = _iota((n, n), 0)
    c = _iota((n, n), 1)
    return (c >= r) if reverse else (c <= r)


def _dot(a, b):
    return jnp.dot(a, b, preferred_element_type=F32)


def _dot_nt(a, b):
    return lax.dot_general(a, b, (((1,), (1,)), ((), ())), preferred_element_type=F32)


def _dot_tn(a, b):
    return lax.dot_general(a, b, (((0,), (0,)), ((), ())), preferred_element_type=F32)


def _dot_hi(a, b):
    return jnp.dot(a, b, precision=HIGHEST, preferred_element_type=F32)


def _ln_kernel(x_ref, g_ref, b_ref, o_ref):
    o_ref[...] = _layer_norm(x_ref[...], g_ref[...], b_ref[...])


def _input_ln(x, g, b):
    T, D = x.shape
    return pl.pallas_call(
        _ln_kernel,
        out_shape=jax.ShapeDtypeStruct((T, D), F32),
        grid=(T // TM_LN,),
        in_specs=[pl.BlockSpec((TM_LN, D), lambda i: (i, 0)),
                  pl.BlockSpec((1, D), lambda i: (0, 0)),
                  pl.BlockSpec((1, D), lambda i: (0, 0))],
        out_specs=pl.BlockSpec((TM_LN, D), lambda i: (i, 0)),
        compiler_params=_cparams(("parallel",)),
        name="input_ln",
    )(x, g.reshape(1, D), b.reshape(1, D))


def _inproj_kernel(h_ref, w_ref, wgt_ref, proj_ref, gt_ref):
    hb = h_ref[...].astype(BF16)
    proj_ref[...] = _dot(hb, w_ref[...])
    gt_ref[...] = _dot_nt(wgt_ref[...], hb)


def _inproj(h, w, wgt):
    T, D = h.shape
    tm = TM_INPROJ
    return pl.pallas_call(
        _inproj_kernel,
        out_shape=(jax.ShapeDtypeStruct((T, D_PROJ_PAD), F32),
                   jax.ShapeDtypeStruct((LANES, T), F32)),
        grid=(T // tm,),
        in_specs=[pl.BlockSpec((tm, D), lambda i: (i, 0)),
                  pl.BlockSpec((D, D_PROJ_PAD), lambda i: (0, 0), pipeline_mode=pl.Buffered(1)),
                  pl.BlockSpec((LANES, D), lambda i: (0, 0), pipeline_mode=pl.Buffered(1))],
        out_specs=(pl.BlockSpec((tm, D_PROJ_PAD), lambda i: (i, 0)),
                   pl.BlockSpec((LANES, tm), lambda i: (0, i))),
        compiler_params=_cparams(("parallel",)),
        name="inproj",
    )(h, w, wgt)


def _conv_kernel(prev_ref, x_ref, next_ref, w_ref, b_ref, o_ref, *, tm, tiles_per_seq):
    i = pl.program_id(0)
    pos = i % tiles_per_seq
    x = x_ref[...]
    prev = jnp.where(pos == 0, 0.0, prev_ref[...])
    nxt = jnp.where(pos == tiles_per_seq - 1, 0.0, next_ref[...])
    xe = jnp.concatenate([prev, x, nxt], axis=0)
    n = tm + 2 * SUBLANES
    acc = b_ref[...] + w_ref[2:3, :] * x
    for k in (0, 1, 3, 4):
        shifted = pltpu.roll(xe, (2 - k) % n, 0)
        acc = acc + w_ref[k:k + 1, :] * shifted[SUBLANES:SUBLANES + tm]
    o_ref[...] = _silu(acc)


def _conv_silu(proj, w, b, colblk, width, seq_len):
    T = proj.shape[0]
    tm = TM_CONV
    rb = tm // SUBLANES
    nrb = T // SUBLANES
    kern = functools.partial(_conv_kernel, tm=tm, tiles_per_seq=seq_len // tm)
    return pl.pallas_call(
        kern,
        out_shape=jax.ShapeDtypeStruct((T, width), F32),
        grid=(T // tm,),
        in_specs=[pl.BlockSpec((SUBLANES, width), lambda i: (jnp.maximum(i * rb - 1, 0), colblk)),
                  pl.BlockSpec((tm, width), lambda i: (i, colblk)),
                  pl.BlockSpec((SUBLANES, width), lambda i: (jnp.minimum((i + 1) * rb, nrb - 1), colblk)),
                  pl.BlockSpec((CONV_WIDTH, width), lambda i: (0, 0)),
                  pl.BlockSpec((1, width), lambda i: (0, 0))],
        out_specs=pl.BlockSpec((tm, width), lambda i: (i, 0)),
        compiler_params=_cparams(("parallel",)),
        name="conv_silu",
    )(proj, proj, proj, w, b.reshape(1, width))


def _cumsum_lanes(x, reverse):
    n = x.shape[1]
    r = _iota((3 * n, n), 0) % n
    c = _iota((3 * n, n), 1)
    tri3 = ((r >= c) if reverse else (r <= c)).astype(BF16)
    return _dot(_split3(x, axis=1), tri3)


def _cols_of(rows_list):
    used = sum(r.shape[0] for r in rows_list)
    pad = jnp.zeros((LANES - used, LANES), F32)
    return jnp.concatenate(list(rows_list) + [pad], axis=0).T


def _ssd_gates(gates_t, bias_cr, alog_cr, reverse):
    last = 0 if reverse else SSD_CHUNK - 1
    dt_r = _softplus(gates_t + bias_cr)
    acs_r = _cumsum_lanes(dt_r * (-jnp.exp(alog_cr)) * LOG2E, reverse)
    a_last = acs_r[:, last:last + 1]
    w_r = dt_r * jnp.exp2(a_last - acs_r)
    return acs_r - jnp.log2(dt_r), acs_r, w_r, jnp.exp2(a_last)


def _ssd_direction(x, gate_rows, acs_cols, st_ref, d, y_ref):
    reverse = d == 1
    n = SSD_CHUNK
    tri = _causal(n, reverse)
    lane_lo = _iota((n, LANES), 1) < SSD_HEAD_DIM
    acs_dt_r, _, w_r, dec = gate_rows

    for g in range(SSD_GROUPS):
        b0 = SSD_INNER + g * SSD_STATE
        c0 = SSD_INNER + SSD_GROUPS * SSD_STATE + g * SSD_STATE
        bg = x[:, b0:b0 + SSD_STATE]
        cgb = x[:, c0:c0 + SSD_STATE].astype(BF16)
        cb = _dot_nt(cgb, bg.astype(BF16))
        bg_t = bg.T
        for pair in range(4):
            pi = g * 4 + pair
            p0 = pi * LANES
            xp = x[:, p0:p0 + LANES]
            h0 = g * 8 + pair * 2
            ms, bhs, xms, cols = [], [], [], []
            for hh in range(2):
                h = h0 + hh
                xms.append(jnp.where(lane_lo if hh == 0 else jnp.logical_not(lane_lo), xp, 0.0).astype(BF16))
                col = jnp.broadcast_to(acs_cols[:, d * SSD_HEADS + h:d * SSD_HEADS + h + 1], (n, n))
                decay_dt = jnp.exp2(jnp.where(tri, col - acs_dt_r[h:h + 1, :], MASK_NEG))
                ms.append((cb * decay_dt).astype(BF16))
                bhs.append((bg_t * w_r[h:h + 1, :]).astype(BF16))
                cols.append(col)
            xcat = jnp.concatenate(xms, axis=0)
            st = st_ref[d, pi]
            from_state = jnp.exp2(jnp.where(lane_lo, cols[0], cols[1])) * _dot(cgb, st.astype(BF16))
            y_ref[:, p0:p0 + LANES] = _dot(jnp.concatenate(ms, axis=1), xcat) + from_state
            dec_row = jnp.where(lane_lo[0:1, :], dec[h0:h0 + 1, :], dec[h0 + 1:h0 + 2, :])
            st_ref[d, pi] = st * dec_row + _dot(jnp.concatenate(bhs, axis=1), xcat)


def _ssd_kernel(xf_ref, xb_ref, gtf_ref, gtb_ref, bcr_ref, acr_ref, yf_ref, yb_ref, st_ref):
    @pl.when(pl.program_id(1) == 0)
    def _():
        st_ref[...] = jnp.zeros_like(st_ref)

    gf = _ssd_gates(gtf_ref[...], bcr_ref[0:SSD_HEADS, :], acr_ref[0:SSD_HEADS, :], False)
    gb = _ssd_gates(gtb_ref[...], bcr_ref[SSD_HEADS:2 * SSD_HEADS, :], acr_ref[SSD_HEADS:2 * SSD_HEADS, :], True)
    acs_cols = _cols_of([gf[1], gb[1]])
    _ssd_direction(xf_ref[...], gf, acs_cols, st_ref, 0, yf_ref)
    _ssd_direction(xb_ref[...], gb, acs_cols, st_ref, 1, yb_ref)


def _ssd_scan(xbc_act, gates_t, bias_cr, alog_cr, batch):
    T = xbc_act.shape[0]
    n = T // batch // SSD_CHUNK
    fwd = lambda b, i: b * n + i
    bwd = lambda b, i: b * n + (n - 1 - i)
    const = lambda b, i: (0, 0)
    return pl.pallas_call(
        _ssd_kernel,
        out_shape=(jax.ShapeDtypeStruct((T, SSD_INNER), F32),
                   jax.ShapeDtypeStruct((T, SSD_INNER), F32)),
        grid=(batch, n),
        in_specs=[pl.BlockSpec((SSD_CHUNK, SSD_XBC), lambda b, i: (fwd(b, i), 0)),
                  pl.BlockSpec((SSD_CHUNK, SSD_XBC), lambda b, i: (bwd(b, i), 0)),
                  pl.BlockSpec((SSD_HEADS, SSD_CHUNK), lambda b, i: (0, fwd(b, i))),
                  pl.BlockSpec((SSD_HEADS, SSD_CHUNK), lambda b, i: (1, bwd(b, i))),
                  pl.BlockSpec((2 * SSD_HEADS, LANES), const),
                  pl.BlockSpec((2 * SSD_HEADS, LANES), const)],
        out_specs=(pl.BlockSpec((SSD_CHUNK, SSD_INNER), lambda b, i: (fwd(b, i), 0)),
                   pl.BlockSpec((SSD_CHUNK, SSD_INNER), lambda b, i: (bwd(b, i), 0))),
        scratch_shapes=[pltpu.VMEM((2, SSD_HEADS // 2, SSD_STATE, LANES), F32)],
        compiler_params=_cparams(("arbitrary", "arbitrary")),
        name="ssd_scan",
    )(xbc_act, xbc_act, gates_t, gates_t, bias_cr, alog_cr)


def _split3(x, axis=0):
    hi = x.astype(BF16)
    r1 = x - hi.astype(F32)
    mid = r1.astype(BF16)
    lo = (r1 - mid.astype(F32)).astype(BF16)
    return jnp.concatenate([hi, mid, lo], axis=axis)


def _hgrn2_cumsum_lhs(reverse):
    n = HG_SUB
    r_i = _iota((2 * n, n), 0)
    c_i = _iota((2 * n, n), 1)
    t = jnp.where(r_i < n, r_i, r_i - n)
    ref = (t // HG_REFBLK) * HG_REFBLK + (HG_REFBLK - 1 if reverse else 0)
    upto = jnp.where(r_i < n, t, ref)
    m = ((c_i >= upto) if reverse else (c_i <= upto)).astype(BF16)
    return jnp.concatenate([m, m, m], axis=1)


def _hgrn2_intra(q, f, v, lb, qmask, reverse):
    n = HG_SUB
    nblk = n // HG_REFBLK
    slab = 2 * SUBLANES
    tri = _causal(n, reverse)
    last = 0 if reverse else n - 1
    rowblk = _iota((slab, HG_DK), 0) // HG_REFBLK

    qh = _silu(q)
    key = (1.0 - lb) / (1.0 + jnp.exp(f))
    g = jnp.log2(1.0 - key)
    gg = _dot(_hgrn2_cumsum_lhs(reverse), _split3(g))
    gc = gg[:n]
    gref = gg[n:]
    qg = (qh * jnp.exp2(gc - gref)).astype(BF16)
    qs = (qh * jnp.exp2(gc)).astype(BF16)
    g_last = gc[last:last + 1, :]
    hk = gc - jnp.log2(key)
    kd = jnp.exp2(g_last - hk).astype(BF16)
    dec = jnp.exp2(g_last)
    vb = v.astype(BF16)

    o_intra = []
    for h in range(HG_HEADS):
        cols = slice(h * HG_DK, (h + 1) * HG_DK)
        hk_h = hk[:, cols]
        gc_h = gc[:, cols]
        k_parts = [None] * nblk
        far = []
        prev_edge = prev_r = prev_near = None
        for j in (range(nblk - 1, -1, -1) if reverse else range(nblk)):
            ref_row = j * HG_REFBLK + (HG_REFBLK - 1 if reverse else 0)
            rj = gc_h[ref_row:ref_row + 1, :]
            edge = (j * HG_REFBLK) // slab
            jb = j - edge * (slab // HG_REFBLK)
            if prev_r is not None:
                if edge != prev_edge:
                    far.append(prev_near)
                step = jnp.exp2(rj - prev_r)
                far = [p * step for p in far]
            live = (rowblk >= jb) if reverse else (rowblk <= jb)
            near = jnp.exp2(jnp.where(live, rj - hk_h[edge * slab:(edge + 1) * slab], MASK_NEG))
            pieces = [near] + far[::-1] if reverse else far + [near]
            k_parts[j] = jnp.concatenate([p.astype(BF16) for p in pieces], axis=0)
            prev_edge, prev_r, prev_near = edge, rj, near
        per_slab = slab // HG_REFBLK
        att_slabs = []
        for m in range(n // slab):
            q_m = jnp.tile(qg[m * slab:(m + 1) * slab, cols], (1, per_slab)) * qmask
            k_m = jnp.concatenate(k_parts[m * per_slab:(m + 1) * per_slab], axis=1)
            att_m = _dot_nt(q_m, k_m)
            dead = jnp.zeros((slab, n - att_m.shape[1]), F32)
            if dead.shape[1]:
                att_m = jnp.concatenate([dead, att_m] if reverse else [att_m, dead], axis=1)
            att_slabs.append(att_m)
        att = jnp.where(tri, jnp.concatenate(att_slabs, axis=0), 0.0).astype(BF16)
        o_intra.append(_dot(att, vb[:, cols]))
    return o_intra, qs, kd, dec, vb


def _hgrn2_state(intra, st_ref, d, o_ref, rows):
    o_intra, qs, kd, dec, vb = intra
    for h in range(HG_HEADS):
        cols = slice(h * HG_DK, (h + 1) * HG_DK)
        st = st_ref[d, h]
        o_ref[rows, cols] = o_intra[h] + _dot_nt(qs[:, cols], st.astype(BF16))
        st_ref[d, h] = st * dec[:, cols] + _dot_tn(vb[:, cols], kd[:, cols])


def _hgrn2_kernel(qf_ref, ff_ref, vf_ref, qb_ref, fb_ref, vb_ref, lb_ref, qmask_ref, of_ref, ob_ref, st_ref):
    @pl.when(pl.program_id(1) == 0)
    def _():
        st_ref[...] = jnp.zeros_like(st_ref)

    lb = lb_ref[...]
    qmask = qmask_ref[...]
    refs = ((qf_ref, ff_ref, vf_ref, of_ref), (qb_ref, fb_ref, vb_ref, ob_ref))
    nsub = SCAN_STEP // HG_SUB
    units = []
    for k in range(nsub):
        units.append((0, k))
        units.append((1, nsub - 1 - k))
    pending = None
    for d, sc in units:
        rows = slice(sc * HG_SUB, (sc + 1) * HG_SUB)
        q_ref, f_ref, v_ref, o_ref = refs[d]
        intra = _hgrn2_intra(q_ref[rows, :], f_ref[rows, :], v_ref[rows, :], lb, qmask, d == 1)
        if pending is not None:
            _hgrn2_state(*pending)
        pending = (intra, st_ref, d, o_ref, rows)
    _hgrn2_state(*pending)


def _hgrn2_scan(proj, lb, batch):
    T = proj.shape[0]
    n = T // batch // SCAN_STEP
    fwd = lambda b, i: b * n + i
    bwd = lambda b, i: b * n + (n - 1 - i)
    blk = (SCAN_STEP, HG_WIDTH)
    slab = 2 * SUBLANES
    nblk = slab // HG_REFBLK
    qmask = (jnp.arange(slab)[:, None] // HG_REFBLK == jnp.arange(nblk * HG_DK)[None, :] // HG_DK).astype(BF16)
    return pl.pallas_call(
        _hgrn2_kernel,
        out_shape=(jax.ShapeDtypeStruct((T, HG_WIDTH), F32),
                   jax.ShapeDtypeStruct((T, HG_WIDTH), F32)),
        grid=(batch, n),
        in_specs=[pl.BlockSpec(blk, lambda b, i: (fwd(b, i), COLBLK_HQ)),
                  pl.BlockSpec(blk, lambda b, i: (fwd(b, i), COLBLK_HF_FWD)),
                  pl.BlockSpec(blk, lambda b, i: (fwd(b, i), COLBLK_HI)),
                  pl.BlockSpec(blk, lambda b, i: (bwd(b, i), COLBLK_HQ)),
                  pl.BlockSpec(blk, lambda b, i: (bwd(b, i), COLBLK_HF_BWD)),
                  pl.BlockSpec(blk, lambda b, i: (bwd(b, i), COLBLK_HI)),
                  pl.BlockSpec((1, HG_WIDTH), lambda b, i: (0, 0)),
                  pl.BlockSpec((slab, nblk * HG_DK), lambda b, i: (0, 0))],
        out_specs=(pl.BlockSpec(blk, lambda b, i: (fwd(b, i), 0)),
                   pl.BlockSpec(blk, lambda b, i: (bwd(b, i), 0))),
        scratch_shapes=[pltpu.VMEM((2, HG_HEADS, HG_DK, HG_DK), F32)],
        compiler_params=_cparams(("arbitrary", "arbitrary")),
        name="hgrn2_scan",
    )(proj, proj, proj, proj, proj, proj, lb.reshape(1, HG_WIDTH), qmask)


def _mlstm_kernel(batch, qkf_ref, vf_ref, qkb_ref, vb_ref, *rest):
    gt_refs = (rest[:batch], rest[batch:2 * batch])
    bcr_ref, of_ref, ob_ref, c_ref, m_ref, mlane_ref = rest[2 * batch:]

    @pl.when(pl.program_id(0) == 0)
    def _():
        c_ref[...] = jnp.zeros_like(c_ref)
        m_ref[...] = jnp.zeros_like(m_ref)
        mlane_ref[...] = jnp.zeros_like(mlane_ref)

    n = SCAN_STEP
    ones = jnp.ones((n, ML_DH), BF16)
    io_refs = ((qkf_ref, vf_ref, of_ref), (qkb_ref, vb_ref, ob_ref))
    tris = (_causal(n, False), _causal(n, True))
    row_i = _iota((n, LANES), 0)

    gates = {}
    for d in range(2):
        reverse = d == 1
        last = 0 if reverse else n - 1
        rows = []
        for b in range(batch):
            g_r = gt_refs[d][b][...] + bcr_ref[...]
            bc = _cumsum_lanes(_log_sigmoid(g_r) * LOG2E, reverse)[SUBLANES:]
            ig = g_r[:SUBLANES] * LOG2E
            rows.append((bc, ig, ig - bc))
        tb = _cols_of([r[0] for r in rows])
        pm = _cols_of([r[2] for r in rows])
        for sh in (1, 2, 4, 8, 16, 32, 64):
            shifted = pltpu.roll(pm, (n - sh) if reverse else sh, 0)
            valid = (row_i < n - sh) if reverse else (row_i >= sh)
            pm = jnp.maximum(pm, jnp.where(valid, shifted, -jnp.inf))
        m_prev_lane = mlane_ref[d][0:1, :]
        floor = jnp.where((row_i > 0) if reverse else (row_i < n - 1), MASK_NEG * LOG2E, -jnp.inf)
        mt = jnp.maximum(jnp.maximum(tb + pm, tb + m_prev_lane), floor)
        b_last_lane = tb[last:last + 1, :]
        m_new_lane = jnp.maximum(b_last_lane + m_prev_lane, b_last_lane + pm[last:last + 1, :])
        mlane_ref[d] = jnp.broadcast_to(m_new_lane, (SUBLANES, LANES))
        bm = tb - mt
        for b in range(batch):
            bc, ig, u = rows[b]
            m_prev = m_ref[b * 2 + d]
            b_last = bc[:, last:last + 1]
            m_new = jnp.maximum(b_last + m_prev[:, 0:1], b_last + jnp.max(u, axis=1, keepdims=True))
            w_s = jnp.exp2(b_last - bc + ig - m_new)
            dec = jnp.exp2(b_last + m_prev[:, 0:1] - m_new)
            m_ref[b * 2 + d] = jnp.broadcast_to(m_new, (SUBLANES, LANES))
            gates[b, d] = (u, bm, mt, m_prev, w_s, dec)

    units = [(b, d, h) for h in range(ML_HEADS) for b in range(batch) for d in range(2)]
    for b, d, h in units:
        u, bm, mt, m_prev, w_s, dec = gates[b, d]
        qk_ref, v_ref, o_ref = io_refs[d]
        r = d * ML_HEADS + h
        c = b * SUBLANES + r
        qf = qk_ref[b, :, h * ML_DH:(h + 1) * ML_DH] * (ML_DH ** -0.5)
        k = qk_ref[b, :, ML_WIDTH + h * ML_DH:ML_WIDTH + (h + 1) * ML_DH]
        ve = jnp.concatenate([v_ref[b, :, h * ML_DH:(h + 1) * ML_DH].astype(BF16), ones], axis=1)
        b_minus_m = jnp.broadcast_to(bm[:, c:c + 1], (n, n))
        m_t = jnp.broadcast_to(mt[:, c:c + 1], (n, n))
        w_intra = jnp.exp2(jnp.where(tris[d], b_minus_m + u[r:r + 1, :], MASK_NEG))
        w_inter = jnp.exp2(b_minus_m + m_prev[r:r + 1, 0:1])
        s = (_dot_nt(qf.astype(BF16), k.astype(BF16)) * w_intra).astype(BF16)
        cst = c_ref[b, d, h]
        numden = _dot(jnp.concatenate([s, (qf * w_inter).astype(BF16)], axis=1),
                      jnp.concatenate([ve, cst.astype(BF16)], axis=0))
        den = numden[:, ML_DH:]
        o_ref[b, :, h * ML_DH:(h + 1) * ML_DH] = numden[:, :ML_DH] / jnp.maximum(jnp.abs(den), jnp.exp2(-m_t))
        kw_t = (k.T * w_s[r:r + 1, :]).astype(BF16)
        c_ref[b, d, h] = dec[r:r + 1, :] * cst + _dot(kw_t, ve)


def _mlstm_scan(mqk_act, proj, gates_t, bias_cr, batch):
    T = proj.shape[0]
    seq = T // batch
    n = seq // SCAN_STEP
    gt_rowblk = GATE_IG // (2 * SUBLANES)
    fwd3 = lambda cb: (lambda i: (0, i, cb))
    bwd3 = lambda cb: (lambda i: (0, n - 1 - i, cb))
    gt_fwd = [pl.BlockSpec((2 * SUBLANES, SCAN_STEP), functools.partial(lambda b, i: (gt_rowblk, b * n + i), b))
              for b in range(batch)]
    gt_bwd = [pl.BlockSpec((2 * SUBLANES, SCAN_STEP),
                           functools.partial(lambda b, i: (gt_rowblk, b * n + n - 1 - i), b))
              for b in range(batch)]
    qk3 = mqk_act.reshape(batch, seq, 2 * ML_WIDTH)
    proj3 = proj.reshape(batch, seq, D_PROJ_PAD)
    o_f, o_b = pl.pallas_call(
        functools.partial(_mlstm_kernel, batch),
        out_shape=(jax.ShapeDtypeStruct((batch, seq, ML_WIDTH), F32),
                   jax.ShapeDtypeStruct((batch, seq, ML_WIDTH), F32)),
        grid=(n,),
        in_specs=[pl.BlockSpec((batch, SCAN_STEP, 2 * ML_WIDTH), fwd3(0)),
                  pl.BlockSpec((batch, SCAN_STEP, ML_WIDTH), fwd3(COLBLK_MV)),
                  pl.BlockSpec((batch, SCAN_STEP, 2 * ML_WIDTH), bwd3(0)),
                  pl.BlockSpec((batch, SCAN_STEP, ML_WIDTH), bwd3(COLBLK_MV))]
                 + gt_fwd + gt_bwd
                 + [pl.BlockSpec((2 * SUBLANES, LANES), lambda i: (0, 0))],
        out_specs=(pl.BlockSpec((batch, SCAN_STEP, ML_WIDTH), fwd3(0)),
                   pl.BlockSpec((batch, SCAN_STEP, ML_WIDTH), bwd3(0))),
        scratch_shapes=[pltpu.VMEM((batch, 2, ML_HEADS, ML_DH, 2 * ML_DH), F32),
                        pltpu.VMEM((batch * 2, SUBLANES, LANES), F32),
                        pltpu.VMEM((2, SUBLANES, LANES), F32)],
        compiler_params=_cparams(("arbitrary",)),
        name="mlstm_scan",
    )(qk3, proj3, qk3, proj3, *([gates_t] * (2 * batch)), bias_cr)
    return o_f.reshape(T, ML_WIDTH), o_b.reshape(T, ML_WIDTH)


def _outproj_kernel(yf_ref, yb_ref, xs_ref, z_ref, of_ref, ob_ref, hg_ref, mf_ref, mb_ref, mo_ref,
                    h_ref, dskip_ref, ssdw_ref, hgw_ref, mlw_ref, wout_ref, g_ref, b_ref, o_ref):
    y = (yf_ref[...] + yb_ref[...] + xs_ref[...] * dskip_ref[...]) * _silu(z_ref[...])
    gw = SSD_INNER // SSD_GROUPS
    acc = None
    for g in range(SSD_GROUPS):
        seg = y[:, g * gw:(g + 1) * gw]
        seg = seg * lax.rsqrt(jnp.mean(seg * seg, -1, keepdims=True) + RMS_EPS) * ssdw_ref[:, g * gw:(g + 1) * gw]
        part = _dot(seg.astype(BF16), wout_ref[g * gw:(g + 1) * gw, :])
        acc = part if acc is None else acc + part
    o = of_ref[...] + ob_ref[...]
    hgate = _silu(hg_ref[...])
    m = mf_ref[...] + mb_ref[...]
    mgate = _sigmoid(mo_ref[...])
    for h in range(HG_HEADS):
        cols = slice(h * HG_DK, (h + 1) * HG_DK)
        seg = o[:, cols]
        seg = seg * lax.rsqrt(jnp.mean(seg * seg, -1, keepdims=True) + RMS_EPS) * hgw_ref[:, cols] * hgate[:, cols]
        r0 = SSD_INNER + h * HG_DK
        acc = acc + _dot(seg.astype(BF16), wout_ref[r0:r0 + HG_DK, :])
    for h in range(ML_HEADS):
        cols = slice(h * ML_DH, (h + 1) * ML_DH)
        seg = m[:, cols]
        seg = seg - jnp.mean(seg, -1, keepdims=True)
        seg = seg * lax.rsqrt(jnp.mean(seg * seg, -1, keepdims=True) + LN_EPS) * mlw_ref[:, cols] * mgate[:, cols]
        r0 = SSD_INNER + HG_WIDTH + h * ML_DH
        acc = acc + _dot(seg.astype(BF16), wout_ref[r0:r0 + ML_DH, :])
    o_ref[...] = _layer_norm(ALPHA * h_ref[...] + acc, g_ref[...], b_ref[...])


def _outproj(y_f, y_b, xbc_act, proj, o_f, o_b, m_f, m_b, h, dskip, ssd_w, hg_w, ml_w, w_out, g, b):
    T, D = h.shape
    tm = TM_OUT
    row = lambda width, cb: pl.BlockSpec((tm, width), lambda i: (i, cb))
    par = lambda width: pl.BlockSpec((1, width), lambda i: (0, 0))
    return pl.pallas_call(
        _outproj_kernel,
        out_shape=jax.ShapeDtypeStruct((T, D), F32),
        grid=(T // tm,),
        in_specs=[row(SSD_INNER, 0), row(SSD_INNER, 0), row(SSD_INNER, 0), row(SSD_INNER, COLBLK_Z),
                  row(HG_WIDTH, 0), row(HG_WIDTH, 0), row(HG_WIDTH, COLBLK_HG),
                  row(ML_WIDTH, 0), row(ML_WIDTH, 0), row(ML_WIDTH, COLBLK_MO),
                  row(D, 0), par(SSD_INNER), par(SSD_INNER), par(HG_WIDTH), par(ML_WIDTH),
                  pl.BlockSpec((D_MIX, D), lambda i: (0, 0), pipeline_mode=pl.Buffered(1)),
                  par(D), par(D)],
        out_specs=pl.BlockSpec((tm, D), lambda i: (i, 0)),
        compiler_params=_cparams(("parallel",)),
        name="outproj_ln1",
    )(y_f, y_b, xbc_act, proj, o_f, o_b, proj, m_f, m_b, proj, h, dskip, ssd_w, hg_w, ml_w, w_out,
      g.reshape(1, D), b.reshape(1, D))


def _ffn_kernel(h_ref, w1_ref, w3_ref, w2_ref, g_ref, b_ref, o_ref, acc_ref):
    h = h_ref[...]
    hb = h.astype(BF16)
    for c in range(D_FF_DENSE // TF_FFN):
        cols = slice(c * TF_FFN, (c + 1) * TF_FFN)
        a = _dot(hb, w1_ref[:, cols])
        u = _dot(hb, w3_ref[:, cols])
        part = _dot((_silu(a) * u).astype(BF16), w2_ref[cols, :])
        if c == 0:
            acc_ref[...] = part
        else:
            acc_ref[...] += part
    o_ref[...] = _layer_norm(ALPHA * h + acc_ref[...], g_ref[...], b_ref[...])


def _dense_ffn(h, w1, w3, w2, g, b):
    T, D = h.shape
    tm = TM_FFN
    resident = lambda shape: pl.BlockSpec(shape, lambda i: (0, 0), pipeline_mode=pl.Buffered(1))
    return pl.pallas_call(
        _ffn_kernel,
        out_shape=jax.ShapeDtypeStruct((T, D), F32),
        grid=(T // tm,),
        in_specs=[pl.BlockSpec((tm, D), lambda i: (i, 0)),
                  resident((D, D_FF_DENSE)), resident((D, D_FF_DENSE)), resident((D_FF_DENSE, D)),
                  pl.BlockSpec((1, D), lambda i: (0, 0)), pl.BlockSpec((1, D), lambda i: (0, 0))],
        out_specs=pl.BlockSpec((tm, D), lambda i: (i, 0)),
        scratch_shapes=[pltpu.VMEM((tm, D), F32)],
        compiler_params=_cparams(("parallel",)),
        name="dense_ffn_ln2",
    )(h, w1, w3, w2, g.reshape(1, D), b.reshape(1, D))


def _router_kernel(h_ref, w_ref, idx_ref, gate_ref):
    logits = _dot_hi(h_ref[...], w_ref[...])
    lane = _iota(logits.shape, 1)
    lane_f = lane.astype(F32)
    x = jnp.where(lane < N_EXPERTS, logits, -jnp.inf)
    m1 = jnp.max(x, axis=1, keepdims=True)
    i1 = jnp.min(jnp.where(x == m1, lane_f, float(LANES)), axis=1, keepdims=True)
    x2 = jnp.where(lane_f == i1, -jnp.inf, x)
    m2 = jnp.max(x2, axis=1, keepdims=True)
    i2 = jnp.min(jnp.where(x2 == m2, lane_f, float(LANES)), axis=1, keepdims=True)
    e2 = jnp.exp(m2 - m1)
    g1 = 1.0 / (1.0 + e2)
    g2 = e2 / (1.0 + e2)
    idx_ref[...] = jnp.where(lane == 0, i1, jnp.where(lane == 1, i2, 0.0)).astype(jnp.int32)
    gate_ref[...] = jnp.where(lane == 0, g1, jnp.where(lane == 1, g2, 0.0))


def _router(h, w_pad):
    T, D = h.shape
    tm = TM_ROUTER
    return pl.pallas_call(
        _router_kernel,
        out_shape=(jax.ShapeDtypeStruct((T, LANES), jnp.int32),
                   jax.ShapeDtypeStruct((T, LANES), F32)),
        grid=(T // tm,),
        in_specs=[pl.BlockSpec((tm, D), lambda i: (i, 0)),
                  pl.BlockSpec((D, LANES), lambda i: (0, 0))],
        out_specs=(pl.BlockSpec((tm, LANES), lambda i: (i, 0)),
                   pl.BlockSpec((tm, LANES), lambda i: (i, 0))),
        compiler_params=_cparams(("parallel",)),
        name="moe_router",
    )(h, w_pad)


def _gather_rows(src, idx):
    n = idx.shape[0]
    D = src.shape[1]
    per_worker = n // SC_WORKERS
    n_chunks = per_worker // SC_GATHER_CHUNK
    assert n_chunks * SC_GATHER_CHUNK * SC_WORKERS == n and n_chunks % 2 == 0
    mesh = plsc.VectorSubcoreMesh(core_axis_name="c", subcore_axis_name="s")

    @functools.partial(
        pl.kernel, mesh=mesh,
        out_type=jax.ShapeDtypeStruct((n, D), src.dtype),
        scratch_types=[pltpu.VMEM((2, SC_GATHER_CHUNK), jnp.int32),
                       pltpu.VMEM((2, SC_GATHER_CHUNK, D), src.dtype),
                       pltpu.SemaphoreType.DMA((2,))],
        name="moe_row_gather",
    )
    def gather(src_hbm, idx_hbm, out_hbm, idx_v, rows_v, sems):
        worker = lax.axis_index("s") * SC_CORES + lax.axis_index("c")
        base = worker * per_worker

        def gather_copy(slot):
            return pltpu.make_async_copy(src_hbm.at[idx_v.at[slot]], rows_v.at[slot], sems.at[slot])

        def start(j, slot):
            pltpu.sync_copy(idx_hbm.at[pl.ds(base + j * SC_GATHER_CHUNK, SC_GATHER_CHUNK)], idx_v.at[slot])
            gather_copy(slot).start()

        def finish(j, slot):
            gather_copy(slot).wait()
            pltpu.sync_copy(rows_v.at[slot], out_hbm.at[pl.ds(base + j * SC_GATHER_CHUNK, SC_GATHER_CHUNK)])

        start(0, 0)

        @pl.loop(0, n_chunks, step=2)
        def _(j):
            start(j + 1, 1)
            finish(j, 0)

            @pl.when(j + 2 < n_chunks)
            def _():
                start(j + 2, 0)

            finish(j + 1, 1)

    return gather(src, idx)


def _scatter_rows(init, vals, idx):
    n_out, W = init.shape
    n = idx.shape[0]
    out_per = n_out // SC_SUBCORES
    per = n // SC_SUBCORES
    n_chunks = per // SC_SCATTER_CHUNK
    init_chunks = out_per // SC_INIT_CHUNK
    assert init_chunks * SC_INIT_CHUNK * SC_SUBCORES == n_out and n_chunks * SC_SCATTER_CHUNK * SC_SUBCORES == n
    mesh = plsc.VectorSubcoreMesh(core_axis_name="c", subcore_axis_name="s", num_cores=1)

    @functools.partial(
        pl.kernel, mesh=mesh,
        out_type=jax.ShapeDtypeStruct((n_out, W), init.dtype),
        scratch_types=[pltpu.VMEM((SC_INIT_CHUNK, W), init.dtype),
                       pltpu.VMEM((SC_SCATTER_CHUNK,), jnp.int32),
                       pltpu.VMEM((SC_SCATTER_CHUNK, W), init.dtype)],
        name="moe_row_scatter",
    )
    def scatter(init_hbm, vals_hbm, idx_hbm, out_hbm, init_v, idx_v, vals_v):
        worker = lax.axis_index("s")

        @pl.loop(0, init_chunks)
        def _(j):
            off = worker * out_per + j * SC_INIT_CHUNK
            pltpu.sync_copy(init_hbm.at[pl.ds(off, SC_INIT_CHUNK)], init_v)
            pltpu.sync_copy(init_v, out_hbm.at[pl.ds(off, SC_INIT_CHUNK)])

        plsc.subcore_barrier()

        @pl.loop(0, n_chunks)
        def _(j):
            off = worker * per + j * SC_SCATTER_CHUNK
            pltpu.sync_copy(idx_hbm.at[pl.ds(off, SC_SCATTER_CHUNK)], idx_v)
            pltpu.sync_copy(vals_hbm.at[pl.ds(off, SC_SCATTER_CHUNK)], vals_v)
            pltpu.sync_copy(vals_v, out_hbm.at[idx_v])

    return scatter(init, vals, idx)


def _expert_kernel(be_ref, nact_ref, x_ref, w1_ref, w3_ref, w2_ref, o_ref):
    i = pl.program_id(0)
    f = pl.program_id(1)

    @pl.when(f == 0)
    def _():
        o_ref[...] = jnp.zeros_like(o_ref)

    @pl.when(i < nact_ref[0])
    def _():
        xb = x_ref[...].astype(BF16)
        a = _dot(xb, w1_ref[...].astype(BF16))
        u = _dot(xb, w3_ref[...].astype(BF16))
        o_ref[...] += _dot((_silu(a) * u).astype(BF16), w2_ref[...].astype(BF16))


def _expert_ffn(xs, block_expert, n_active, w1, w3, w2, layer):
    n_rows, D = xs.shape
    n_blocks = n_rows // MOE_BM
    n_ff = D_FF_EXPERT // MOE_TF

    def widx(i, f, be, nact):
        live = i < nact[0]
        return be[jnp.minimum(i, nact[0] - 1)], jnp.where(live, f, n_ff - 1)

    def w13_map(i, f, be, nact):
        e, ff = widx(i, f, be, nact)
        return layer, e, 0, ff

    def w2_map(i, f, be, nact):
        e, ff = widx(i, f, be, nact)
        return layer, e, ff, 0

    grid_spec = pltpu.PrefetchScalarGridSpec(
        num_scalar_prefetch=2,
        grid=(n_blocks, n_ff),
        in_specs=[pl.BlockSpec((MOE_BM, D), lambda i, f, be, nact: (i, 0)),
                  pl.BlockSpec((None, None, D, MOE_TF), w13_map),
                  pl.BlockSpec((None, None, D, MOE_TF), w13_map),
                  pl.BlockSpec((None, None, MOE_TF, D), w2_map)],
        out_specs=pl.BlockSpec((MOE_BM, D), lambda i, f, be, nact: (i, 0)),
    )
    return pl.pallas_call(
        _expert_kernel,
        out_shape=jax.ShapeDtypeStruct((n_rows, D), F32),
        grid_spec=grid_spec,
        compiler_params=_cparams(("arbitrary", "arbitrary")),
        name="moe_experts",
    )(block_expert, n_active, xs, w1, w3, w2)


def _combine_kernel(y0_ref, y1_ref, gate_ref, h_ref, g_ref, b_ref, o_ref):
    gate = gate_ref[...]
    f = y0_ref[...] * gate[:, 0:1] + y1_ref[...] * gate[:, 1:2]
    o_ref[...] = _layer_norm(ALPHA * h_ref[...] + f, g_ref[...], b_ref[...])


def _combine(y_slots, gates, h, g, b):
    T, D = h.shape
    tm = TM_COMBINE
    return pl.pallas_call(
        _combine_kernel,
        out_shape=jax.ShapeDtypeStruct((T, D), F32),
        grid=(T // tm,),
        in_specs=[pl.BlockSpec((tm, D), lambda i: (i, 0)),
                  pl.BlockSpec((tm, D), lambda i: (i + T // tm, 0)),
                  pl.BlockSpec((tm, LANES), lambda i: (i, 0)),
                  pl.BlockSpec((tm, D), lambda i: (i, 0)),
                  pl.BlockSpec((1, D), lambda i: (0, 0)),
                  pl.BlockSpec((1, D), lambda i: (0, 0))],
        out_specs=pl.BlockSpec((tm, D), lambda i: (i, 0)),
        compiler_params=_cparams(("parallel",)),
        name="moe_combine_ln2",
    )(y_slots, y_slots, gates, h, g.reshape(1, D), b.reshape(1, D))


def _moe_ffn(h, router_w, w1, w3, w2, layer, g, b):
    T, D = h.shape
    n_pairs = T * TOP_K
    n_blocks = n_pairs // MOE_BM + N_EXPERTS
    n_rows = n_blocks * MOE_BM
    w_pad = jnp.pad(router_w.astype(F32), ((0, 0), (0, LANES - N_EXPERTS)))
    idx_pad, gates_pad = _router(h, w_pad)
    expert_ids = idx_pad[:, :TOP_K].reshape(-1)
    onehot = (expert_ids[:, None] == jnp.arange(N_EXPERTS, dtype=jnp.int32)[None, :]).astype(jnp.int32)
    running = jnp.cumsum(onehot, axis=0)
    counts = running[-1]
    rank = jnp.sum(onehot * running, axis=1) - 1
    padded = (counts + MOE_BM - 1) // MOE_BM * MOE_BM
    ends = jnp.cumsum(padded)
    pos = (ends - padded)[expert_ids] + rank
    token_ids = jnp.arange(n_pairs, dtype=jnp.int32) // TOP_K
    default = jnp.broadcast_to((jnp.arange(n_rows, dtype=jnp.int32) % T)[:, None], (n_rows, SC_ROW_WORDS))
    row_token = _scatter_rows(default, jnp.broadcast_to(token_ids[:, None], (n_pairs, SC_ROW_WORDS)),
                              pos.astype(jnp.int32))[:, 0]
    block_start = jnp.arange(n_blocks, dtype=jnp.int32) * MOE_BM
    block_expert = jnp.minimum(jnp.searchsorted(ends, block_start, side='right'),
                               N_EXPERTS - 1).astype(jnp.int32)
    n_active = (ends[-1:] // MOE_BM).astype(jnp.int32)

    xs = _gather_rows(h, row_token)
    ys = _expert_ffn(xs, block_expert, n_active, w1, w3, w2, layer)
    slot_major = pos.astype(jnp.int32).reshape(T, TOP_K).T.reshape(-1)
    y_slots = _gather_rows(ys, slot_major)
    return _combine(y_slots, gates_pad, h, g, b)


def _relayout_w_in(w):
    sizes = (SSD_INNER, SSD_XBC, 2 * SSD_HEADS, HG_WIDTH, 2 * HG_WIDTH, HG_WIDTH, HG_WIDTH,
             2 * ML_WIDTH, ML_WIDTH, ML_WIDTH, 2 * ML_HEADS, 2 * ML_HEADS)
    offs = [0]
    for s in sizes:
        offs.append(offs[-1] + s)
    (z, xbc, dt, hq, hf, hi, hg, mqk, mv, mo, mig, mfg) = [w[:, offs[k]:offs[k + 1]] for k in range(len(sizes))]
    pad = jnp.zeros((w.shape[0], LANES - 2 * SSD_HEADS - 4 * ML_HEADS), w.dtype)
    wn = jnp.concatenate([z, mqk, hf, xbc, hq, hi, hg, mv, mo], axis=1).astype(BF16)
    wgt = jnp.concatenate([dt, mig, mfg, pad], axis=1).astype(BF16).T
    return wn, wgt


def _gate_rows(vals_by_offset):
    row = jnp.zeros((LANES,), F32)
    for off, v in vals_by_offset:
        row = lax.dynamic_update_slice(row, v.reshape(-1).astype(F32), (off,))
    return jnp.broadcast_to(row[:, None], (LANES, LANES))


def kernel(x, ln_in_g, ln_in_b, w_in, ssd_conv_w, ssd_conv_b, ssd_dt_bias, ssd_a_log, ssd_d, ssd_norm_w, hg_lb_logits, hg_norm_w, ml_conv_w, ml_conv_b, ml_ig_bias, ml_fg_bias, ml_norm_w, w_out, ln1_g, ln1_b, ln2_g, ln2_b, ffn_w1, ffn_w3, ffn_w2, moe_router, moe_w1, moe_w3, moe_w2):
    batch, seq_len, D = x.shape
    T = batch * seq_len
    f32 = lambda t: t.astype(F32)
    lb_soft = jax.nn.softmax(f32(hg_lb_logits), axis=0)
    lb_all = jnp.cumsum(lb_soft, axis=0) - lb_soft[0]

    h = _input_ln(f32(x).reshape(T, D), f32(ln_in_g), f32(ln_in_b))
    for l in range(DEPTH):
        w, wgt = _relayout_w_in(w_in[l])
        bias_cr = _gate_rows(((GATE_DT, ssd_dt_bias[l]), (GATE_IG, ml_ig_bias[l]), (GATE_FG, ml_fg_bias[l])))
        alog_cr = _gate_rows(((GATE_DT, ssd_a_log[l]),))

        proj, gates_t = _inproj(h, w, wgt)
        xbc_act = _conv_silu(proj, f32(ssd_conv_w[l]), f32(ssd_conv_b[l]), COLBLK_XBC, SSD_XBC, seq_len)
        mqk_act = _conv_silu(proj, f32(ml_conv_w[l]), f32(ml_conv_b[l]), COLBLK_MQK, 2 * ML_WIDTH, seq_len)
        y_f, y_b = _ssd_scan(xbc_act, gates_t, bias_cr[GATE_DT:GATE_DT + 2 * SSD_HEADS],
                             alog_cr[GATE_DT:GATE_DT + 2 * SSD_HEADS], batch)
        o_f, o_b = _hgrn2_scan(proj, lb_all[l], batch)
        m_f, m_b = _mlstm_scan(mqk_act, proj, gates_t, bias_cr[GATE_IG:GATE_IG + 4 * ML_HEADS], batch)
        dskip = jnp.repeat(f32(ssd_d[l]), SSD_HEAD_DIM).reshape(1, SSD_INNER)
        h = _outproj(y_f, y_b, xbc_act, proj, o_f, o_b, m_f, m_b, h, dskip,
                     f32(ssd_norm_w[l]).reshape(1, -1), f32(hg_norm_w[l]).reshape(1, -1),
                     f32(ml_norm_w[l]).reshape(1, -1), w_out[l].astype(BF16), f32(ln1_g[l]), f32(ln1_b[l]))
        if l % 2 == 0:
            j = l // 2
            h = _dense_ffn(h, ffn_w1[j].astype(BF16), ffn_w3[j].astype(BF16), ffn_w2[j].astype(BF16),
                           f32(ln2_g[l]), f32(ln2_b[l]))
        else:
            j = l // 2
            h = _moe_ffn(h, moe_router[j], f32(moe_w1), f32(moe_w3), f32(moe_w2), j,
                         f32(ln2_g[l]), f32(ln2_b[l]))
    return h.reshape(batch, seq_len, D).astype(x.dtype)
```

```python
import functools

import jax
import jax.numpy as jnp
from jax import lax
from jax.experimental import pallas as pl
from jax.experimental.pallas import tpu as pltpu
from jax.experimental.pallas import tpu_sc as plsc

F32 = jnp.float32
BF16 = jnp.bfloat16
HIGHEST = lax.Precision.HIGHEST

D_MODEL = 1024
DEPTH = 4
D_MIX = 2 * D_MODEL
SSD_HEADS = 16
SSD_INNER = 1024
SSD_HEAD_DIM = 64
SSD_GROUPS = 2
SSD_STATE = 128
SSD_XBC = 1536
SSD_CHUNK = 128
HG_HEADS = 4
HG_WIDTH = 512
HG_DK = 128
ML_HEADS = 4
ML_WIDTH = 512
ML_DH = 128
CONV_WIDTH = 5
D_FF_DENSE = 2816
N_EXPERTS = 8
TOP_K = 2
D_FF_EXPERT = 3584
LN_EPS = 1e-5
RMS_EPS = 1e-6
MASK_NEG = -1e4
ALPHA = (2.0 * DEPTH) ** 0.25
LOG2E = 1.4426950408889634

LANES = 128
SUBLANES = 8
VMEM_LIMIT = 56 * 1024 * 1024

D_PROJ_PAD = 7168
COLBLK_Z = 0
COLBLK_MQK = 1
COLBLK_XBC = 2
COLBLK_HF_FWD = 4
COLBLK_HF_BWD = 5
COLBLK_HQ = 9
COLBLK_HI = 10
COLBLK_HG = 11
COLBLK_MV = 12
COLBLK_MO = 13
GATE_DT = 0
GATE_IG = 32
GATE_FG = 40

TM_LN = 512
TM_INPROJ = 512
TM_CONV = 512
SSD_STEP = 256
SCAN_STEP = 128
HG_STEP = 256
HG_SUB = 64
HG_REFBLK = 4
TM_OUT = 512
TM_FFN = 512
TF_FFN = 256
TM_ROUTER = 512
MOE_BM = 1024
MOE_TF = 512
SC_CORES = 2
SC_SUBCORES = 16
SC_WORKERS = SC_CORES * SC_SUBCORES
SC_SCATTER_CHUNK = 128
SC_ROW_WORDS = 128
SC_INIT_CHUNK = 512
SC_GATHER_CHUNK = 32
TM_COMBINE = 512


def _cparams(sem):
    return pltpu.CompilerParams(dimension_semantics=sem, vmem_limit_bytes=VMEM_LIMIT)


def _iota(shape, dim):
    return lax.broadcasted_iota(jnp.int32, shape, dim)


def _sigmoid(x):
    return 1.0 / (1.0 + jnp.exp(-x))


def _silu(x):
    return x * _sigmoid(x)


def _softplus(x):
    return jnp.maximum(x, 0.0) + jnp.log1p(jnp.exp(-jnp.abs(x)))


def _log_sigmoid(x):
    return -_softplus(-x)


def _layer_norm(x, g, b):
    mu = jnp.mean(x, -1, keepdims=True)
    xc = x - mu
    var = jnp.mean(xc * xc, -1, keepdims=True)
    return xc * lax.rsqrt(var + LN_EPS) * g + b


def _causal(n, reverse):
    r = _iota((n, n), 0)
    c = _iota((n, n), 1)
    return (c >= r) if reverse else (c <= r)


def _dot(a, b):
    return jnp.dot(a, b, preferred_element_type=F32)


def _dot_nt(a, b):
    return lax.dot_general(a, b, (((1,), (1,)), ((), ())), preferred_element_type=F32)


def _dot_tn(a, b):
    return lax.dot_general(a, b, (((0,), (0,)), ((), ())), preferred_element_type=F32)


def _dot_hi(a, b):
    return jnp.dot(a, b, precision=HIGHEST, preferred_element_type=F32)


def _ln_kernel(x_ref, g_ref, b_ref, o_ref):
    o_ref[...] = _layer_norm(x_ref[...], g_ref[...], b_ref[...])


def _input_ln(x, g, b):
    T, D = x.shape
    return pl.pallas_call(
        _ln_kernel,
        out_shape=jax.ShapeDtypeStruct((T, D), F32),
        grid=(T // TM_LN,),
        in_specs=[pl.BlockSpec((TM_LN, D), lambda i: (i, 0)),
                  pl.BlockSpec((1, D), lambda i: (0, 0)),
                  pl.BlockSpec((1, D), lambda i: (0, 0))],
        out_specs=pl.BlockSpec((TM_LN, D), lambda i: (i, 0)),
        compiler_params=_cparams(("parallel",)),
        name="input_ln",
    )(x, g.reshape(1, D), b.reshape(1, D))


def _inproj_kernel(h_ref, w_ref, wgt_ref, proj_ref, gt_ref):
    hb = h_ref[...].astype(BF16)
    proj_ref[...] = _dot(hb, w_ref[...])
    gt_ref[...] = _dot_nt(wgt_ref[...], hb)


def _inproj(h, w, wgt):
    T, D = h.shape
    tm = TM_INPROJ
    return pl.pallas_call(
        _inproj_kernel,
        out_shape=(jax.ShapeDtypeStruct((T, D_PROJ_PAD), F32),
                   jax.ShapeDtypeStruct((LANES, T), F32)),
        grid=(T // tm,),
        in_specs=[pl.BlockSpec((tm, D), lambda i: (i, 0)),
                  pl.BlockSpec((D, D_PROJ_PAD), lambda i: (0, 0), pipeline_mode=pl.Buffered(1)),
                  pl.BlockSpec((LANES, D), lambda i: (0, 0), pipeline_mode=pl.Buffered(1))],
        out_specs=(pl.BlockSpec((tm, D_PROJ_PAD), lambda i: (i, 0)),
                   pl.BlockSpec((LANES, tm), lambda i: (0, i))),
        compiler_params=_cparams(("parallel",)),
        name="inproj",
    )(h, w, wgt)


def _conv_kernel(prev_ref, x_ref, next_ref, w_ref, b_ref, o_ref, *, tm, tiles_per_seq):
    i = pl.program_id(0)
    pos = i % tiles_per_seq
    x = x_ref[...]
    prev = jnp.where(pos == 0, 0.0, prev_ref[...])
    nxt = jnp.where(pos == tiles_per_seq - 1, 0.0, next_ref[...])
    xe = jnp.concatenate([prev, x, nxt], axis=0)
    n = tm + 2 * SUBLANES
    acc = b_ref[...] + w_ref[2:3, :] * x
    for k in (0, 1, 3, 4):
        shifted = pltpu.roll(xe, (2 - k) % n, 0)
        acc = acc + w_ref[k:k + 1, :] * shifted[SUBLANES:SUBLANES + tm]
    o_ref[...] = _silu(acc)


def _conv_silu(proj, w, b, colblk, width, seq_len):
    T = proj.shape[0]
    tm = TM_CONV
    rb = tm // SUBLANES
    nrb = T // SUBLANES
    kern = functools.partial(_conv_kernel, tm=tm, tiles_per_seq=seq_len // tm)
    return pl.pallas_call(
        kern,
        out_shape=jax.ShapeDtypeStruct((T, width), F32),
        grid=(T // tm,),
        in_specs=[pl.BlockSpec((SUBLANES, width), lambda i: (jnp.maximum(i * rb - 1, 0), colblk)),
                  pl.BlockSpec((tm, width), lambda i: (i, colblk)),
                  pl.BlockSpec((SUBLANES, width), lambda i: (jnp.minimum((i + 1) * rb, nrb - 1), colblk)),
                  pl.BlockSpec((CONV_WIDTH, width), lambda i: (0, 0)),
                  pl.BlockSpec((1, width), lambda i: (0, 0))],
        out_specs=pl.BlockSpec((tm, width), lambda i: (i, 0)),
        compiler_params=_cparams(("parallel",)),
        name="conv_silu",
    )(proj, proj, proj, w, b.reshape(1, width))


def _cumsum_lanes(x, reverse):
    n = x.shape[1]
    r = _iota((3 * n, n), 0) % n
    c = _iota((3 * n, n), 1)
    tri3 = ((r >= c) if reverse else (r <= c)).astype(BF16)
    return _dot(_split3(x, axis=1), tri3)


def _cols_of(rows_list):
    used = sum(r.shape[0] for r in rows_list)
    pad = jnp.zeros((LANES - used, LANES), F32)
    return jnp.concatenate(list(rows_list) + [pad], axis=0).T


def _ssd_gates(gates_t, bias_cr, alog_cr, reverse):
    last = 0 if reverse else SSD_CHUNK - 1
    dt_r = _softplus(gates_t + bias_cr)
    acs_r = _cumsum_lanes(dt_r * (-jnp.exp(alog_cr)) * LOG2E, reverse)
    a_last = acs_r[:, last:last + 1]
    w_r = dt_r * jnp.exp2(a_last - acs_r)
    return acs_r - jnp.log2(dt_r), acs_r, w_r, jnp.exp2(a_last)


def _ssd_direction(x, gate_rows, acs_cols, st_ref, d, y_ref, rows):
    reverse = d == 1
    n = SSD_CHUNK
    tri = _causal(n, reverse)
    lane_lo = _iota((n, LANES), 1) < SSD_HEAD_DIM
    acs_dt_r, _, w_r, dec = gate_rows

    for g in range(SSD_GROUPS):
        b0 = SSD_INNER + g * SSD_STATE
        c0 = SSD_INNER + SSD_GROUPS * SSD_STATE + g * SSD_STATE
        bg = x[:, b0:b0 + SSD_STATE]
        cgb = x[:, c0:c0 + SSD_STATE].astype(BF16)
        cb = _dot_nt(cgb, bg.astype(BF16))
        bg_t = bg.T
        for pair in range(4):
            pi = g * 4 + pair
            p0 = pi * LANES
            xp = x[:, p0:p0 + LANES]
            h0 = g * 8 + pair * 2
            ms, bhs, xms, cols = [], [], [], []
            for hh in range(2):
                h = h0 + hh
                xms.append(jnp.where(lane_lo if hh == 0 else jnp.logical_not(lane_lo), xp, 0.0).astype(BF16))
                col = jnp.broadcast_to(acs_cols[:, d * SSD_HEADS + h:d * SSD_HEADS + h + 1], (n, n))
                decay_dt = jnp.exp2(jnp.where(tri, col - acs_dt_r[h:h + 1, :], MASK_NEG))
                ms.append((cb * decay_dt).astype(BF16))
                bhs.append((bg_t * w_r[h:h + 1, :]).astype(BF16))
                cols.append(col)
            xcat = jnp.concatenate(xms, axis=0)
            st = st_ref[d, pi]
            from_state = jnp.exp2(jnp.where(lane_lo, cols[0], cols[1])) * _dot(cgb, st.astype(BF16))
            y_ref[rows, p0:p0 + LANES] = _dot(jnp.concatenate(ms, axis=1), xcat) + from_state
            dec_row = jnp.where(lane_lo[0:1, :], dec[h0:h0 + 1, :], dec[h0 + 1:h0 + 2, :])
            st_ref[d, pi] = st * dec_row + _dot(jnp.concatenate(bhs, axis=1), xcat)


def _ssd_kernel(xf_ref, xb_ref, gtf_ref, gtb_ref, bcr_ref, acr_ref, yf_ref, yb_ref, st_ref):
    @pl.when(pl.program_id(1) == 0)
    def _():
        st_ref[...] = jnp.zeros_like(st_ref)

    nsub = SSD_STEP // SSD_CHUNK
    for k in range(nsub):
        cf, cb_ = k, nsub - 1 - k
        rf = slice(cf * SSD_CHUNK, (cf + 1) * SSD_CHUNK)
        rb = slice(cb_ * SSD_CHUNK, (cb_ + 1) * SSD_CHUNK)
        gf = _ssd_gates(gtf_ref[:, rf], bcr_ref[0:SSD_HEADS, :], acr_ref[0:SSD_HEADS, :], False)
        gb = _ssd_gates(gtb_ref[:, rb], bcr_ref[SSD_HEADS:2 * SSD_HEADS, :], acr_ref[SSD_HEADS:2 * SSD_HEADS, :], True)
        acs_cols = _cols_of([gf[1], gb[1]])
        _ssd_direction(xf_ref[rf, :], gf, acs_cols, st_ref, 0, yf_ref, rf)
        _ssd_direction(xb_ref[rb, :], gb, acs_cols, st_ref, 1, yb_ref, rb)


def _ssd_scan(xbc_act, gates_t, bias_cr, alog_cr, batch):
    T = xbc_act.shape[0]
    n = T // batch // SSD_STEP
    fwd = lambda b, i: b * n + i
    bwd = lambda b, i: b * n + (n - 1 - i)
    const = lambda b, i: (0, 0)
    return pl.pallas_call(
        _ssd_kernel,
        out_shape=(jax.ShapeDtypeStruct((T, SSD_INNER), F32),
                   jax.ShapeDtypeStruct((T, SSD_INNER), F32)),
        grid=(batch, n),
        in_specs=[pl.BlockSpec((SSD_STEP, SSD_XBC), lambda b, i: (fwd(b, i), 0)),
                  pl.BlockSpec((SSD_STEP, SSD_XBC), lambda b, i: (bwd(b, i), 0)),
                  pl.BlockSpec((SSD_HEADS, SSD_STEP), lambda b, i: (0, fwd(b, i))),
                  pl.BlockSpec((SSD_HEADS, SSD_STEP), lambda b, i: (1, bwd(b, i))),
                  pl.BlockSpec((2 * SSD_HEADS, LANES), const),
                  pl.BlockSpec((2 * SSD_HEADS, LANES), const)],
        out_specs=(pl.BlockSpec((SSD_STEP, SSD_INNER), lambda b, i: (fwd(b, i), 0)),
                   pl.BlockSpec((SSD_STEP, SSD_INNER), lambda b, i: (bwd(b, i), 0))),
        scratch_shapes=[pltpu.VMEM((2, SSD_HEADS // 2, SSD_STATE, LANES), F32)],
        compiler_params=_cparams(("arbitrary", "arbitrary")),
        name="ssd_scan",
    )(xbc_act, xbc_act, gates_t, gates_t, bias_cr, alog_cr)


def _split3(x, axis=0):
    hi = x.astype(BF16)
    r1 = x - hi.astype(F32)
    mid = r1.astype(BF16)
    lo = (r1 - mid.astype(F32)).astype(BF16)
    return jnp.concatenate([hi, mid, lo], axis=axis)


def _hgrn2_cumsum_lhs(reverse):
    n = HG_SUB
    r_i = _iota((2 * n, n), 0)
    c_i = _iota((2 * n, n), 1)
    t = jnp.where(r_i < n, r_i, r_i - n)
    ref = (t // HG_REFBLK) * HG_REFBLK + (HG_REFBLK - 1 if reverse else 0)
    upto = jnp.where(r_i < n, t, ref)
    m = ((c_i >= upto) if reverse else (c_i <= upto)).astype(BF16)
    return jnp.concatenate([m, m, m], axis=1)


def _hgrn2_intra(q, f, v, lb, qmask, reverse):
    n = HG_SUB
    nblk = n // HG_REFBLK
    slab = 2 * SUBLANES
    tri = _causal(n, reverse)
    last = 0 if reverse else n - 1
    rowblk = _iota((slab, HG_DK), 0) // HG_REFBLK

    qh = _silu(q)
    key = (1.0 - lb) / (1.0 + jnp.exp(f))
    g = jnp.log2(1.0 - key)
    gg = _dot(_hgrn2_cumsum_lhs(reverse), _split3(g))
    gc = gg[:n]
    gref = gg[n:]
    qg = (qh * jnp.exp2(gc - gref)).astype(BF16)
    qs = (qh * jnp.exp2(gc)).astype(BF16)
    g_last = gc[last:last + 1, :]
    hk = gc - jnp.log2(key)
    kd = jnp.exp2(g_last - hk).astype(BF16)
    dec = jnp.exp2(g_last)
    vb = v.astype(BF16)

    o_intra = []
    for h in range(HG_HEADS):
        cols = slice(h * HG_DK, (h + 1) * HG_DK)
        hk_h = hk[:, cols]
        gc_h = gc[:, cols]
        k_parts = [None] * nblk
        far = []
        prev_edge = prev_r = prev_near = None
        for j in (range(nblk - 1, -1, -1) if reverse else range(nblk)):
            ref_row = j * HG_REFBLK + (HG_REFBLK - 1 if reverse else 0)
            rj = gc_h[ref_row:ref_row + 1, :]
            edge = (j * HG_REFBLK) // slab
            jb = j - edge * (slab // HG_REFBLK)
            if prev_r is not None:
                if edge != prev_edge:
                    far.append(prev_near)
                step = jnp.exp2(rj - prev_r)
                far = [p * step for p in far]
            live = (rowblk >= jb) if reverse else (rowblk <= jb)
            near = jnp.exp2(jnp.where(live, rj - hk_h[edge * slab:(edge + 1) * slab], MASK_NEG))
            pieces = [near] + far[::-1] if reverse else far + [near]
            k_parts[j] = jnp.concatenate([p.astype(BF16) for p in pieces], axis=0)
            prev_edge, prev_r, prev_near = edge, rj, near
        per_slab = slab // HG_REFBLK
        att_slabs = []
        for m in range(n // slab):
            q_m = jnp.tile(qg[m * slab:(m + 1) * slab, cols], (1, per_slab)) * qmask
            k_m = jnp.concatenate(k_parts[m * per_slab:(m + 1) * per_slab], axis=1)
            att_m = _dot_nt(q_m, k_m)
            dead = jnp.zeros((slab, n - att_m.shape[1]), F32)
            if dead.shape[1]:
                att_m = jnp.concatenate([dead, att_m] if reverse else [att_m, dead], axis=1)
            att_slabs.append(att_m)
        att = jnp.where(tri, jnp.concatenate(att_slabs, axis=0), 0.0).astype(BF16)
        o_intra.append(_dot(att, vb[:, cols]))
    return o_intra, qs, kd, dec, vb


def _hgrn2_state(intra, st_ref, d, o_ref, rows):
    o_intra, qs, kd, dec, vb = intra
    for h in range(HG_HEADS):
        cols = slice(h * HG_DK, (h + 1) * HG_DK)
        st = st_ref[d, h]
        o_ref[rows, cols] = o_intra[h] + _dot_nt(qs[:, cols], st.astype(BF16))
        st_ref[d, h] = st * dec[:, cols] + _dot_tn(vb[:, cols], kd[:, cols])


def _hgrn2_kernel(qf_ref, ff_ref, vf_ref, qb_ref, fb_ref, vb_ref, lb_ref, qmask_ref, of_ref, ob_ref, st_ref):
    @pl.when(pl.program_id(1) == 0)
    def _():
        st_ref[...] = jnp.zeros_like(st_ref)

    lb = lb_ref[...]
    qmask = qmask_ref[...]
    refs = ((qf_ref, ff_ref, vf_ref, of_ref), (qb_ref, fb_ref, vb_ref, ob_ref))
    nsub = HG_STEP // HG_SUB
    units = []
    for k in range(nsub):
        units.append((0, k))
        units.append((1, nsub - 1 - k))
    pending = None
    for d, sc in units:
        rows = slice(sc * HG_SUB, (sc + 1) * HG_SUB)
        q_ref, f_ref, v_ref, o_ref = refs[d]
        intra = _hgrn2_intra(q_ref[rows, :], f_ref[rows, :], v_ref[rows, :], lb, qmask, d == 1)
        if pending is not None:
            _hgrn2_state(*pending)
        pending = (intra, st_ref, d, o_ref, rows)
    _hgrn2_state(*pending)


def _hgrn2_scan(proj, lb, batch):
    T = proj.shape[0]
    n = T // batch // HG_STEP
    fwd = lambda b, i: b * n + i
    bwd = lambda b, i: b * n + (n - 1 - i)
    blk = (HG_STEP, HG_WIDTH)
    slab = 2 * SUBLANES
    nblk = slab // HG_REFBLK
    qmask = (jnp.arange(slab)[:, None] // HG_REFBLK == jnp.arange(nblk * HG_DK)[None, :] // HG_DK).astype(BF16)
    return pl.pallas_call(
        _hgrn2_kernel,
        out_shape=(jax.ShapeDtypeStruct((T, HG_WIDTH), F32),
                   jax.ShapeDtypeStruct((T, HG_WIDTH), F32)),
        grid=(batch, n),
        in_specs=[pl.BlockSpec(blk, lambda b, i: (fwd(b, i), COLBLK_HQ)),
                  pl.BlockSpec(blk, lambda b, i: (fwd(b, i), COLBLK_HF_FWD)),
                  pl.BlockSpec(blk, lambda b, i: (fwd(b, i), COLBLK_HI)),
                  pl.BlockSpec(blk, lambda b, i: (bwd(b, i), COLBLK_HQ)),
                  pl.BlockSpec(blk, lambda b, i: (bwd(b, i), COLBLK_HF_BWD)),
                  pl.BlockSpec(blk, lambda b, i: (bwd(b, i), COLBLK_HI)),
                  pl.BlockSpec((1, HG_WIDTH), lambda b, i: (0, 0)),
                  pl.BlockSpec((slab, nblk * HG_DK), lambda b, i: (0, 0))],
        out_specs=(pl.BlockSpec(blk, lambda b, i: (fwd(b, i), 0)),
                   pl.BlockSpec(blk, lambda b, i: (bwd(b, i), 0))),
        scratch_shapes=[pltpu.VMEM((2, HG_HEADS, HG_DK, HG_DK), F32)],
        compiler_params=_cparams(("arbitrary", "arbitrary")),
        name="hgrn2_scan",
    )(proj, proj, proj, proj, proj, proj, lb.reshape(1, HG_WIDTH), qmask)


def _mlstm_kernel(batch, qkf_ref, vf_ref, qkb_ref, vb_ref, *rest):
    gt_refs = (rest[:batch], rest[batch:2 * batch])
    bcr_ref, of_ref, ob_ref, c_ref, m_ref, mlane_ref = rest[2 * batch:]

    @pl.when(pl.program_id(0) == 0)
    def _():
        c_ref[...] = jnp.zeros_like(c_ref)
        m_ref[...] = jnp.zeros_like(m_ref)
        mlane_ref[...] = jnp.zeros_like(mlane_ref)

    n = SCAN_STEP
    ones = jnp.ones((n, ML_DH), BF16)
    io_refs = ((qkf_ref, vf_ref, of_ref), (qkb_ref, vb_ref, ob_ref))
    tris = (_causal(n, False), _causal(n, True))
    row_i = _iota((n, LANES), 0)

    gates = {}
    for d in range(2):
        reverse = d == 1
        last = 0 if reverse else n - 1
        rows = []
        for b in range(batch):
            g_r = gt_refs[d][b][...] + bcr_ref[...]
            bc = _cumsum_lanes(_log_sigmoid(g_r) * LOG2E, reverse)[SUBLANES:]
            ig = g_r[:SUBLANES] * LOG2E
            rows.append((bc, ig, ig - bc))
        tb = _cols_of([r[0] for r in rows])
        pm = _cols_of([r[2] for r in rows])
        for sh in (1, 2, 4, 8, 16, 32, 64):
            shifted = pltpu.roll(pm, (n - sh) if reverse else sh, 0)
            valid = (row_i < n - sh) if reverse else (row_i >= sh)
            pm = jnp.maximum(pm, jnp.where(valid, shifted, -jnp.inf))
        m_prev_lane = mlane_ref[d][0:1, :]
        floor = jnp.where((row_i > 0) if reverse else (row_i < n - 1), MASK_NEG * LOG2E, -jnp.inf)
        mt = jnp.maximum(jnp.maximum(tb + pm, tb + m_prev_lane), floor)
        b_last_lane = tb[last:last + 1, :]
        m_new_lane = jnp.maximum(b_last_lane + m_prev_lane, b_last_lane + pm[last:last + 1, :])
        mlane_ref[d] = jnp.broadcast_to(m_new_lane, (SUBLANES, LANES))
        bm = tb - mt
        for b in range(batch):
            bc, ig, u = rows[b]
            m_prev = m_ref[b * 2 + d]
            b_last = bc[:, last:last + 1]
            m_new = jnp.maximum(b_last + m_prev[:, 0:1], b_last + jnp.max(u, axis=1, keepdims=True))
            w_s = jnp.exp2(b_last - bc + ig - m_new)
            dec = jnp.exp2(b_last + m_prev[:, 0:1] - m_new)
            m_ref[b * 2 + d] = jnp.broadcast_to(m_new, (SUBLANES, LANES))
            gates[b, d] = (u, bm, mt, m_prev, w_s, dec)

    units = [(b, d, h) for h in range(ML_HEADS) for b in range(batch) for d in range(2)]
    for b, d, h in units:
        u, bm, mt, m_prev, w_s, dec = gates[b, d]
        qk_ref, v_ref, o_ref = io_refs[d]
        r = d * ML_HEADS + h
        c = b * SUBLANES + r
        qf = qk_ref[b, :, h * ML_DH:(h + 1) * ML_DH] * (ML_DH ** -0.5)
        k = qk_ref[b, :, ML_WIDTH + h * ML_DH:ML_WIDTH + (h + 1) * ML_DH]
        ve = jnp.concatenate([v_ref[b, :, h * ML_DH:(h + 1) * ML_DH].astype(BF16), ones], axis=1)
        b_minus_m = jnp.broadcast_to(bm[:, c:c + 1], (n, n))
        m_t = jnp.broadcast_to(mt[:, c:c + 1], (n, n))
        w_intra = jnp.exp2(jnp.where(tris[d], b_minus_m + u[r:r + 1, :], MASK_NEG))
        w_inter = jnp.exp2(b_minus_m + m_prev[r:r + 1, 0:1])
        s = (_dot_nt(qf.astype(BF16), k.astype(BF16)) * w_intra).astype(BF16)
        cst = c_ref[b, d, h]
        numden = _dot(jnp.concatenate([s, (qf * w_inter).astype(BF16)], axis=1),
                      jnp.concatenate([ve, cst.astype(BF16)], axis=0))
        den = numden[:, ML_DH:]
        o_ref[b, :, h * ML_DH:(h + 1) * ML_DH] = numden[:, :ML_DH] / jnp.maximum(jnp.abs(den), jnp.exp2(-m_t))
        kw_t = (k.T * w_s[r:r + 1, :]).astype(BF16)
        c_ref[b, d, h] = dec[r:r + 1, :] * cst + _dot(kw_t, ve)


def _mlstm_scan(mqk_act, proj, gates_t, bias_cr, batch):
    T = proj.shape[0]
    seq = T // batch
    n = seq // SCAN_STEP
    gt_rowblk = GATE_IG // (2 * SUBLANES)
    fwd3 = lambda cb: (lambda i: (0, i, cb))
    bwd3 = lambda cb: (lambda i: (0, n - 1 - i, cb))
    gt_fwd = [pl.BlockSpec((2 * SUBLANES, SCAN_STEP), functools.partial(lambda b, i: (gt_rowblk, b * n + i), b))
              for b in range(batch)]
    gt_bwd = [pl.BlockSpec((2 * SUBLANES, SCAN_STEP),
                           functools.partial(lambda b, i: (gt_rowblk, b * n + n - 1 - i), b))
              for b in range(batch)]
    qk3 = mqk_act.reshape(batch, seq, 2 * ML_WIDTH)
    proj3 = proj.reshape(batch, seq, D_PROJ_PAD)
    o_f, o_b = pl.pallas_call(
        functools.partial(_mlstm_kernel, batch),
        out_shape=(jax.ShapeDtypeStruct((batch, seq, ML_WIDTH), F32),
                   jax.ShapeDtypeStruct((batch, seq, ML_WIDTH), F32)),
        grid=(n,),
        in_specs=[pl.BlockSpec((batch, SCAN_STEP, 2 * ML_WIDTH), fwd3(0)),
                  pl.BlockSpec((batch, SCAN_STEP, ML_WIDTH), fwd3(COLBLK_MV)),
                  pl.BlockSpec((batch, SCAN_STEP, 2 * ML_WIDTH), bwd3(0)),
                  pl.BlockSpec((batch, SCAN_STEP, ML_WIDTH), bwd3(COLBLK_MV))]
                 + gt_fwd + gt_bwd
                 + [pl.BlockSpec((2 * SUBLANES, LANES), lambda i: (0, 0))],
        out_specs=(pl.BlockSpec((batch, SCAN_STEP, ML_WIDTH), fwd3(0)),
                   pl.BlockSpec((batch, SCAN_STEP, ML_WIDTH), bwd3(0))),
        scratch_shapes=[pltpu.VMEM((batch, 2, ML_HEADS, ML_DH, 2 * ML_DH), F32),
                        pltpu.VMEM((batch * 2, SUBLANES, LANES), F32),
                        pltpu.VMEM((2, SUBLANES, LANES), F32)],
        compiler_params=_cparams(("arbitrary",)),
        name="mlstm_scan",
    )(qk3, proj3, qk3, proj3, *([gates_t] * (2 * batch)), bias_cr)
    return o_f.reshape(T, ML_WIDTH), o_b.reshape(T, ML_WIDTH)


def _outproj_kernel(yf_ref, yb_ref, xs_ref, z_ref, of_ref, ob_ref, hg_ref, mf_ref, mb_ref, mo_ref,
                    h_ref, dskip_ref, ssdw_ref, hgw_ref, mlw_ref, wout_ref, g_ref, b_ref, o_ref):
    y = (yf_ref[...] + yb_ref[...] + xs_ref[...] * dskip_ref[...]) * _silu(z_ref[...])
    gw = SSD_INNER // SSD_GROUPS
    acc = None
    for g in range(SSD_GROUPS):
        seg = y[:, g * gw:(g + 1) * gw]
        seg = seg * lax.rsqrt(jnp.mean(seg * seg, -1, keepdims=True) + RMS_EPS) * ssdw_ref[:, g * gw:(g + 1) * gw]
        part = _dot(seg.astype(BF16), wout_ref[g * gw:(g + 1) * gw, :])
        acc = part if acc is None else acc + part
    o = of_ref[...] + ob_ref[...]
    hgate = _silu(hg_ref[...])
    m = mf_ref[...] + mb_ref[...]
    mgate = _sigmoid(mo_ref[...])
    for h in range(HG_HEADS):
        cols = slice(h * HG_DK, (h + 1) * HG_DK)
        seg = o[:, cols]
        seg = seg * lax.rsqrt(jnp.mean(seg * seg, -1, keepdims=True) + RMS_EPS) * hgw_ref[:, cols] * hgate[:, cols]
        r0 = SSD_INNER + h * HG_DK
        acc = acc + _dot(seg.astype(BF16), wout_ref[r0:r0 + HG_DK, :])
    for h in range(ML_HEADS):
        cols = slice(h * ML_DH, (h + 1) * ML_DH)
        seg = m[:, cols]
        seg = seg - jnp.mean(seg, -1, keepdims=True)
        seg = seg * lax.rsqrt(jnp.mean(seg * seg, -1, keepdims=True) + LN_EPS) * mlw_ref[:, cols] * mgate[:, cols]
        r0 = SSD_INNER + HG_WIDTH + h * ML_DH
        acc = acc + _dot(seg.astype(BF16), wout_ref[r0:r0 + ML_DH, :])
    o_ref[...] = _layer_norm(ALPHA * h_ref[...] + acc, g_ref[...], b_ref[...])


def _outproj(y_f, y_b, xbc_act, proj, o_f, o_b, m_f, m_b, h, dskip, ssd_w, hg_w, ml_w, w_out, g, b):
    T, D = h.shape
    tm = TM_OUT
    row = lambda width, cb: pl.BlockSpec((tm, width), lambda i: (i, cb))
    par = lambda width: pl.BlockSpec((1, width), lambda i: (0, 0))
    return pl.pallas_call(
        _outproj_kernel,
        out_shape=jax.ShapeDtypeStruct((T, D), F32),
        grid=(T // tm,),
        in_specs=[row(SSD_INNER, 0), row(SSD_INNER, 0), row(SSD_INNER, 0), row(SSD_INNER, COLBLK_Z),
                  row(HG_WIDTH, 0), row(HG_WIDTH, 0), row(HG_WIDTH, COLBLK_HG),
                  row(ML_WIDTH, 0), row(ML_WIDTH, 0), row(ML_WIDTH, COLBLK_MO),
                  row(D, 0), par(SSD_INNER), par(SSD_INNER), par(HG_WIDTH), par(ML_WIDTH),
                  pl.BlockSpec((D_MIX, D), lambda i: (0, 0), pipeline_mode=pl.Buffered(1)),
                  par(D), par(D)],
        out_specs=pl.BlockSpec((tm, D), lambda i: (i, 0)),
        compiler_params=_cparams(("parallel",)),
        name="outproj_ln1",
    )(y_f, y_b, xbc_act, proj, o_f, o_b, proj, m_f, m_b, proj, h, dskip, ssd_w, hg_w, ml_w, w_out,
      g.reshape(1, D), b.reshape(1, D))


def _ffn_kernel(h_ref, w1_ref, w3_ref, w2_ref, g_ref, b_ref, o_ref, acc_ref):
    h = h_ref[...]
    hb = h.astype(BF16)
    for c in range(D_FF_DENSE // TF_FFN):
        cols = slice(c * TF_FFN, (c + 1) * TF_FFN)
        a = _dot(hb, w1_ref[:, cols])
        u = _dot(hb, w3_ref[:, cols])
        part = _dot((_silu(a) * u).astype(BF16), w2_ref[cols, :])
        if c == 0:
            acc_ref[...] = part
        else:
            acc_ref[...] += part
    o_ref[...] = _layer_norm(ALPHA * h + acc_ref[...], g_ref[...], b_ref[...])


def _dense_ffn(h, w1, w3, w2, g, b):
    T, D = h.shape
    tm = TM_FFN
    resident = lambda shape: pl.BlockSpec(shape, lambda i: (0, 0), pipeline_mode=pl.Buffered(1))
    return pl.pallas_call(
        _ffn_kernel,
        out_shape=jax.ShapeDtypeStruct((T, D), F32),
        grid=(T // tm,),
        in_specs=[pl.BlockSpec((tm, D), lambda i: (i, 0)),
                  resident((D, D_FF_DENSE)), resident((D, D_FF_DENSE)), resident((D_FF_DENSE, D)),
                  pl.BlockSpec((1, D), lambda i: (0, 0)), pl.BlockSpec((1, D), lambda i: (0, 0))],
        out_specs=pl.BlockSpec((tm, D), lambda i: (i, 0)),
        scratch_shapes=[pltpu.VMEM((tm, D), F32)],
        compiler_params=_cparams(("parallel",)),
        name="dense_ffn_ln2",
    )(h, w1, w3, w2, g.reshape(1, D), b.reshape(1, D))


def _router_kernel(h_ref, w_ref, idx_ref, gate_ref):
    logits = _dot_hi(h_ref[...], w_ref[...])
    lane = _iota(logits.shape, 1)
    lane_f = lane.astype(F32)
    x = jnp.where(lane < N_EXPERTS, logits, -jnp.inf)
    m1 = jnp.max(x, axis=1, keepdims=True)
    i1 = jnp.min(jnp.where(x == m1, lane_f, float(LANES)), axis=1, keepdims=True)
    x2 = jnp.where(lane_f == i1, -jnp.inf, x)
    m2 = jnp.max(x2, axis=1, keepdims=True)
    i2 = jnp.min(jnp.where(x2 == m2, lane_f, float(LANES)), axis=1, keepdims=True)
    e2 = jnp.exp(m2 - m1)
    g1 = 1.0 / (1.0 + e2)
    g2 = e2 / (1.0 + e2)
    idx_ref[...] = jnp.where(lane == 0, i1, jnp.where(lane == 1, i2, 0.0)).astype(jnp.int32)
    gate_ref[...] = jnp.where(lane == 0, g1, jnp.where(lane == 1, g2, 0.0))


def _router(h, w_pad):
    T, D = h.shape
    tm = TM_ROUTER
    return pl.pallas_call(
        _router_kernel,
        out_shape=(jax.ShapeDtypeStruct((T, LANES), jnp.int32),
                   jax.ShapeDtypeStruct((T, LANES), F32)),
        grid=(T // tm,),
        in_specs=[pl.BlockSpec((tm, D), lambda i: (i, 0)),
                  pl.BlockSpec((D, LANES), lambda i: (0, 0))],
        out_specs=(pl.BlockSpec((tm, LANES), lambda i: (i, 0)),
                   pl.BlockSpec((tm, LANES), lambda i: (i, 0))),
        compiler_params=_cparams(("parallel",)),
        name="moe_router",
    )(h, w_pad)


def _gather_rows(src, idx):
    n = idx.shape[0]
    D = src.shape[1]
    per_worker = n // SC_WORKERS
    n_chunks = per_worker // SC_GATHER_CHUNK
    assert n_chunks * SC_GATHER_CHUNK * SC_WORKERS == n and n_chunks % 2 == 0
    mesh = plsc.VectorSubcoreMesh(core_axis_name="c", subcore_axis_name="s")

    @functools.partial(
        pl.kernel, mesh=mesh,
        out_type=jax.ShapeDtypeStruct((n, D), src.dtype),
        scratch_types=[pltpu.VMEM((2, SC_GATHER_CHUNK), jnp.int32),
                       pltpu.VMEM((2, SC_GATHER_CHUNK, D), src.dtype),
                       pltpu.SemaphoreType.DMA((2,))],
        name="moe_row_gather",
    )
    def gather(src_hbm, idx_hbm, out_hbm, idx_v, rows_v, sems):
        worker = lax.axis_index("s") * SC_CORES + lax.axis_index("c")
        base = worker * per_worker

        def gather_copy(slot):
            return pltpu.make_async_copy(src_hbm.at[idx_v.at[slot]], rows_v.at[slot], sems.at[slot])

        def start(j, slot):
            pltpu.sync_copy(idx_hbm.at[pl.ds(base + j * SC_GATHER_CHUNK, SC_GATHER_CHUNK)], idx_v.at[slot])
            gather_copy(slot).start()

        def finish(j, slot):
            gather_copy(slot).wait()
            pltpu.sync_copy(rows_v.at[slot], out_hbm.at[pl.ds(base + j * SC_GATHER_CHUNK, SC_GATHER_CHUNK)])

        start(0, 0)

        @pl.loop(0, n_chunks, step=2)
        def _(j):
            start(j + 1, 1)
            finish(j, 0)

            @pl.when(j + 2 < n_chunks)
            def _():
                start(j + 2, 0)

            finish(j + 1, 1)

    return gather(src, idx)


def _scatter_rows(init, vals, idx):
    n_out, W = init.shape
    n = idx.shape[0]
    out_per = n_out // SC_SUBCORES
    per = n // SC_SUBCORES
    n_chunks = per // SC_SCATTER_CHUNK
    init_chunks = out_per // SC_INIT_CHUNK
    assert init_chunks * SC_INIT_CHUNK * SC_SUBCORES == n_out and n_chunks * SC_SCATTER_CHUNK * SC_SUBCORES == n
    mesh = plsc.VectorSubcoreMesh(core_axis_name="c", subcore_axis_name="s", num_cores=1)

    @functools.partial(
        pl.kernel, mesh=mesh,
        out_type=jax.ShapeDtypeStruct((n_out, W), init.dtype),
        scratch_types=[pltpu.VMEM((SC_INIT_CHUNK, W), init.dtype),
                       pltpu.VMEM((SC_SCATTER_CHUNK,), jnp.int32),
                       pltpu.VMEM((SC_SCATTER_CHUNK, W), init.dtype)],
        name="moe_row_scatter",
    )
    def scatter(init_hbm, vals_hbm, idx_hbm, out_hbm, init_v, idx_v, vals_v):
        worker = lax.axis_index("s")

        @pl.loop(0, init_chunks)
        def _(j):
            off = worker * out_per + j * SC_INIT_CHUNK
            pltpu.sync_copy(init_hbm.at[pl.ds(off, SC_INIT_CHUNK)], init_v)
            pltpu.sync_copy(init_v, out_hbm.at[pl.ds(off, SC_INIT_CHUNK)])

        plsc.subcore_barrier()

        @pl.loop(0, n_chunks)
        def _(j):
            off = worker * per + j * SC_SCATTER_CHUNK
            pltpu.sync_copy(idx_hbm.at[pl.ds(off, SC_SCATTER_CHUNK)], idx_v)
            pltpu.sync_copy(vals_hbm.at[pl.ds(off, SC_SCATTER_CHUNK)], vals_v)
            pltpu.sync_copy(vals_v, out_hbm.at[idx_v])

    return scatter(init, vals, idx)


def _expert_kernel(be_ref, nact_ref, x_ref, w1_ref, w3_ref, w2_ref, o_ref):
    i = pl.program_id(0)
    f = pl.program_id(1)

    @pl.when(f == 0)
    def _():
        o_ref[...] = jnp.zeros_like(o_ref)

    @pl.when(i < nact_ref[0])
    def _():
        xb = x_ref[...].astype(BF16)
        a = _dot(xb, w1_ref[...].astype(BF16))
        u = _dot(xb, w3_ref[...].astype(BF16))
        o_ref[...] += _dot((_silu(a) * u).astype(BF16), w2_ref[...].astype(BF16))


def _expert_ffn(xs, block_expert, n_active, w1, w3, w2, layer):
    n_rows, D = xs.shape
    n_blocks = n_rows // MOE_BM
    n_ff = D_FF_EXPERT // MOE_TF

    def widx(i, f, be, nact):
        live = i < nact[0]
        return be[jnp.minimum(i, nact[0] - 1)], jnp.where(live, f, n_ff - 1)

    def w13_map(i, f, be, nact):
        e, ff = widx(i, f, be, nact)
        return layer, e, 0, ff

    def w2_map(i, f, be, nact):
        e, ff = widx(i, f, be, nact)
        return layer, e, ff, 0

    grid_spec = pltpu.PrefetchScalarGridSpec(
        num_scalar_prefetch=2,
        grid=(n_blocks, n_ff),
        in_specs=[pl.BlockSpec((MOE_BM, D), lambda i, f, be, nact: (i, 0)),
                  pl.BlockSpec((None, None, D, MOE_TF), w13_map),
                  pl.BlockSpec((None, None, D, MOE_TF), w13_map),
                  pl.BlockSpec((None, None, MOE_TF, D), w2_map)],
        out_specs=pl.BlockSpec((MOE_BM, D), lambda i, f, be, nact: (i, 0)),
    )
    return pl.pallas_call(
        _expert_kernel,
        out_shape=jax.ShapeDtypeStruct((n_rows, D), F32),
        grid_spec=grid_spec,
        compiler_params=_cparams(("arbitrary", "arbitrary")),
        name="moe_experts",
    )(block_expert, n_active, xs, w1, w3, w2)


def _combine_kernel(y0_ref, y1_ref, gate_ref, h_ref, g_ref, b_ref, o_ref):
    gate = gate_ref[...]
    f = y0_ref[...] * gate[:, 0:1] + y1_ref[...] * gate[:, 1:2]
    o_ref[...] = _layer_norm(ALPHA * h_ref[...] + f, g_ref[...], b_ref[...])


def _combine(y_slots, gates, h, g, b):
    T, D = h.shape
    tm = TM_COMBINE
    return pl.pallas_call(
        _combine_kernel,
        out_shape=jax.ShapeDtypeStruct((T, D), F32),
        grid=(T // tm,),
        in_specs=[pl.BlockSpec((tm, D), lambda i: (i, 0)),
                  pl.BlockSpec((tm, D), lambda i: (i + T // tm, 0)),
                  pl.BlockSpec((tm, LANES), lambda i: (i, 0)),
                  pl.BlockSpec((tm, D), lambda i: (i, 0)),
                  pl.BlockSpec((1, D), lambda i: (0, 0)),
                  pl.BlockSpec((1, D), lambda i: (0, 0))],
        out_specs=pl.BlockSpec((tm, D), lambda i: (i, 0)),
        compiler_params=_cparams(("parallel",)),
        name="moe_combine_ln2",
    )(y_slots, y_slots, gates, h, g.reshape(1, D), b.reshape(1, D))


def _moe_ffn(h, router_w, w1, w3, w2, layer, g, b):
    T, D = h.shape
    n_pairs = T * TOP_K
    n_blocks = n_pairs // MOE_BM + N_EXPERTS
    n_rows = n_blocks * MOE_BM
    w_pad = jnp.pad(router_w.astype(F32), ((0, 0), (0, LANES - N_EXPERTS)))
    idx_pad, gates_pad = _router(h, w_pad)
    expert_ids = idx_pad[:, :TOP_K].reshape(-1)
    onehot = (expert_ids[:, None] == jnp.arange(N_EXPERTS, dtype=jnp.int32)[None, :]).astype(jnp.int32)
    running = jnp.cumsum(onehot, axis=0)
    counts = running[-1]
    rank = jnp.sum(onehot * running, axis=1) - 1
    padded = (counts + MOE_BM - 1) // MOE_BM * MOE_BM
    ends = jnp.cumsum(padded)
    pos = (ends - padded)[expert_ids] + rank
    token_ids = jnp.arange(n_pairs, dtype=jnp.int32) // TOP_K
    default = jnp.broadcast_to((jnp.arange(n_rows, dtype=jnp.int32) % T)[:, None], (n_rows, SC_ROW_WORDS))
    row_token = _scatter_rows(default, jnp.broadcast_to(token_ids[:, None], (n_pairs, SC_ROW_WORDS)),
                              pos.astype(jnp.int32))[:, 0]
    block_start = jnp.arange(n_blocks, dtype=jnp.int32) * MOE_BM
    block_expert = jnp.minimum(jnp.searchsorted(ends, block_start, side='right'),
                               N_EXPERTS - 1).astype(jnp.int32)
    n_active = (ends[-1:] // MOE_BM).astype(jnp.int32)

    xs = _gather_rows(h, row_token)
    ys = _expert_ffn(xs, block_expert, n_active, w1, w3, w2, layer)
    slot_major = pos.astype(jnp.int32).reshape(T, TOP_K).T.reshape(-1)
    y_slots = _gather_rows(ys, slot_major)
    return _combine(y_slots, gates_pad, h, g, b)


def _relayout_w_in(w):
    sizes = (SSD_INNER, SSD_XBC, 2 * SSD_HEADS, HG_WIDTH, 2 * HG_WIDTH, HG_WIDTH, HG_WIDTH,
             2 * ML_WIDTH, ML_WIDTH, ML_WIDTH, 2 * ML_HEADS, 2 * ML_HEADS)
    offs = [0]
    for s in sizes:
        offs.append(offs[-1] + s)
    (z, xbc, dt, hq, hf, hi, hg, mqk, mv, mo, mig, mfg) = [w[:, offs[k]:offs[k + 1]] for k in range(len(sizes))]
    pad = jnp.zeros((w.shape[0], LANES - 2 * SSD_HEADS - 4 * ML_HEADS), w.dtype)
    wn = jnp.concatenate([z, mqk, hf, xbc, hq, hi, hg, mv, mo], axis=1).astype(BF16)
    wgt = jnp.concatenate([dt, mig, mfg, pad], axis=1).astype(BF16).T
    return wn, wgt


def _gate_rows(vals_by_offset):
    row = jnp.zeros((LANES,), F32)
    for off, v in vals_by_offset:
        row = lax.dynamic_update_slice(row, v.reshape(-1).astype(F32), (off,))
    return jnp.broadcast_to(row[:, None], (LANES, LANES))


def kernel(x, ln_in_g, ln_in_b, w_in, ssd_conv_w, ssd_conv_b, ssd_dt_bias, ssd_a_log, ssd_d, ssd_norm_w, hg_lb_logits, hg_norm_w, ml_conv_w, ml_conv_b, ml_ig_bias, ml_fg_bias, ml_norm_w, w_out, ln1_g, ln1_b, ln2_g, ln2_b, ffn_w1, ffn_w3, ffn_w2, moe_router, moe_w1, moe_w3, moe_w2):
    batch, seq_len, D = x.shape
    T = batch * seq_len
    f32 = lambda t: t.astype(F32)
    lb_soft = jax.nn.softmax(f32(hg_lb_logits), axis=0)
    lb_all = jnp.cumsum(lb_soft, axis=0) - lb_soft[0]

    h = _input_ln(f32(x).reshape(T, D), f32(ln_in_g), f32(ln_in_b))
    for l in range(DEPTH):
        w, wgt = _relayout_w_in(w_in[l])
        bias_cr = _gate_rows(((GATE_DT, ssd_dt_bias[l]), (GATE_IG, ml_ig_bias[l]), (GATE_FG, ml_fg_bias[l])))
        alog_cr = _gate_rows(((GATE_DT, ssd_a_log[l]),))

        proj, gates_t = _inproj(h, w, wgt)
        xbc_act = _conv_silu(proj, f32(ssd_conv_w[l]), f32(ssd_conv_b[l]), COLBLK_XBC, SSD_XBC, seq_len)
        mqk_act = _conv_silu(proj, f32(ml_conv_w[l]), f32(ml_conv_b[l]), COLBLK_MQK, 2 * ML_WIDTH, seq_len)
        y_f, y_b = _ssd_scan(xbc_act, gates_t, bias_cr[GATE_DT:GATE_DT + 2 * SSD_HEADS],
                             alog_cr[GATE_DT:GATE_DT + 2 * SSD_HEADS], batch)
        o_f, o_b = _hgrn2_scan(proj, lb_all[l], batch)
        m_f, m_b = _mlstm_scan(mqk_act, proj, gates_t, bias_cr[GATE_IG:GATE_IG + 4 * ML_HEADS], batch)
        dskip = jnp.repeat(f32(ssd_d[l]), SSD_HEAD_DIM).reshape(1, SSD_INNER)
        h = _outproj(y_f, y_b, xbc_act, proj, o_f, o_b, m_f, m_b, h, dskip,
                     f32(ssd_norm_w[l]).reshape(1, -1), f32(hg_norm_w[l]).reshape(1, -1),
                     f32(ml_norm_w[l]).reshape(1, -1), w_out[l].astype(BF16), f32(ln1_g[l]), f32(ln1_b[l]))
        if l % 2 == 0:
            j = l // 2
            h = _dense_ffn(h, ffn_w1[j].astype(BF16), ffn_w3[j].astype(BF16), ffn_w2[j].astype(BF16),
                           f32(ln2_g[l]), f32(ln2_b[l]))
        else:
            j = l // 2
            h = _moe_ffn(h, moe_router[j], f32(moe_w1), f32(moe_w3), f32(moe_w2), j,
                         f32(ln2_g[l]), f32(ln2_b[l]))
    return h.reshape(batch, seq_len, D).astype(x.dtype)
```

```python
import functools

import jax
import jax.numpy as jnp
from jax import lax
from jax.experimental import pallas as pl
from jax.experimental.pallas import tpu as pltpu
from jax.experimental.pallas import tpu_sc as plsc

F32 = jnp.float32
BF16 = jnp.bfloat16
HIGHEST = lax.Precision.HIGHEST

D_MODEL = 1024
DEPTH = 4
D_MIX = 2 * D_MODEL
SSD_HEADS = 16
SSD_INNER = 1024
SSD_HEAD_DIM = 64
SSD_GROUPS = 2
SSD_STATE = 128
SSD_XBC = 1536
SSD_CHUNK = 128
HG_HEADS = 4
HG_WIDTH = 512
HG_DK = 128
ML_HEADS = 4
ML_WIDTH = 512
ML_DH = 128
CONV_WIDTH = 5
D_FF_DENSE = 2816
N_EXPERTS = 8
TOP_K = 2
D_FF_EXPERT = 3584
LN_EPS = 1e-5
RMS_EPS = 1e-6
MASK_NEG = -1e4
ALPHA = (2.0 * DEPTH) ** 0.25
LOG2E = 1.4426950408889634

LANES = 128
SUBLANES = 8
VMEM_LIMIT = 56 * 1024 * 1024

D_PROJ_PAD = 7168
COLBLK_Z = 0
COLBLK_MQK = 1
COLBLK_XBC = 2
COLBLK_HF_FWD = 4
COLBLK_HF_BWD = 5
COLBLK_HQ = 9
COLBLK_HI = 10
COLBLK_HG = 11
COLBLK_MV = 12
COLBLK_MO = 13
GATE_DT = 0
GATE_IG = 32
GATE_FG = 40

TM_LN = 512
TM_INPROJ = 512
TM_CONV = 512
SSD_STEP = 512
SCAN_STEP = 128
HG_STEP = 512
HG_SUB = 64
HG_REFBLK = 4
TM_OUT = 512
TM_FFN = 512
TF_FFN = 256
TM_ROUTER = 512
MOE_BM = 1024
MOE_TF = 512
SC_CORES = 2
SC_SUBCORES = 16
SC_WORKERS = SC_CORES * SC_SUBCORES
SC_SCATTER_CHUNK = 128
SC_ROW_WORDS = 128
SC_INIT_CHUNK = 512
SC_GATHER_CHUNK = 32
TM_COMBINE = 512


def _cparams(sem):
    return pltpu.CompilerParams(dimension_semantics=sem, vmem_limit_bytes=VMEM_LIMIT)


def _iota(shape, dim):
    return lax.broadcasted_iota(jnp.int32, shape, dim)


def _sigmoid(x):
    return 1.0 / (1.0 + jnp.exp(-x))


def _silu(x):
    return x * _sigmoid(x)


def _softplus(x):
    return jnp.maximum(x, 0.0) + jnp.log1p(jnp.exp(-jnp.abs(x)))


def _log_sigmoid(x):
    return -_softplus(-x)


def _layer_norm(x, g, b):
    mu = jnp.mean(x, -1, keepdims=True)
    xc = x - mu
    var = jnp.mean(xc * xc, -1, keepdims=True)
    return xc * lax.rsqrt(var + LN_EPS) * g + b


def _causal(n, reverse):
    r = _iota((n, n), 0)
    c = _iota((n, n), 1)
    return (c >= r) if reverse else (c <= r)


def _dot(a, b):
    return jnp.dot(a, b, preferred_element_type=F32)


def _dot_nt(a, b):
    return lax.dot_general(a, b, (((1,), (1,)), ((), ())), preferred_element_type=F32)


def _dot_tn(a, b):
    return lax.dot_general(a, b, (((0,), (0,)), ((), ())), preferred_element_type=F32)


def _dot_hi(a, b):
    return jnp.dot(a, b, precision=HIGHEST, preferred_element_type=F32)


def _ln_kernel(x_ref, g_ref, b_ref, o_ref):
    o_ref[...] = _layer_norm(x_ref[...], g_ref[...], b_ref[...])


def _input_ln(x, g, b):
    T, D = x.shape
    return pl.pallas_call(
        _ln_kernel,
        out_shape=jax.ShapeDtypeStruct((T, D), F32),
        grid=(T // TM_LN,),
        in_specs=[pl.BlockSpec((TM_LN, D), lambda i: (i, 0)),
                  pl.BlockSpec((1, D), lambda i: (0, 0)),
                  pl.BlockSpec((1, D), lambda i: (0, 0))],
        out_specs=pl.BlockSpec((TM_LN, D), lambda i: (i, 0)),
        compiler_params=_cparams(("parallel",)),
        name="input_ln",
    )(x, g.reshape(1, D), b.reshape(1, D))


def _inproj_kernel(h_ref, w_ref, wgt_ref, proj_ref, gt_ref):
    hb = h_ref[...].astype(BF16)
    proj_ref[...] = _dot(hb, w_ref[...])
    gt_ref[...] = _dot_nt(wgt_ref[...], hb)


def _inproj(h, w, wgt):
    T, D = h.shape
    tm = TM_INPROJ
    return pl.pallas_call(
        _inproj_kernel,
        out_shape=(jax.ShapeDtypeStruct((T, D_PROJ_PAD), F32),
                   jax.ShapeDtypeStruct((LANES, T), F32)),
        grid=(T // tm,),
        in_specs=[pl.BlockSpec((tm, D), lambda i: (i, 0)),
                  pl.BlockSpec((D, D_PROJ_PAD), lambda i: (0, 0), pipeline_mode=pl.Buffered(1)),
                  pl.BlockSpec((LANES, D), lambda i: (0, 0), pipeline_mode=pl.Buffered(1))],
        out_specs=(pl.BlockSpec((tm, D_PROJ_PAD), lambda i: (i, 0)),
                   pl.BlockSpec((LANES, tm), lambda i: (0, i))),
        compiler_params=_cparams(("parallel",)),
        name="inproj",
    )(h, w, wgt)


def _conv_kernel(prev_ref, x_ref, next_ref, w_ref, b_ref, o_ref, *, tm, tiles_per_seq):
    i = pl.program_id(0)
    pos = i % tiles_per_seq
    x = x_ref[...]
    prev = jnp.where(pos == 0, 0.0, prev_ref[...])
    nxt = jnp.where(pos == tiles_per_seq - 1, 0.0, next_ref[...])
    xe = jnp.concatenate([prev, x, nxt], axis=0)
    n = tm + 2 * SUBLANES
    acc = b_ref[...] + w_ref[2:3, :] * x
    for k in (0, 1, 3, 4):
        shifted = pltpu.roll(xe, (2 - k) % n, 0)
        acc = acc + w_ref[k:k + 1, :] * shifted[SUBLANES:SUBLANES + tm]
    o_ref[...] = _silu(acc)


def _conv_silu(proj, w, b, colblk, width, seq_len):
    T = proj.shape[0]
    tm = TM_CONV
    rb = tm // SUBLANES
    nrb = T // SUBLANES
    kern = functools.partial(_conv_kernel, tm=tm, tiles_per_seq=seq_len // tm)
    return pl.pallas_call(
        kern,
        out_shape=jax.ShapeDtypeStruct((T, width), F32),
        grid=(T // tm,),
        in_specs=[pl.BlockSpec((SUBLANES, width), lambda i: (jnp.maximum(i * rb - 1, 0), colblk)),
                  pl.BlockSpec((tm, width), lambda i: (i, colblk)),
                  pl.BlockSpec((SUBLANES, width), lambda i: (jnp.minimum((i + 1) * rb, nrb - 1), colblk)),
                  pl.BlockSpec((CONV_WIDTH, width), lambda i: (0, 0)),
                  pl.BlockSpec((1, width), lambda i: (0, 0))],
        out_specs=pl.BlockSpec((tm, width), lambda i: (i, 0)),
        compiler_params=_cparams(("parallel",)),
        name="conv_silu",
    )(proj, proj, proj, w, b.reshape(1, width))


def _cumsum_lanes(x, reverse):
    n = x.shape[1]
    r = _iota((3 * n, n), 0) % n
    c = _iota((3 * n, n), 1)
    tri3 = ((r >= c) if reverse else (r <= c)).astype(BF16)
    return _dot(_split3(x, axis=1), tri3)


def _cols_of(rows_list):
    used = sum(r.shape[0] for r in rows_list)
    pad = jnp.zeros((LANES - used, LANES), F32)
    return jnp.concatenate(list(rows_list) + [pad], axis=0).T


def _ssd_gates(gates_t, bias_cr, alog_cr, reverse):
    last = 0 if reverse else SSD_CHUNK - 1
    dt_r = _softplus(gates_t + bias_cr)
    acs_r = _cumsum_lanes(dt_r * (-jnp.exp(alog_cr)) * LOG2E, reverse)
    a_last = acs_r[:, last:last + 1]
    w_r = dt_r * jnp.exp2(a_last - acs_r)
    return acs_r - jnp.log2(dt_r), acs_r, w_r, jnp.exp2(a_last)


def _ssd_direction(x, gate_rows, acs_cols, st_ref, d, y_ref, rows):
    reverse = d == 1
    n = SSD_CHUNK
    tri = _causal(n, reverse)
    lane_lo = _iota((n, LANES), 1) < SSD_HEAD_DIM
    acs_dt_r, _, w_r, dec = gate_rows

    for g in range(SSD_GROUPS):
        b0 = SSD_INNER + g * SSD_STATE
        c0 = SSD_INNER + SSD_GROUPS * SSD_STATE + g * SSD_STATE
        bg = x[:, b0:b0 + SSD_STATE]
        cgb = x[:, c0:c0 + SSD_STATE].astype(BF16)
        cb = _dot_nt(cgb, bg.astype(BF16))
        bg_t = bg.T
        for pair in range(4):
            pi = g * 4 + pair
            p0 = pi * LANES
            xp = x[:, p0:p0 + LANES]
            h0 = g * 8 + pair * 2
            ms, bhs, xms, cols = [], [], [], []
            for hh in range(2):
                h = h0 + hh
                xms.append(jnp.where(lane_lo if hh == 0 else jnp.logical_not(lane_lo), xp, 0.0).astype(BF16))
                col = jnp.broadcast_to(acs_cols[:, d * SSD_HEADS + h:d * SSD_HEADS + h + 1], (n, n))
                decay_dt = jnp.exp2(jnp.where(tri, col - acs_dt_r[h:h + 1, :], MASK_NEG))
                ms.append((cb * decay_dt).astype(BF16))
                bhs.append((bg_t * w_r[h:h + 1, :]).astype(BF16))
                cols.append(col)
            xcat = jnp.concatenate(xms, axis=0)
            st = st_ref[d, pi]
            from_state = jnp.exp2(jnp.where(lane_lo, cols[0], cols[1])) * _dot(cgb, st.astype(BF16))
            y_ref[rows, p0:p0 + LANES] = _dot(jnp.concatenate(ms, axis=1), xcat) + from_state
            dec_row = jnp.where(lane_lo[0:1, :], dec[h0:h0 + 1, :], dec[h0 + 1:h0 + 2, :])
            st_ref[d, pi] = st * dec_row + _dot(jnp.concatenate(bhs, axis=1), xcat)


def _ssd_kernel(xf_ref, xb_ref, gtf_ref, gtb_ref, bcr_ref, acr_ref, yf_ref, yb_ref, st_ref):
    @pl.when(pl.program_id(1) == 0)
    def _():
        st_ref[...] = jnp.zeros_like(st_ref)

    nsub = SSD_STEP // SSD_CHUNK
    for k in range(nsub):
        cf, cb_ = k, nsub - 1 - k
        rf = slice(cf * SSD_CHUNK, (cf + 1) * SSD_CHUNK)
        rb = slice(cb_ * SSD_CHUNK, (cb_ + 1) * SSD_CHUNK)
        gf = _ssd_gates(gtf_ref[:, rf], bcr_ref[0:SSD_HEADS, :], acr_ref[0:SSD_HEADS, :], False)
        gb = _ssd_gates(gtb_ref[:, rb], bcr_ref[SSD_HEADS:2 * SSD_HEADS, :], acr_ref[SSD_HEADS:2 * SSD_HEADS, :], True)
        acs_cols = _cols_of([gf[1], gb[1]])
        _ssd_direction(xf_ref[rf, :], gf, acs_cols, st_ref, 0, yf_ref, rf)
        _ssd_direction(xb_ref[rb, :], gb, acs_cols, st_ref, 1, yb_ref, rb)


def _ssd_scan(xbc_act, gates_t, bias_cr, alog_cr, batch):
    T = xbc_act.shape[0]
    n = T // batch // SSD_STEP
    fwd = lambda b, i: b * n + i
    bwd = lambda b, i: b * n + (n - 1 - i)
    const = lambda b, i: (0, 0)
    return pl.pallas_call(
        _ssd_kernel,
        out_shape=(jax.ShapeDtypeStruct((T, SSD_INNER), F32),
                   jax.ShapeDtypeStruct((T, SSD_INNER), F32)),
        grid=(batch, n),
        in_specs=[pl.BlockSpec((SSD_STEP, SSD_XBC), lambda b, i: (fwd(b, i), 0)),
                  pl.BlockSpec((SSD_STEP, SSD_XBC), lambda b, i: (bwd(b, i), 0)),
                  pl.BlockSpec((SSD_HEADS, SSD_STEP), lambda b, i: (0, fwd(b, i))),
                  pl.BlockSpec((SSD_HEADS, SSD_STEP), lambda b, i: (1, bwd(b, i))),
                  pl.BlockSpec((2 * SSD_HEADS, LANES), const),
                  pl.BlockSpec((2 * SSD_HEADS, LANES), const)],
        out_specs=(pl.BlockSpec((SSD_STEP, SSD_INNER), lambda b, i: (fwd(b, i), 0)),
                   pl.BlockSpec((SSD_STEP, SSD_INNER), lambda b, i: (bwd(b, i), 0))),
        scratch_shapes=[pltpu.VMEM((2, SSD_HEADS // 2, SSD_STATE, LANES), F32)],
        compiler_params=_cparams(("arbitrary", "arbitrary")),
        name="ssd_scan",
    )(xbc_act, xbc_act, gates_t, gates_t, bias_cr, alog_cr)


def _split3(x, axis=0):
    hi = x.astype(BF16)
    r1 = x - hi.astype(F32)
    mid = r1.astype(BF16)
    lo = (r1 - mid.astype(F32)).astype(BF16)
    return jnp.concatenate([hi, mid, lo], axis=axis)


def _hgrn2_cumsum_lhs(reverse):
    n = HG_SUB
    r_i = _iota((2 * n, n), 0)
    c_i = _iota((2 * n, n), 1)
    t = jnp.where(r_i < n, r_i, r_i - n)
    ref = (t // HG_REFBLK) * HG_REFBLK + (HG_REFBLK - 1 if reverse else 0)
    upto = jnp.where(r_i < n, t, ref)
    m = ((c_i >= upto) if reverse else (c_i <= upto)).astype(BF16)
    return jnp.concatenate([m, m, m], axis=1)


def _hgrn2_intra(q, f, v, lb, qmask, reverse):
    n = HG_SUB
    nblk = n // HG_REFBLK
    slab = 2 * SUBLANES
    tri = _causal(n, reverse)
    last = 0 if reverse else n - 1
    rowblk = _iota((slab, HG_DK), 0) // HG_REFBLK

    qh = _silu(q)
    key = (1.0 - lb) / (1.0 + jnp.exp(f))
    g = jnp.log2(1.0 - key)
    gg = _dot(_hgrn2_cumsum_lhs(reverse), _split3(g))
    gc = gg[:n]
    gref = gg[n:]
    qg = (qh * jnp.exp2(gc - gref)).astype(BF16)
    qs = (qh * jnp.exp2(gc)).astype(BF16)
    g_last = gc[last:last + 1, :]
    hk = gc - jnp.log2(key)
    kd = jnp.exp2(g_last - hk).astype(BF16)
    dec = jnp.exp2(g_last)
    vb = v.astype(BF16)

    o_intra = []
    for h in range(HG_HEADS):
        cols = slice(h * HG_DK, (h + 1) * HG_DK)
        hk_h = hk[:, cols]
        gc_h = gc[:, cols]
        k_parts = [None] * nblk
        far = []
        prev_edge = prev_r = prev_near = None
        for j in (range(nblk - 1, -1, -1) if reverse else range(nblk)):
            ref_row = j * HG_REFBLK + (HG_REFBLK - 1 if reverse else 0)
            rj = gc_h[ref_row:ref_row + 1, :]
            edge = (j * HG_REFBLK) // slab
            jb = j - edge * (slab // HG_REFBLK)
            if prev_r is not None:
                if edge != prev_edge:
                    far.append(prev_near)
                step = jnp.exp2(rj - prev_r)
                far = [p * step for p in far]
            live = (rowblk >= jb) if reverse else (rowblk <= jb)
            near = jnp.exp2(jnp.where(live, rj - hk_h[edge * slab:(edge + 1) * slab], MASK_NEG))
            pieces = [near] + far[::-1] if reverse else far + [near]
            k_parts[j] = jnp.concatenate([p.astype(BF16) for p in pieces], axis=0)
            prev_edge, prev_r, prev_near = edge, rj, near
        per_slab = slab // HG_REFBLK
        att_slabs = []
        for m in range(n // slab):
            q_m = jnp.tile(qg[m * slab:(m + 1) * slab, cols], (1, per_slab)) * qmask
            k_m = jnp.concatenate(k_parts[m * per_slab:(m + 1) * per_slab], axis=1)
            att_m = _dot_nt(q_m, k_m)
            dead = jnp.zeros((slab, n - att_m.shape[1]), F32)
            if dead.shape[1]:
                att_m = jnp.concatenate([dead, att_m] if reverse else [att_m, dead], axis=1)
            att_slabs.append(att_m)
        att = jnp.where(tri, jnp.concatenate(att_slabs, axis=0), 0.0).astype(BF16)
        o_intra.append(_dot(att, vb[:, cols]))
    return o_intra, qs, kd, dec, vb


def _hgrn2_state(intra, st_ref, d, o_ref, rows):
    o_intra, qs, kd, dec, vb = intra
    for h in range(HG_HEADS):
        cols = slice(h * HG_DK, (h + 1) * HG_DK)
        st = st_ref[d, h]
        o_ref[rows, cols] = o_intra[h] + _dot_nt(qs[:, cols], st.astype(BF16))
        st_ref[d, h] = st * dec[:, cols] + _dot_tn(vb[:, cols], kd[:, cols])


def _hgrn2_kernel(qf_ref, ff_ref, vf_ref, qb_ref, fb_ref, vb_ref, lb_ref, qmask_ref, of_ref, ob_ref, st_ref):
    @pl.when(pl.program_id(1) == 0)
    def _():
        st_ref[...] = jnp.zeros_like(st_ref)

    lb = lb_ref[...]
    qmask = qmask_ref[...]
    refs = ((qf_ref, ff_ref, vf_ref, of_ref), (qb_ref, fb_ref, vb_ref, ob_ref))
    nsub = HG_STEP // HG_SUB
    units = []
    for k in range(nsub):
        units.append((0, k))
        units.append((1, nsub - 1 - k))
    pending = None
    for d, sc in units:
        rows = slice(sc * HG_SUB, (sc + 1) * HG_SUB)
        q_ref, f_ref, v_ref, o_ref = refs[d]
        intra = _hgrn2_intra(q_ref[rows, :], f_ref[rows, :], v_ref[rows, :], lb, qmask, d == 1)
        if pending is not None:
            _hgrn2_state(*pending)
        pending = (intra, st_ref, d, o_ref, rows)
    _hgrn2_state(*pending)


def _hgrn2_scan(proj, lb, batch):
    T = proj.shape[0]
    n = T // batch // HG_STEP
    fwd = lambda b, i: b * n + i
    bwd = lambda b, i: b * n + (n - 1 - i)
    blk = (HG_STEP, HG_WIDTH)
    slab = 2 * SUBLANES
    nblk = slab // HG_REFBLK
    qmask = (jnp.arange(slab)[:, None] // HG_REFBLK == jnp.arange(nblk * HG_DK)[None, :] // HG_DK).astype(BF16)
    return pl.pallas_call(
        _hgrn2_kernel,
        out_shape=(jax.ShapeDtypeStruct((T, HG_WIDTH), F32),
                   jax.ShapeDtypeStruct((T, HG_WIDTH), F32)),
        grid=(batch, n),
        in_specs=[pl.BlockSpec(blk, lambda b, i: (fwd(b, i), COLBLK_HQ)),
                  pl.BlockSpec(blk, lambda b, i: (fwd(b, i), COLBLK_HF_FWD)),
                  pl.BlockSpec(blk, lambda b, i: (fwd(b, i), COLBLK_HI)),
                  pl.BlockSpec(blk, lambda b, i: (bwd(b, i), COLBLK_HQ)),
                  pl.BlockSpec(blk, lambda b, i: (bwd(b, i), COLBLK_HF_BWD)),
                  pl.BlockSpec(blk, lambda b, i: (bwd(b, i), COLBLK_HI)),
                  pl.BlockSpec((1, HG_WIDTH), lambda b, i: (0, 0)),
                  pl.BlockSpec((slab, nblk * HG_DK), lambda b, i: (0, 0))],
        out_specs=(pl.BlockSpec(blk, lambda b, i: (fwd(b, i), 0)),
                   pl.BlockSpec(blk, lambda b, i: (bwd(b, i), 0))),
        scratch_shapes=[pltpu.VMEM((2, HG_HEADS, HG_DK, HG_DK), F32)],
        compiler_params=_cparams(("arbitrary", "arbitrary")),
        name="hgrn2_scan",
    )(proj, proj, proj, proj, proj, proj, lb.reshape(1, HG_WIDTH), qmask)


def _mlstm_kernel(batch, qkf_ref, vf_ref, qkb_ref, vb_ref, *rest):
    gt_refs = (rest[:batch], rest[batch:2 * batch])
    bcr_ref, of_ref, ob_ref, c_ref, m_ref, mlane_ref = rest[2 * batch:]

    @pl.when(pl.program_id(0) == 0)
    def _():
        c_ref[...] = jnp.zeros_like(c_ref)
        m_ref[...] = jnp.zeros_like(m_ref)
        mlane_ref[...] = jnp.zeros_like(mlane_ref)

    n = SCAN_STEP
    ones = jnp.ones((n, ML_DH), BF16)
    io_refs = ((qkf_ref, vf_ref, of_ref), (qkb_ref, vb_ref, ob_ref))
    tris = (_causal(n, False), _causal(n, True))
    row_i = _iota((n, LANES), 0)

    gates = {}
    for d in range(2):
        reverse = d == 1
        last = 0 if reverse else n - 1
        rows = []
        for b in range(batch):
            g_r = gt_refs[d][b][...] + bcr_ref[...]
            bc = _cumsum_lanes(_log_sigmoid(g_r) * LOG2E, reverse)[SUBLANES:]
            ig = g_r[:SUBLANES] * LOG2E
            rows.append((bc, ig, ig - bc))
        tb = _cols_of([r[0] for r in rows])
        pm = _cols_of([r[2] for r in rows])
        for sh in (1, 2, 4, 8, 16, 32, 64):
            shifted = pltpu.roll(pm, (n - sh) if reverse else sh, 0)
            valid = (row_i < n - sh) if reverse else (row_i >= sh)
            pm = jnp.maximum(pm, jnp.where(valid, shifted, -jnp.inf))
        m_prev_lane = mlane_ref[d][0:1, :]
        floor = jnp.where((row_i > 0) if reverse else (row_i < n - 1), MASK_NEG * LOG2E, -jnp.inf)
        mt = jnp.maximum(jnp.maximum(tb + pm, tb + m_prev_lane), floor)
        b_last_lane = tb[last:last + 1, :]
        m_new_lane = jnp.maximum(b_last_lane + m_prev_lane, b_last_lane + pm[last:last + 1, :])
        mlane_ref[d] = jnp.broadcast_to(m_new_lane, (SUBLANES, LANES))
        bm = tb - mt
        for b in range(batch):
            bc, ig, u = rows[b]
            m_prev = m_ref[b * 2 + d]
            b_last = bc[:, last:last + 1]
            m_new = jnp.maximum(b_last + m_prev[:, 0:1], b_last + jnp.max(u, axis=1, keepdims=True))
            w_s = jnp.exp2(b_last - bc + ig - m_new)
            dec = jnp.exp2(b_last + m_prev[:, 0:1] - m_new)
            m_ref[b * 2 + d] = jnp.broadcast_to(m_new, (SUBLANES, LANES))
            gates[b, d] = (u, bm, mt, m_prev, w_s, dec)

    units = [(b, d, h) for h in range(ML_HEADS) for b in range(batch) for d in range(2)]
    for b, d, h in units:
        u, bm, mt, m_prev, w_s, dec = gates[b, d]
        qk_ref, v_ref, o_ref = io_refs[d]
        r = d * ML_HEADS + h
        c = b * SUBLANES + r
        qf = qk_ref[b, :, h * ML_DH:(h + 1) * ML_DH] * (ML_DH ** -0.5)
        k = qk_ref[b, :, ML_WIDTH + h * ML_DH:ML_WIDTH + (h + 1) * ML_DH]
        ve = jnp.concatenate([v_ref[b, :, h * ML_DH:(h + 1) * ML_DH].astype(BF16), ones], axis=1)
        b_minus_m = jnp.broadcast_to(bm[:, c:c + 1], (n, n))
        m_t = jnp.broadcast_to(mt[:, c:c + 1], (n, n))
        w_intra = jnp.exp2(jnp.where(tris[d], b_minus_m + u[r:r + 1, :], MASK_NEG))
        w_inter = jnp.exp2(b_minus_m + m_prev[r:r + 1, 0:1])
        s = (_dot_nt(qf.astype(BF16), k.astype(BF16)) * w_intra).astype(BF16)
        cst = c_ref[b, d, h]
        numden = _dot(jnp.concatenate([s, (qf * w_inter).astype(BF16)], axis=1),
                      jnp.concatenate([ve, cst.astype(BF16)], axis=0))
        den = numden[:, ML_DH:]
        o_ref[b, :, h * ML_DH:(h + 1) * ML_DH] = numden[:, :ML_DH] / jnp.maximum(jnp.abs(den), jnp.exp2(-m_t))
        kw_t = (k.T * w_s[r:r + 1, :]).astype(BF16)
        c_ref[b, d, h] = dec[r:r + 1, :] * cst + _dot(kw_t, ve)


def _mlstm_scan(mqk_act, proj, gates_t, bias_cr, batch):
    T = proj.shape[0]
    seq = T // batch
    n = seq // SCAN_STEP
    gt_rowblk = GATE_IG // (2 * SUBLANES)
    fwd3 = lambda cb: (lambda i: (0, i, cb))
    bwd3 = lambda cb: (lambda i: (0, n - 1 - i, cb))
    gt_fwd = [pl.BlockSpec((2 * SUBLANES, SCAN_STEP), functools.partial(lambda b, i: (gt_rowblk, b * n + i), b))
              for b in range(batch)]
    gt_bwd = [pl.BlockSpec((2 * SUBLANES, SCAN_STEP),
                           functools.partial(lambda b, i: (gt_rowblk, b * n + n - 1 - i), b))
              for b in range(batch)]
    qk3 = mqk_act.reshape(batch, seq, 2 * ML_WIDTH)
    proj3 = proj.reshape(batch, seq, D_PROJ_PAD)
    o_f, o_b = pl.pallas_call(
        functools.partial(_mlstm_kernel, batch),
        out_shape=(jax.ShapeDtypeStruct((batch, seq, ML_WIDTH), F32),
                   jax.ShapeDtypeStruct((batch, seq, ML_WIDTH), F32)),
        grid=(n,),
        in_specs=[pl.BlockSpec((batch, SCAN_STEP, 2 * ML_WIDTH), fwd3(0)),
                  pl.BlockSpec((batch, SCAN_STEP, ML_WIDTH), fwd3(COLBLK_MV)),
                  pl.BlockSpec((batch, SCAN_STEP, 2 * ML_WIDTH), bwd3(0)),
                  pl.BlockSpec((batch, SCAN_STEP, ML_WIDTH), bwd3(COLBLK_MV))]
                 + gt_fwd + gt_bwd
                 + [pl.BlockSpec((2 * SUBLANES, LANES), lambda i: (0, 0))],
        out_specs=(pl.BlockSpec((batch, SCAN_STEP, ML_WIDTH), fwd3(0)),
                   pl.BlockSpec((batch, SCAN_STEP, ML_WIDTH), bwd3(0))),
        scratch_shapes=[pltpu.VMEM((batch, 2, ML_HEADS, ML_DH, 2 * ML_DH), F32),
                        pltpu.VMEM((batch * 2, SUBLANES, LANES), F32),
                        pltpu.VMEM((2, SUBLANES, LANES), F32)],
        compiler_params=_cparams(("arbitrary",)),
        name="mlstm_scan",
    )(qk3, proj3, qk3, proj3, *([gates_t] * (2 * batch)), bias_cr)
    return o_f.reshape(T, ML_WIDTH), o_b.reshape(T, ML_WIDTH)


def _outproj_kernel(yf_ref, yb_ref, xs_ref, z_ref, of_ref, ob_ref, hg_ref, mf_ref, mb_ref, mo_ref,
                    h_ref, dskip_ref, ssdw_ref, hgw_ref, mlw_ref, wout_ref, g_ref, b_ref, o_ref):
    y = (yf_ref[...] + yb_ref[...] + xs_ref[...] * dskip_ref[...]) * _silu(z_ref[...])
    gw = SSD_INNER // SSD_GROUPS
    acc = None
    for g in range(SSD_GROUPS):
        seg = y[:, g * gw:(g + 1) * gw]
        seg = seg * lax.rsqrt(jnp.mean(seg * seg, -1, keepdims=True) + RMS_EPS) * ssdw_ref[:, g * gw:(g + 1) * gw]
        part = _dot(seg.astype(BF16), wout_ref[g * gw:(g + 1) * gw, :])
        acc = part if acc is None else acc + part
    o = of_ref[...] + ob_ref[...]
    hgate = _silu(hg_ref[...])
    m = mf_ref[...] + mb_ref[...]
    mgate = _sigmoid(mo_ref[...])
    for h in range(HG_HEADS):
        cols = slice(h * HG_DK, (h + 1) * HG_DK)
        seg = o[:, cols]
        seg = seg * lax.rsqrt(jnp.mean(seg * seg, -1, keepdims=True) + RMS_EPS) * hgw_ref[:, cols] * hgate[:, cols]
        r0 = SSD_INNER + h * HG_DK
        acc = acc + _dot(seg.astype(BF16), wout_ref[r0:r0 + HG_DK, :])
    for h in range(ML_HEADS):
        cols = slice(h * ML_DH, (h + 1) * ML_DH)
        seg = m[:, cols]
        seg = seg - jnp.mean(seg, -1, keepdims=True)
        seg = seg * lax.rsqrt(jnp.mean(seg * seg, -1, keepdims=True) + LN_EPS) * mlw_ref[:, cols] * mgate[:, cols]
        r0 = SSD_INNER + HG_WIDTH + h * ML_DH
        acc = acc + _dot(seg.astype(BF16), wout_ref[r0:r0 + ML_DH, :])
    o_ref[...] = _layer_norm(ALPHA * h_ref[...] + acc, g_ref[...], b_ref[...])


def _outproj(y_f, y_b, xbc_act, proj, o_f, o_b, m_f, m_b, h, dskip, ssd_w, hg_w, ml_w, w_out, g, b):
    T, D = h.shape
    tm = TM_OUT
    row = lambda width, cb: pl.BlockSpec((tm, width), lambda i: (i, cb))
    par = lambda width: pl.BlockSpec((1, width), lambda i: (0, 0))
    return pl.pallas_call(
        _outproj_kernel,
        out_shape=jax.ShapeDtypeStruct((T, D), F32),
        grid=(T // tm,),
        in_specs=[row(SSD_INNER, 0), row(SSD_INNER, 0), row(SSD_INNER, 0), row(SSD_INNER, COLBLK_Z),
                  row(HG_WIDTH, 0), row(HG_WIDTH, 0), row(HG_WIDTH, COLBLK_HG),
                  row(ML_WIDTH, 0), row(ML_WIDTH, 0), row(ML_WIDTH, COLBLK_MO),
                  row(D, 0), par(SSD_INNER), par(SSD_INNER), par(HG_WIDTH), par(ML_WIDTH),
                  pl.BlockSpec((D_MIX, D), lambda i: (0, 0), pipeline_mode=pl.Buffered(1)),
                  par(D), par(D)],
        out_specs=pl.BlockSpec((tm, D), lambda i: (i, 0)),
        compiler_params=_cparams(("parallel",)),
        name="outproj_ln1",
    )(y_f, y_b, xbc_act, proj, o_f, o_b, proj, m_f, m_b, proj, h, dskip, ssd_w, hg_w, ml_w, w_out,
      g.reshape(1, D), b.reshape(1, D))


def _ffn_kernel(h_ref, w1_ref, w3_ref, w2_ref, g_ref, b_ref, o_ref, acc_ref):
    h = h_ref[...]
    hb = h.astype(BF16)
    for c in range(D_FF_DENSE // TF_FFN):
        cols = slice(c * TF_FFN, (c + 1) * TF_FFN)
        a = _dot(hb, w1_ref[:, cols])
        u = _dot(hb, w3_ref[:, cols])
        part = _dot((_silu(a) * u).astype(BF16), w2_ref[cols, :])
        if c == 0:
            acc_ref[...] = part
        else:
            acc_ref[...] += part
    o_ref[...] = _layer_norm(ALPHA * h + acc_ref[...], g_ref[...], b_ref[...])


def _dense_ffn(h, w1, w3, w2, g, b):
    T, D = h.shape
    tm = TM_FFN
    resident = lambda shape: pl.BlockSpec(shape, lambda i: (0, 0), pipeline_mode=pl.Buffered(1))
    return pl.pallas_call(
        _ffn_kernel,
        out_shape=jax.ShapeDtypeStruct((T, D), F32),
        grid=(T // tm,),
        in_specs=[pl.BlockSpec((tm, D), lambda i: (i, 0)),
                  resident((D, D_FF_DENSE)), resident((D, D_FF_DENSE)), resident((D_FF_DENSE, D)),
                  pl.BlockSpec((1, D), lambda i: (0, 0)), pl.BlockSpec((1, D), lambda i: (0, 0))],
        out_specs=pl.BlockSpec((tm, D), lambda i: (i, 0)),
        scratch_shapes=[pltpu.VMEM((tm, D), F32)],
        compiler_params=_cparams(("parallel",)),
        name="dense_ffn_ln2",
    )(h, w1, w3, w2, g.reshape(1, D), b.reshape(1, D))


def _router_kernel(h_ref, w_ref, idx_ref, gate_ref):
    logits = _dot_hi(h_ref[...], w_ref[...])
    lane = _iota(logits.shape, 1)
    lane_f = lane.astype(F32)
    x = jnp.where(lane < N_EXPERTS, logits, -jnp.inf)
    m1 = jnp.max(x, axis=1, keepdims=True)
    i1 = jnp.min(jnp.where(x == m1, lane_f, float(LANES)), axis=1, keepdims=True)
    x2 = jnp.where(lane_f == i1, -jnp.inf, x)
    m2 = jnp.max(x2, axis=1, keepdims=True)
    i2 = jnp.min(jnp.where(x2 == m2, lane_f, float(LANES)), axis=1, keepdims=True)
    e2 = jnp.exp(m2 - m1)
    g1 = 1.0 / (1.0 + e2)
    g2 = e2 / (1.0 + e2)
    idx_ref[...] = jnp.where(lane == 0, i1, jnp.where(lane == 1, i2, 0.0)).astype(jnp.int32)
    gate_ref[...] = jnp.where(lane == 0, g1, jnp.where(lane == 1, g2, 0.0))


def _router(h, w_pad):
    T, D = h.shape
    tm = TM_ROUTER
    return pl.pallas_call(
        _router_kernel,
        out_shape=(jax.ShapeDtypeStruct((T, LANES), jnp.int32),
                   jax.ShapeDtypeStruct((T, LANES), F32)),
        grid=(T // tm,),
        in_specs=[pl.BlockSpec((tm, D), lambda i: (i, 0)),
                  pl.BlockSpec((D, LANES), lambda i: (0, 0))],
        out_specs=(pl.BlockSpec((tm, LANES), lambda i: (i, 0)),
                   pl.BlockSpec((tm, LANES), lambda i: (i, 0))),
        compiler_params=_cparams(("parallel",)),
        name="moe_router",
    )(h, w_pad)


def _gather_rows(src, idx):
    n = idx.shape[0]
    D = src.shape[1]
    per_worker = n // SC_WORKERS
    n_chunks = per_worker // SC_GATHER_CHUNK
    assert n_chunks * SC_GATHER_CHUNK * SC_WORKERS == n and n_chunks % 2 == 0
    mesh = plsc.VectorSubcoreMesh(core_axis_name="c", subcore_axis_name="s")

    @functools.partial(
        pl.kernel, mesh=mesh,
        out_type=jax.ShapeDtypeStruct((n, D), src.dtype),
        scratch_types=[pltpu.VMEM((2, SC_GATHER_CHUNK), jnp.int32),
                       pltpu.VMEM((2, SC_GATHER_CHUNK, D), src.dtype),
                       pltpu.SemaphoreType.DMA((2,))],
        name="moe_row_gather",
    )
    def gather(src_hbm, idx_hbm, out_hbm, idx_v, rows_v, sems):
        worker = lax.axis_index("s") * SC_CORES + lax.axis_index("c")
        base = worker * per_worker

        def gather_copy(slot):
            return pltpu.make_async_copy(src_hbm.at[idx_v.at[slot]], rows_v.at[slot], sems.at[slot])

        def start(j, slot):
            pltpu.sync_copy(idx_hbm.at[pl.ds(base + j * SC_GATHER_CHUNK, SC_GATHER_CHUNK)], idx_v.at[slot])
            gather_copy(slot).start()

        def finish(j, slot):
            gather_copy(slot).wait()
            pltpu.sync_copy(rows_v.at[slot], out_hbm.at[pl.ds(base + j * SC_GATHER_CHUNK, SC_GATHER_CHUNK)])

        start(0, 0)

        @pl.loop(0, n_chunks, step=2)
        def _(j):
            start(j + 1, 1)
            finish(j, 0)

            @pl.when(j + 2 < n_chunks)
            def _():
                start(j + 2, 0)

            finish(j + 1, 1)

    return gather(src, idx)


def _scatter_rows(init, vals, idx):
    n_out, W = init.shape
    n = idx.shape[0]
    out_per = n_out // SC_SUBCORES
    per = n // SC_SUBCORES
    n_chunks = per // SC_SCATTER_CHUNK
    init_chunks = out_per // SC_INIT_CHUNK
    assert init_chunks * SC_INIT_CHUNK * SC_SUBCORES == n_out and n_chunks * SC_SCATTER_CHUNK * SC_SUBCORES == n
    mesh = plsc.VectorSubcoreMesh(core_axis_name="c", subcore_axis_name="s", num_cores=1)

    @functools.partial(
        pl.kernel, mesh=mesh,
        out_type=jax.ShapeDtypeStruct((n_out, W), init.dtype),
        scratch_types=[pltpu.VMEM((SC_INIT_CHUNK, W), init.dtype),
                       pltpu.VMEM((SC_SCATTER_CHUNK,), jnp.int32),
                       pltpu.VMEM((SC_SCATTER_CHUNK, W), init.dtype)],
        name="moe_row_scatter",
    )
    def scatter(init_hbm, vals_hbm, idx_hbm, out_hbm, init_v, idx_v, vals_v):
        worker = lax.axis_index("s")

        @pl.loop(0, init_chunks)
        def _(j):
            off = worker * out_per + j * SC_INIT_CHUNK
            pltpu.sync_copy(init_hbm.at[pl.ds(off, SC_INIT_CHUNK)], init_v)
            pltpu.sync_copy(init_v, out_hbm.at[pl.ds(off, SC_INIT_CHUNK)])

        plsc.subcore_barrier()

        @pl.loop(0, n_chunks)
        def _(j):
            off = worker * per + j * SC_SCATTER_CHUNK
            pltpu.sync_copy(idx_hbm.at[pl.ds(off, SC_SCATTER_CHUNK)], idx_v)
            pltpu.sync_copy(vals_hbm.at[pl.ds(off, SC_SCATTER_CHUNK)], vals_v)
            pltpu.sync_copy(vals_v, out_hbm.at[idx_v])

    return scatter(init, vals, idx)


def _expert_kernel(be_ref, nact_ref, x_ref, w1_ref, w3_ref, w2_ref, o_ref):
    i = pl.program_id(0)
    f = pl.program_id(1)

    @pl.when(f == 0)
    def _():
        o_ref[...] = jnp.zeros_like(o_ref)

    @pl.when(i < nact_ref[0])
    def _():
        xb = x_ref[...].astype(BF16)
        a = _dot(xb, w1_ref[...].astype(BF16))
        u = _dot(xb, w3_ref[...].astype(BF16))
        o_ref[...] += _dot((_silu(a) * u).astype(BF16), w2_ref[...].astype(BF16))


def _expert_ffn(xs, block_expert, n_active, w1, w3, w2, layer):
    n_rows, D = xs.shape
    n_blocks = n_rows // MOE_BM
    n_ff = D_FF_EXPERT // MOE_TF

    def widx(i, f, be, nact):
        live = i < nact[0]
        return be[jnp.minimum(i, nact[0] - 1)], jnp.where(live, f, n_ff - 1)

    def w13_map(i, f, be, nact):
        e, ff = widx(i, f, be, nact)
        return layer, e, 0, ff

    def w2_map(i, f, be, nact):
        e, ff = widx(i, f, be, nact)
        return layer, e, ff, 0

    grid_spec = pltpu.PrefetchScalarGridSpec(
        num_scalar_prefetch=2,
        grid=(n_blocks, n_ff),
        in_specs=[pl.BlockSpec((MOE_BM, D), lambda i, f, be, nact: (i, 0)),
                  pl.BlockSpec((None, None, D, MOE_TF), w13_map),
                  pl.BlockSpec((None, None, D, MOE_TF), w13_map),
                  pl.BlockSpec((None, None, MOE_TF, D), w2_map)],
        out_specs=pl.BlockSpec((MOE_BM, D), lambda i, f, be, nact: (i, 0)),
    )
    return pl.pallas_call(
        _expert_kernel,
        out_shape=jax.ShapeDtypeStruct((n_rows, D), F32),
        grid_spec=grid_spec,
        compiler_params=_cparams(("arbitrary", "arbitrary")),
        name="moe_experts",
    )(block_expert, n_active, xs, w1, w3, w2)


def _combine_kernel(y0_ref, y1_ref, gate_ref, h_ref, g_ref, b_ref, o_ref):
    gate = gate_ref[...]
    f = y0_ref[...] * gate[:, 0:1] + y1_ref[...] * gate[:, 1:2]
    o_ref[...] = _layer_norm(ALPHA * h_ref[...] + f, g_ref[...], b_ref[...])


def _combine(y_slots, gates, h, g, b):
    T, D = h.shape
    tm = TM_COMBINE
    return pl.pallas_call(
        _combine_kernel,
        out_shape=jax.ShapeDtypeStruct((T, D), F32),
        grid=(T // tm,),
        in_specs=[pl.BlockSpec((tm, D), lambda i: (i, 0)),
                  pl.BlockSpec((tm, D), lambda i: (i + T // tm, 0)),
                  pl.BlockSpec((tm, LANES), lambda i: (i, 0)),
                  pl.BlockSpec((tm, D), lambda i: (i, 0)),
                  pl.BlockSpec((1, D), lambda i: (0, 0)),
                  pl.BlockSpec((1, D), lambda i: (0, 0))],
        out_specs=pl.BlockSpec((tm, D), lambda i: (i, 0)),
        compiler_params=_cparams(("parallel",)),
        name="moe_combine_ln2",
    )(y_slots, y_slots, gates, h, g.reshape(1, D), b.reshape(1, D))


def _moe_ffn(h, router_w, w1, w3, w2, layer, g, b):
    T, D = h.shape
    n_pairs = T * TOP_K
    n_blocks = n_pairs // MOE_BM + N_EXPERTS
    n_rows = n_blocks * MOE_BM
    w_pad = jnp.pad(router_w.astype(F32), ((0, 0), (0, LANES - N_EXPERTS)))
    idx_pad, gates_pad = _router(h, w_pad)
    expert_ids = idx_pad[:, :TOP_K].reshape(-1)
    onehot = (expert_ids[:, None] == jnp.arange(N_EXPERTS, dtype=jnp.int32)[None, :]).astype(jnp.int32)
    running = jnp.cumsum(onehot, axis=0)
    counts = running[-1]
    rank = jnp.sum(onehot * running, axis=1) - 1
    padded = (counts + MOE_BM - 1) // MOE_BM * MOE_BM
    ends = jnp.cumsum(padded)
    pos = (ends - padded)[expert_ids] + rank
    token_ids = jnp.arange(n_pairs, dtype=jnp.int32) // TOP_K
    default = jnp.broadcast_to((jnp.arange(n_rows, dtype=jnp.int32) % T)[:, None], (n_rows, SC_ROW_WORDS))
    row_token = _scatter_rows(default, jnp.broadcast_to(token_ids[:, None], (n_pairs, SC_ROW_WORDS)),
                              pos.astype(jnp.int32))[:, 0]
    block_start = jnp.arange(n_blocks, dtype=jnp.int32) * MOE_BM
    block_expert = jnp.minimum(jnp.searchsorted(ends, block_start, side='right'),
                               N_EXPERTS - 1).astype(jnp.int32)
    n_active = (ends[-1:] // MOE_BM).astype(jnp.int32)

    xs = _gather_rows(h, row_token)
    ys = _expert_ffn(xs, block_expert, n_active, w1, w3, w2, layer)
    slot_major = pos.astype(jnp.int32).reshape(T, TOP_K).T.reshape(-1)
    y_slots = _gather_rows(ys, slot_major)
    return _combine(y_slots, gates_pad, h, g, b)


def _relayout_w_in(w):
    sizes = (SSD_INNER, SSD_XBC, 2 * SSD_HEADS, HG_WIDTH, 2 * HG_WIDTH, HG_WIDTH, HG_WIDTH,
             2 * ML_WIDTH, ML_WIDTH, ML_WIDTH, 2 * ML_HEADS, 2 * ML_HEADS)
    offs = [0]
    for s in sizes:
        offs.append(offs[-1] + s)
    (z, xbc, dt, hq, hf, hi, hg, mqk, mv, mo, mig, mfg) = [w[:, offs[k]:offs[k + 1]] for k in range(len(sizes))]
    pad = jnp.zeros((w.shape[0], LANES - 2 * SSD_HEADS - 4 * ML_HEADS), w.dtype)
    wn = jnp.concatenate([z, mqk, hf, xbc, hq, hi, hg, mv, mo], axis=1).astype(BF16)
    wgt = jnp.concatenate([dt, mig, mfg, pad], axis=1).astype(BF16).T
    return wn, wgt


def _gate_rows(vals_by_offset):
    row = jnp.zeros((LANES,), F32)
    for off, v in vals_by_offset:
        row = lax.dynamic_update_slice(row, v.reshape(-1).astype(F32), (off,))
    return jnp.broadcast_to(row[:, None], (LANES, LANES))


def kernel(x, ln_in_g, ln_in_b, w_in, ssd_conv_w, ssd_conv_b, ssd_dt_bias, ssd_a_log, ssd_d, ssd_norm_w, hg_lb_logits, hg_norm_w, ml_conv_w, ml_conv_b, ml_ig_bias, ml_fg_bias, ml_norm_w, w_out, ln1_g, ln1_b, ln2_g, ln2_b, ffn_w1, ffn_w3, ffn_w2, moe_router, moe_w1, moe_w3, moe_w2):
    batch, seq_len, D = x.shape
    T = batch * seq_len
    f32 = lambda t: t.astype(F32)
    lb_soft = jax.nn.softmax(f32(hg_lb_logits), axis=0)
    lb_all = jnp.cumsum(lb_soft, axis=0) - lb_soft[0]

    h = _input_ln(f32(x).reshape(T, D), f32(ln_in_g), f32(ln_in_b))
    for l in range(DEPTH):
        w, wgt = _relayout_w_in(w_in[l])
        bias_cr = _gate_rows(((GATE_DT, ssd_dt_bias[l]), (GATE_IG, ml_ig_bias[l]), (GATE_FG, ml_fg_bias[l])))
        alog_cr = _gate_rows(((GATE_DT, ssd_a_log[l]),))

        proj, gates_t = _inproj(h, w, wgt)
        xbc_act = _conv_silu(proj, f32(ssd_conv_w[l]), f32(ssd_conv_b[l]), COLBLK_XBC, SSD_XBC, seq_len)
        mqk_act = _conv_silu(proj, f32(ml_conv_w[l]), f32(ml_conv_b[l]), COLBLK_MQK, 2 * ML_WIDTH, seq_len)
        y_f, y_b = _ssd_scan(xbc_act, gates_t, bias_cr[GATE_DT:GATE_DT + 2 * SSD_HEADS],
                             alog_cr[GATE_DT:GATE_DT + 2 * SSD_HEADS], batch)
        o_f, o_b = _hgrn2_scan(proj, lb_all[l], batch)
        m_f, m_b = _mlstm_scan(mqk_act, proj, gates_t, bias_cr[GATE_IG:GATE_IG + 4 * ML_HEADS], batch)
        dskip = jnp.repeat(f32(ssd_d[l]), SSD_HEAD_DIM).reshape(1, SSD_INNER)
        h = _outproj(y_f, y_b, xbc_act, proj, o_f, o_b, m_f, m_b, h, dskip,
                     f32(ssd_norm_w[l]).reshape(1, -1), f32(hg_norm_w[l]).reshape(1, -1),
                     f32(ml_norm_w[l]).reshape(1, -1), w_out[l].astype(BF16), f32(ln1_g[l]), f32(ln1_b[l]))
        if l % 2 == 0:
            j = l // 2
            h = _dense_ffn(h, ffn_w1[j].astype(BF16), ffn_w3[j].astype(BF16), ffn_w2[j].astype(BF16),
                           f32(ln2_g[l]), f32(ln2_b[l]))
        else:
            j = l // 2
            h = _moe_ffn(h, moe_router[j], f32(moe_w1), f32(moe_w3), f32(moe_w2), j,
                         f32(ln2_g[l]), f32(ln2_b[l]))
    return h.reshape(batch, seq_len, D).astype(x.dtype)
```
